```python
import math
import jax, jax.numpy as jnp
from jax import lax
import numpy as np

D_MODEL = 1024
BATCH = 8
SEQ = 2048
DEPTH = 2
DEC_BATCH = 128
DEC_SEQ = 8
PAST_LEN = 16384
PAGE_SIZE = 128

N_EVEN = (DEPTH + 1) // 2
N_ODD = DEPTH // 2
CONV_A_DIM = 512
CONV_A_WIDTH = 31
LN_EPS = 1e-5
RET_HEADS = 4
RET_QK_DIM = 128
RET_V_DIM = 128
RET_QK = RET_HEADS * RET_QK_DIM
RET_DIM = RET_HEADS * RET_V_DIM
RET_CHUNK = 128
ROPE_BASE = 10000.0
GN_EPS = 1e-5
IN_AB_DIM = 2 * CONV_A_DIM + 2 * RET_QK + 2 * RET_DIM
SPLIT_AB = (CONV_A_DIM, 2 * CONV_A_DIM, 2 * CONV_A_DIM + RET_QK, 2 * CONV_A_DIM + 2 * RET_QK, 2 * CONV_A_DIM + 2 * RET_QK + RET_DIM)
MIX_AB_DIM = CONV_A_DIM + RET_DIM
LRU_DIM = 1024
LRU_BLOCKS = 8
LRU_BLOCK = LRU_DIM // LRU_BLOCKS
LRU_CONV_WIDTH = 4
LRU_C = 8.0
FFN_DIM = 2816
FFN_CONV_WIDTH = 3
RMS_EPS = 1e-6

kernel_name = "hybrid_conformerconv_retention_rglru_convffn_step"


def _rmsnorm(x, g):
    xf = x.astype(jnp.float32)
    y = xf * lax.rsqrt(jnp.mean(xf * xf, axis=-1, keepdims=True) + RMS_EPS) * g.astype(jnp.float32)
    return y.astype(x.dtype)


def _layernorm(x, g, b):
    xf = x.astype(jnp.float32)
    mu = jnp.mean(xf, axis=-1, keepdims=True)
    var = jnp.mean(jnp.square(xf - mu), axis=-1, keepdims=True)
    y = (xf - mu) * lax.rsqrt(var + LN_EPS) * g.astype(jnp.float32) + b.astype(jnp.float32)
    return y.astype(x.dtype)


def _causal_dwconv(x, buf, w, b):
    width = w.shape[0]
    xp = jnp.concatenate([buf.astype(x.dtype), x], axis=1)
    y = lax.conv_general_dilated(xp, w[:, None, :].astype(x.dtype), window_strides=(1,), padding='VALID',
                                 dimension_numbers=('NWC', 'WIO', 'NWC'), feature_group_count=x.shape[-1])
    return y + b.astype(x.dtype), xp[:, xp.shape[1] - (width - 1):]


def _rope(x, pos):
    d = x.shape[-1]
    inv_freq = ROPE_BASE ** (-jnp.arange(0, d, 2, dtype=jnp.float32) / d)
    ang = pos.astype(jnp.float32)[:, None] * inv_freq[None, :]
    cos = jnp.cos(ang)[None, :, None, :]
    sin = jnp.sin(ang)[None, :, None, :]
    xf = x.astype(jnp.float32)
    x1, x2 = xf[..., : d // 2], xf[..., d // 2:]
    return jnp.concatenate([x1 * cos - x2 * sin, x2 * cos + x1 * sin], axis=-1)


def _retention(q, k, v, s0):
    bsz, t, nh, dk = q.shape
    dv = v.shape[-1]
    c = RET_CHUNK if t % RET_CHUNK == 0 else t
    n = t // c
    log_gamma = jnp.log(1.0 - 2.0 ** (-5.0 - jnp.arange(nh, dtype=jnp.float32)))
    idx = jnp.arange(c, dtype=jnp.float32)
    rel = idx[:, None] - idx[None, :]
    decay_mask = jnp.where(rel >= 0, jnp.exp(jnp.maximum(rel, 0.0)[None] * log_gamma[:, None, None]), 0.0)
    q_decay = jnp.exp((idx + 1.0)[None, :] * log_gamma[:, None])
    k_decay = jnp.exp((c - 1.0 - idx)[None, :] * log_gamma[:, None])
    chunk_decay = jnp.exp(c * log_gamma)

    def to_chunks(a):
        return a.astype(jnp.float32).reshape(bsz, n, c, nh, a.shape[-1]).transpose(1, 0, 3, 2, 4)

    def step(s, inp):
        qb, kb, vb = inp
        scores = jnp.einsum('bhid,bhjd->bhij', qb, kb) * decay_mask
        inner = jnp.einsum('bhij,bhjv->bhiv', scores, vb)
        cross = jnp.einsum('bhid,bhdv->bhiv', qb, s) * q_decay[None, :, :, None]
        s_new = s * chunk_decay[None, :, None, None] + jnp.einsum('bhjd,bhjv->bhdv', kb * k_decay[None, :, :, None], vb)
        return s_new, inner + cross

    s_fin, o = lax.scan(step, s0.astype(jnp.float32), (to_chunks(q), to_chunks(k), to_chunks(v)))
    o = o.transpose(1, 0, 3, 2, 4).reshape(bsz, t, nh, dv)
    return o, s_fin


def _even_mixer(h, conv_buf, ret_state, pos0, w_in, conv_w, conv_b, ln_g, ln_b, gn_g, w_out):
    bsz, t, _ = h.shape
    z = h @ w_in
    a_lin, a_gate, q, k, v, g = jnp.split(z, SPLIT_AB, axis=-1)
    u = a_lin * jax.nn.sigmoid(a_gate)
    cv, new_conv_buf = _causal_dwconv(u, conv_buf, conv_w, conv_b)
    ya = jax.nn.silu(_layernorm(cv, ln_g, ln_b))
    pos = pos0 + jnp.arange(t, dtype=jnp.int32)
    qh = _rope(q.reshape(bsz, t, RET_HEADS, RET_QK_DIM), pos)
    kh = _rope(k.reshape(bsz, t, RET_HEADS, RET_QK_DIM), pos) * (RET_QK_DIM ** -0.5)
    vh = v.reshape(bsz, t, RET_HEADS, RET_V_DIM)
    o, new_ret = _retention(qh, kh, vh, ret_state)
    mu = jnp.mean(o, axis=-1, keepdims=True)
    var = jnp.mean(jnp.square(o - mu), axis=-1, keepdims=True)
    o = ((o - mu) * lax.rsqrt(var + GN_EPS)).reshape(bsz, t, RET_DIM) * gn_g.astype(jnp.float32)
    yb = o.astype(h.dtype) * jax.nn.silu(g)
    y = jnp.concatenate([ya, yb], axis=-1) @ w_out
    return y, new_conv_buf, new_ret


def _lru_combine(left, right):
    a1, b1 = left
    a2, b2 = right
    return a1 * a2, a2 * b1 + b2


def _odd_mixer(h, conv_buf, h0, w_in, conv_w, conv_b, w_a, b_a, w_x, b_x, lam, w_out):
    bsz, t, _ = h.shape
    z = h @ w_in
    gate_br, rec_br = jnp.split(z, 2, axis=-1)
    xc, new_conv_buf = _causal_dwconv(rec_br, conv_buf, conv_w, conv_b)
    xb = xc.reshape(bsz, t, LRU_BLOCKS, LRU_BLOCK)
    r = jax.nn.sigmoid(jnp.einsum('btnc,ncd->btnd', xb, w_a).reshape(bsz, t, LRU_DIM) + b_a).astype(jnp.float32)
    i = jax.nn.sigmoid(jnp.einsum('btnc,ncd->btnd', xb, w_x).reshape(bsz, t, LRU_DIM) + b_x).astype(jnp.float32)
    log_a = -LRU_C * r * jax.nn.softplus(-lam.astype(jnp.float32))
    a = jnp.exp(log_a)
    bterm = jnp.sqrt(jnp.maximum(1.0 - a * a, 0.0)) * (i * xc.astype(jnp.float32))
    a_cum, b_cum = lax.associative_scan(_lru_combine, (a, bterm), axis=1)
    hs = a_cum * h0.astype(jnp.float32)[:, None, :] + b_cum
    y = (hs.astype(h.dtype) * jax.nn.gelu(gate_br)) @ w_out
    return y, new_conv_buf, hs[:, -1]


def _conv_ffn(h, buf, w_up, conv_w, conv_b, w_down):
    z = h @ w_up
    zc, new_buf = _causal_dwconv(z, buf, conv_w, conv_b)
    g, u = jnp.split(zc, 2, axis=-1)
    return (jax.nn.gelu(g) * u) @ w_down, new_buf


def _trunk(x, conv_a_buf, ret_st, lru_conv_buf, lru_h, ffn_buf, pos0, p):
    new_conv_a, new_ret, new_lru_conv, new_lru_h, new_ffn = [], [], [], [], []
    for l in range(DEPTH):
        hn = _rmsnorm(x, p['norm_mix'][l])
        if l % 2 == 0:
            e = l // 2
            y, cb, rs = _even_mixer(hn, conv_a_buf[e], ret_st[e], pos0, p['w_in_ab'][e], p['conv_a_w'][e], p['conv_a_b'][e],
                                    p['ln_a_g'][e], p['ln_a_b'][e], p['gn_ret_g'][e], p['w_out_ab'][e])
            new_conv_a.append(cb.astype(x.dtype))
            new_ret.append(rs.astype(x.dtype))
        else:
            o = l // 2
            y, cb, hl = _odd_mixer(hn, lru_conv_buf[o], lru_h[o], p['w_in_c'][o], p['conv_c_w'][o], p['conv_c_b'][o],
                                   p['w_lru_a'][o], p['b_lru_a'][o], p['w_lru_x'][o], p['b_lru_x'][o], p['lru_lambda'][o], p['w_out_c'][o])
            new_lru_conv.append(cb.astype(x.dtype))
            new_lru_h.append(hl.astype(x.dtype))
        x = x + y
        hn = _rmsnorm(x, p['norm_ffn'][l])
        y, fb = _conv_ffn(hn, ffn_buf[l], p['w_ffn_up'][l], p['ffn_conv_w'][l], p['ffn_conv_b'][l], p['w_ffn_down'][l])
        new_ffn.append(fb.astype(x.dtype))
        x = x + y
    x = _rmsnorm(x, p['norm_final'])
    return x, jnp.stack(new_conv_a), jnp.stack(new_ret), jnp.stack(new_lru_conv), jnp.stack(new_lru_h), jnp.stack(new_ffn)


def setup_inputs(seed: int = 0) -> dict:
    key = jax.random.key(seed)
    ks = iter(jax.random.split(key, 40))
    f32 = jnp.float32

    def nrm(shape, scale):
        return jax.random.normal(next(ks), shape, f32) * scale

    u = jax.random.uniform(next(ks), (N_ODD, LRU_DIM), f32, minval=0.9, maxval=0.999)
    s = u ** (1.0 / LRU_C)
    lam = jnp.log(s) - jnp.log1p(-s)
    return {
        'x_prompt': nrm((BATCH, SEQ, D_MODEL), 1.0),
        'x_sample': nrm((DEC_BATCH, DEC_SEQ, D_MODEL), 1.0),
        'state_conv_a': nrm((N_EVEN, DEC_BATCH, CONV_A_WIDTH - 1, CONV_A_DIM), 0.5),
        'state_ret': nrm((N_EVEN, DEC_BATCH, RET_HEADS, RET_QK_DIM, RET_V_DIM), 0.5),
        'state_lru_conv': nrm((N_ODD, DEC_BATCH, LRU_CONV_WIDTH - 1, LRU_DIM), 1.0),
        'state_lru_h': nrm((N_ODD, DEC_BATCH, LRU_DIM), 0.5),
        'state_ffn_conv': nrm((DEPTH, DEC_BATCH, FFN_CONV_WIDTH - 1, 2 * FFN_DIM), 1.0),
        'norm_mix': 1.0 + nrm((DEPTH, D_MODEL), 0.02),
        'norm_ffn': 1.0 + nrm((DEPTH, D_MODEL), 0.02),
        'norm_final': 1.0 + nrm((D_MODEL,), 0.02),
        'w_in_ab': nrm((N_EVEN, D_MODEL, IN_AB_DIM), D_MODEL ** -0.5),
        'conv_a_w': nrm((N_EVEN, CONV_A_WIDTH, CONV_A_DIM), CONV_A_WIDTH ** -0.5),
        'conv_a_b': nrm((N_EVEN, CONV_A_DIM), 0.02),
        'ln_a_g': 1.0 + nrm((N_EVEN, CONV_A_DIM), 0.02),
        'ln_a_b': nrm((N_EVEN, CONV_A_DIM), 0.02),
        'gn_ret_g': 1.0 + nrm((N_EVEN, RET_DIM), 0.02),
        'w_out_ab': nrm((N_EVEN, MIX_AB_DIM, D_MODEL), MIX_AB_DIM ** -0.5),
        'w_in_c': nrm((N_ODD, D_MODEL, 2 * LRU_DIM), D_MODEL ** -0.5),
        'conv_c_w': nrm((N_ODD, LRU_CONV_WIDTH, LRU_DIM), LRU_CONV_WIDTH ** -0.5),
        'conv_c_b': nrm((N_ODD, LRU_DIM), 0.02),
        'w_lru_a': nrm((N_ODD, LRU_BLOCKS, LRU_BLOCK, LRU_BLOCK), LRU_BLOCK ** -0.5),
        'b_lru_a': nrm((N_ODD, LRU_DIM), 0.02),
        'w_lru_x': nrm((N_ODD, LRU_BLOCKS, LRU_BLOCK, LRU_BLOCK), LRU_BLOCK ** -0.5),
        'b_lru_x': nrm((N_ODD, LRU_DIM), 0.02),
        'lru_lambda': lam,
        'w_out_c': nrm((N_ODD, LRU_DIM, D_MODEL), LRU_DIM ** -0.5),
        'w_ffn_up': nrm((DEPTH, D_MODEL, 2 * FFN_DIM), D_MODEL ** -0.5),
        'ffn_conv_w': nrm((DEPTH, FFN_CONV_WIDTH, 2 * FFN_DIM), FFN_CONV_WIDTH ** -0.5),
        'ffn_conv_b': nrm((DEPTH, 2 * FFN_DIM), 0.02),
        'w_ffn_down': nrm((DEPTH, FFN_DIM, D_MODEL), FFN_DIM ** -0.5),
    }


def reference(x_prompt, x_sample, state_conv_a, state_ret, state_lru_conv, state_lru_h, state_ffn_conv,
              norm_mix, norm_ffn, norm_final, w_in_ab, conv_a_w, conv_a_b, ln_a_g, ln_a_b, gn_ret_g, w_out_ab,
              w_in_c, conv_c_w, conv_c_b, w_lru_a, b_lru_a, w_lru_x, b_lru_x, lru_lambda, w_out_c,
              w_ffn_up, ffn_conv_w, ffn_conv_b, w_ffn_down):
    p = {'norm_mix': norm_mix, 'norm_ffn': norm_ffn, 'norm_final': norm_final,
         'w_in_ab': w_in_ab, 'conv_a_w': conv_a_w, 'conv_a_b': conv_a_b, 'ln_a_g': ln_a_g, 'ln_a_b': ln_a_b,
         'gn_ret_g': gn_ret_g, 'w_out_ab': w_out_ab, 'w_in_c': w_in_c, 'conv_c_w': conv_c_w, 'conv_c_b': conv_c_b,
         'w_lru_a': w_lru_a, 'b_lru_a': b_lru_a, 'w_lru_x': w_lru_x, 'b_lru_x': b_lru_x, 'lru_lambda': lru_lambda,
         'w_out_c': w_out_c, 'w_ffn_up': w_ffn_up, 'ffn_conv_w': ffn_conv_w, 'ffn_conv_b': ffn_conv_b,
         'w_ffn_down': w_ffn_down}
    bp = x_prompt.shape[0]
    dt = x_prompt.dtype
    y_prompt, p_conv_a, p_ret, p_lru_conv, p_lru_h, p_ffn = _trunk(
        x_prompt,
        jnp.zeros((N_EVEN, bp, CONV_A_WIDTH - 1, CONV_A_DIM), dt),
        jnp.zeros((N_EVEN, bp, RET_HEADS, RET_QK_DIM, RET_V_DIM), jnp.float32),
        jnp.zeros((N_ODD, bp, LRU_CONV_WIDTH - 1, LRU_DIM), dt),
        jnp.zeros((N_ODD, bp, LRU_DIM), jnp.float32),
        jnp.zeros((DEPTH, bp, FFN_CONV_WIDTH - 1, 2 * FFN_DIM), dt),
        0, p)
    y_sample, s_conv_a, s_ret, s_lru_conv, s_lru_h, s_ffn = _trunk(
        x_sample, state_conv_a, state_ret, state_lru_conv, state_lru_h, state_ffn_conv, PAST_LEN, p)
    return (y_prompt, y_sample, p_conv_a, p_ret, p_lru_conv, p_lru_h, p_ffn, s_conv_a, s_ret, s_lru_conv, s_lru_h, s_ffn)
```

```python
import functools
import math

import jax
import jax.numpy as jnp
from jax import lax
from jax.experimental import pallas as pl
from jax.experimental.pallas import tpu as pltpu

F32 = jnp.float32
BF16 = jnp.bfloat16

D_MODEL = 1024
CONV_A_DIM = 512
CONV_A_WIDTH = 31
CONV_A_HIST = CONV_A_WIDTH - 1
CONV_A_HIST_PAD = 32
LN_EPS = 1e-5
RET_HEADS = 4
RET_HEAD_DIM = 128
RET_DIM = RET_HEADS * RET_HEAD_DIM
RET_CHUNK = 128
ROPE_BASE = 10000.0
GN_EPS = 1e-5
LRU_DIM = 1024
LRU_BLOCKS = 8
LRU_BLOCK = LRU_DIM // LRU_BLOCKS
LRU_CONV_WIDTH = 4
LRU_C = 8.0
FFN_DIM = 2816
FFN_CONV_WIDTH = 3
RMS_EPS = 1e-6
SUBLANES = 8
SAMPLE_GROUP = RET_CHUNK // SUBLANES

VMEM_LIMIT_BYTES = 56 * 1024 * 1024


def _rms(x, g):
    return x * lax.rsqrt(jnp.mean(x * x, axis=-1, keepdims=True) + RMS_EPS) * g


def _sigmoid(x):
    return 1.0 / (1.0 + jnp.exp(-x))


def _gelu(x):
    c = math.sqrt(2.0 / math.pi)
    return 0.5 * x * (1.0 + jnp.tanh(c * (x + 0.044715 * (x * x * x))))


def _mm(a, b):
    return jnp.dot(a, b, preferred_element_type=F32)


def _const_spec(shape):
    nd = len(shape)
    return pl.BlockSpec(shape, lambda b, t: (0,) * nd, pipeline_mode=pl.Buffered(1))


def _even_kernel(*refs, S, TT, nt, sample):
    it = iter(refs)
    x_ref = next(it); cos_ref = next(it); sin_ref = next(it)
    dmask_ref = next(it); qd_ref = next(it); kd_ref = next(it); cd_ref = next(it)
    gmix_ref = next(it); win_ref = next(it); cw_ref = next(it); cb_ref = next(it)
    lng_ref = next(it); lnb_ref = next(it); gng_ref = next(it); wout_ref = next(it)
    if sample:
        convst_ref = next(it); retst_ref = next(it)
    y_ref = next(it); convout_ref = next(it); retout_ref = next(it)
    xp_s = next(it); q_s = next(it); k_s = next(it); v_s = next(it); o_s = next(it)

    t = pl.program_id(1)
    R = S * TT
    HP = CONV_A_HIST_PAD
    H = CONV_A_HIST
    x = x_ref[...].reshape(R, D_MODEL)
    h = _rms(x, gmix_ref[...]).astype(BF16)
    z = _mm(h, win_ref[...])

    @pl.when(t == 0)
    def _():
        xp_s[:, 0:HP, :] = jnp.zeros((S, HP, CONV_A_DIM), F32)
        if sample:
            xp_s[:, HP - H:HP, :] = convst_ref[...]
            retout_ref[...] = retst_ref[...]
        else:
            retout_ref[...] = jnp.zeros(retout_ref.shape, F32)

    u = z[:, 0:CONV_A_DIM] * _sigmoid(z[:, CONV_A_DIM:2 * CONV_A_DIM])
    xp_s[:, HP:HP + TT, :] = u.reshape(S, TT, CONV_A_DIM)
    acc = jnp.zeros((S, TT, CONV_A_DIM), F32) + cb_ref[...]
    for j in range(CONV_A_WIDTH):
        acc = acc + cw_ref[j:j + 1, :] * xp_s[:, HP - H + j:HP - H + j + TT, :]
    cv = acc.reshape(R, CONV_A_DIM)
    xp_s[:, 0:HP, :] = xp_s[:, TT:TT + HP, :]

    @pl.when(t == nt - 1)
    def _():
        convout_ref[...] = xp_s[:, HP - H:HP, :]

    mu = jnp.mean(cv, axis=-1, keepdims=True)
    cvc = cv - mu
    var = jnp.mean(cvc * cvc, axis=-1, keepdims=True)
    ln = cvc * lax.rsqrt(var + LN_EPS) * lng_ref[...] + lnb_ref[...]
    ya = ln * _sigmoid(ln)

    cos = cos_ref[...]
    sin = sin_ref[...]
    o0 = 2 * CONV_A_DIM
    scale = RET_HEAD_DIM ** -0.5
    for hh in range(RET_HEADS):
        lo = hh * RET_HEAD_DIM
        hi = lo + RET_HEAD_DIM
        qh = z[:, o0 + lo:o0 + hi]
        kh = z[:, o0 + RET_DIM + lo:o0 + RET_DIM + hi]
        q_s[:, lo:hi] = qh * cos + pltpu.roll(qh, RET_HEAD_DIM // 2, axis=1) * sin
        k_s[:, lo:hi] = (kh * cos + pltpu.roll(kh, RET_HEAD_DIM // 2, axis=1) * sin) * scale
    v_s[...] = z[:, o0 + 2 * RET_DIM:o0 + 3 * RET_DIM]
    gate = z[:, o0 + 3 * RET_DIM:o0 + 4 * RET_DIM]

    C = RET_CHUNK
    n_chunks = R // C

    def chunk_body(ci, carry):
        r0 = pl.multiple_of(ci * C, C)
        for hh in range(RET_HEADS):
            lo = hh * RET_HEAD_DIM
            hi = lo + RET_HEAD_DIM
            qc = q_s[pl.ds(r0, C), lo:hi]
            kc = k_s[pl.ds(r0, C), lo:hi]
            vc = v_s[pl.ds(r0, C), lo:hi]
            qb = qc.astype(BF16)
            vb = vc.astype(BF16)
            scores = lax.dot_general(qb, kc.astype(BF16), (((1,), (1,)), ((), ())),
                                     preferred_element_type=F32) * dmask_ref[hh]
            inner = _mm(scores.astype(BF16), vb)
            if not sample:
                kdv = (kc * kd_ref[hh]).astype(BF16)
                st = retout_ref[0, hh]
                cross = _mm(qb, st.astype(BF16))
                upd = lax.dot_general(kdv, vb, (((0,), (0,)), ((), ())), preferred_element_type=F32)
                retout_ref[0, hh] = st * cd_ref[hh] + upd
            else:
                parts = []
                for sq in range(SAMPLE_GROUP):
                    sidx = ci * SAMPLE_GROUP + sq
                    rows = slice(sq * SUBLANES, (sq + 1) * SUBLANES)
                    st = retout_ref[sidx, hh]
                    parts.append(_mm(qc[rows].astype(BF16), st.astype(BF16)))
                    kdv_s = (kc[rows] * kd_ref[hh, rows, :]).astype(BF16)
                    upd = lax.dot_general(kdv_s, vc[rows].astype(BF16), (((0,), (0,)), ((), ())),
                                          preferred_element_type=F32)
                    retout_ref[sidx, hh] = st * cd_ref[hh] + upd
                cross = jnp.concatenate(parts, axis=0)
            o = inner + cross * qd_ref[hh]
            mu_o = jnp.mean(o, axis=-1, keepdims=True)
            oc = o - mu_o
            var_o = jnp.mean(oc * oc, axis=-1, keepdims=True)
            o_s[pl.ds(r0, C), lo:hi] = oc * lax.rsqrt(var_o + GN_EPS) * gng_ref[:, lo:hi]
        return carry

    lax.fori_loop(0, n_chunks, chunk_body, 0)

    yb = o_s[...] * (gate * _sigmoid(gate))
    y = (_mm(ya.astype(BF16), wout_ref[0:CONV_A_DIM, :])
         + _mm(yb.astype(BF16), wout_ref[CONV_A_DIM:CONV_A_DIM + RET_DIM, :]) + x)
    y_ref[...] = y.reshape(S, TT, D_MODEL)


def _rope_tables(pos0, T, reps):
    d = RET_HEAD_DIM
    inv_freq = ROPE_BASE ** (-jnp.arange(0, d, 2, dtype=F32) / d)
    pos = (pos0 + jnp.arange(T, dtype=jnp.int32)).astype(F32)
    ang = pos[:, None] * inv_freq[None, :]
    cos = jnp.cos(ang)
    sin = jnp.sin(ang)
    cos2 = jnp.concatenate([cos, cos], axis=-1)
    sin2 = jnp.concatenate([-sin, sin], axis=-1)
    return jnp.tile(cos2, (reps, 1)), jnp.tile(sin2, (reps, 1))


def _decay_tables(c, groups):
    nh = RET_HEADS
    log_gamma = jnp.log(1.0 - 2.0 ** (-5.0 - jnp.arange(nh, dtype=F32)))
    idx = jnp.arange(c, dtype=F32)
    rel = idx[:, None] - idx[None, :]
    dmask = jnp.where(rel >= 0, jnp.exp(jnp.maximum(rel, 0.0)[None] * log_gamma[:, None, None]), 0.0)
    qd = jnp.exp((idx + 1.0)[None, :] * log_gamma[:, None])
    kd = jnp.exp((c - 1.0 - idx)[None, :] * log_gamma[:, None])
    cd = jnp.exp(c * log_gamma)
    if groups > 1:
        eye = jnp.eye(groups, dtype=F32)
        dmask = jnp.einsum('gk,hij->hgikj', eye, dmask).reshape(nh, groups * c, groups * c)
        qd = jnp.tile(qd, (1, groups))
        kd = jnp.tile(kd, (1, groups))
    n = groups * c
    qd = jnp.broadcast_to(qd[:, :, None], (nh, n, RET_HEAD_DIM))
    kd = jnp.broadcast_to(kd[:, :, None], (nh, n, RET_HEAD_DIM))
    cd = jnp.broadcast_to(cd[:, None, None], (nh, 1, RET_HEAD_DIM))
    return dmask, qd, kd, cd


def _even_mixer(x, conv_state, ret_state, pos0, p, *, S, TT):
    B, T, D = x.shape
    sample = conv_state is not None
    nb, nt = B // S, T // TT
    R = S * TT
    if sample:
        assert TT == T == SUBLANES and S % SAMPLE_GROUP == 0
        cos, sin = _rope_tables(pos0, T, S)
        dmask, qd, kd, cd = _decay_tables(T, SAMPLE_GROUP)
    else:
        assert S == 1 and TT % RET_CHUNK == 0
        cos, sin = _rope_tables(pos0, T, 1)
        dmask, qd, kd, cd = _decay_tables(RET_CHUNK, 1)
    consts = [dmask, qd, kd, cd, p['gmix'], p['w_in'], p['conv_w'], p['conv_b'],
              p['ln_g'], p['ln_b'], p['gn_g'], p['w_out']]
    in_specs = [pl.BlockSpec((S, TT, D), lambda b, t: (b, t, 0)),
                pl.BlockSpec((R, RET_HEAD_DIM), lambda b, t: (t, 0)),
                pl.BlockSpec((R, RET_HEAD_DIM), lambda b, t: (t, 0))]
    in_specs += [_const_spec(c.shape) for c in consts]
    args = [x, cos, sin] + consts
    if sample:
        in_specs += [pl.BlockSpec((S, CONV_A_HIST, CONV_A_DIM), lambda b, t: (b, 0, 0)),
                     pl.BlockSpec((S, RET_HEADS, RET_HEAD_DIM, RET_HEAD_DIM), lambda b, t: (b, 0, 0, 0))]
        args += [conv_state, ret_state]
    out_shape = (jax.ShapeDtypeStruct((B, T, D), F32),
                 jax.ShapeDtypeStruct((B, CONV_A_HIST, CONV_A_DIM), F32),
                 jax.ShapeDtypeStruct((B, RET_HEADS, RET_HEAD_DIM, RET_HEAD_DIM), F32))
    out_specs = (pl.BlockSpec((S, TT, D), lambda b, t: (b, t, 0)),
                 pl.BlockSpec((S, CONV_A_HIST, CONV_A_DIM), lambda b, t: (b, 0, 0)),
                 pl.BlockSpec((S, RET_HEADS, RET_HEAD_DIM, RET_HEAD_DIM), lambda b, t: (b, 0, 0, 0)))
    scratch = [pltpu.VMEM((S, CONV_A_HIST_PAD + TT, CONV_A_DIM), F32)]
    scratch += [pltpu.VMEM((R, RET_DIM), F32) for _ in range(4)]
    return pl.pallas_call(
        functools.partial(_even_kernel, S=S, TT=TT, nt=nt, sample=sample),
        grid=(nb, nt), in_specs=in_specs, out_specs=out_specs, out_shape=out_shape,
        scratch_shapes=scratch,
        compiler_params=pltpu.CompilerParams(dimension_semantics=("parallel", "arbitrary"),
                                             vmem_limit_bytes=VMEM_LIMIT_BYTES),
        name="even_mixer_sample" if sample else "even_mixer_prompt",
    )(*args)


def _odd_kernel(*refs, S, TT, nt, sample):
    it = iter(refs)
    x_ref = next(it); gmix_ref = next(it); win_ref = next(it); cw_ref = next(it); cb_ref = next(it)
    wa_ref = next(it); ba_ref = next(it); wx_ref = next(it); bx_ref = next(it)
    lam_ref = next(it); wout_ref = next(it)
    if sample:
        convst_ref = next(it); hst_ref = next(it)
    y_ref = next(it); convout_ref = next(it); hout_ref = next(it)
    xp_s = next(it)

    t = pl.program_id(1)
    R = S * TT
    HP = SUBLANES
    H = LRU_CONV_WIDTH - 1
    x = x_ref[...].reshape(R, D_MODEL)
    h = _rms(x, gmix_ref[...]).astype(BF16)
    gate_br = _mm(h, win_ref[:, 0:LRU_DIM])
    rec_br = _mm(h, win_ref[:, LRU_DIM:2 * LRU_DIM])

    @pl.when(t == 0)
    def _():
        xp_s[:, 0:HP, :] = jnp.zeros((S, HP, LRU_DIM), F32)
        if sample:
            xp_s[:, HP - H:HP, :] = convst_ref[...]
            hout_ref[...] = hst_ref[...]
        else:
            hout_ref[...] = jnp.zeros(hout_ref.shape, F32)

    rec3 = rec_br.reshape(S, TT, LRU_DIM)
    xp_s[:, HP:HP + TT, :] = rec3
    acc = cw_ref[H:H + 1, :] * rec3 + cb_ref[...]
    for j in range(H):
        acc = acc + cw_ref[j:j + 1, :] * xp_s[:, HP - H + j:HP - H + j + TT, :]
    xc = acc.reshape(R, LRU_DIM)
    xp_s[:, 0:HP, :] = xp_s[:, TT:TT + HP, :]

    @pl.when(t == nt - 1)
    def _():
        convout_ref[...] = xp_s[:, HP - H:HP, :]

    xcb = xc.astype(BF16)
    r_parts = []
    i_parts = []
    for n in range(LRU_BLOCKS):
        blk = xcb[:, n * LRU_BLOCK:(n + 1) * LRU_BLOCK]
        r_parts.append(_mm(blk, wa_ref[n]))
        i_parts.append(_mm(blk, wx_ref[n]))
    r = _sigmoid(jnp.concatenate(r_parts, axis=-1) + ba_ref[...])
    ig = _sigmoid(jnp.concatenate(i_parts, axis=-1) + bx_ref[...])
    nlam = -lam_ref[...]
    softplus = jnp.maximum(nlam, 0.0) + jnp.log(1.0 + jnp.exp(-jnp.abs(nlam)))
    a = jnp.exp((-LRU_C) * r * softplus)
    bt = jnp.sqrt(jnp.maximum(1.0 - a * a, 0.0)) * (ig * xc)

    assert TT & (TT - 1) == 0
    tpos = lax.broadcasted_iota(jnp.int32, (R, LRU_DIM), 0) & (TT - 1)
    sh = 1
    while sh < TT:
        keep = tpos >= sh
        a_sh = jnp.where(keep, pltpu.roll(a, sh, axis=0), 1.0)
        b_sh = jnp.where(keep, pltpu.roll(bt, sh, axis=0), 0.0)
        bt = a * b_sh + bt
        a = a * a_sh
        sh *= 2
    hs3 = a.reshape(S, TT, LRU_DIM) * hout_ref[...] + bt.reshape(S, TT, LRU_DIM)
    hout_ref[...] = hs3[:, TT - 1:TT, :]
    hs = hs3.reshape(R, LRU_DIM)
    y = _mm((hs * _gelu(gate_br)).astype(BF16), wout_ref[...]) + x
    y_ref[...] = y.reshape(S, TT, D_MODEL)


def _odd_mixer(x, conv_state, h_state, p, *, S, TT):
    B, T, D = x.shape
    sample = conv_state is not None
    nb, nt = B // S, T // TT
    H = LRU_CONV_WIDTH - 1
    consts = [p['gmix'], p['w_in'], p['conv_w'], p['conv_b'], p['w_a'], p['b_a'],
              p['w_x'], p['b_x'], p['lam'], p['w_out']]
    in_specs = [pl.BlockSpec((S, TT, D), lambda b, t: (b, t, 0))]
    in_specs += [_const_spec(c.shape) for c in consts]
    args = [x] + consts
    if sample:
        in_specs += [pl.BlockSpec((S, H, LRU_DIM), lambda b, t: (b, 0, 0)),
                     pl.BlockSpec((S, 1, LRU_DIM), lambda b, t: (b, 0, 0))]
        args += [conv_state, h_state]
    out_shape = (jax.ShapeDtypeStruct((B, T, D), F32),
                 jax.ShapeDtypeStruct((B, H, LRU_DIM), F32),
                 jax.ShapeDtypeStruct((B, 1, LRU_DIM), F32))
    out_specs = (pl.BlockSpec((S, TT, D), lambda b, t: (b, t, 0)),
                 pl.BlockSpec((S, H, LRU_DIM), lambda b, t: (b, 0, 0)),
                 pl.BlockSpec((S, 1, LRU_DIM), lambda b, t: (b, 0, 0)))
    scratch = [pltpu.VMEM((S, SUBLANES + TT, LRU_DIM), F32)]
    return pl.pallas_call(
        functools.partial(_odd_kernel, S=S, TT=TT, nt=nt, sample=sample),
        grid=(nb, nt), in_specs=in_specs, out_specs=out_specs, out_shape=out_shape,
        scratch_shapes=scratch,
        compiler_params=pltpu.CompilerParams(dimension_semantics=("parallel", "arbitrary"),
                                             vmem_limit_bytes=VMEM_LIMIT_BYTES),
        name="odd_mixer_sample" if sample else "odd_mixer_prompt",
    )(*args)


FFN_COL_CHUNK = 256


def _ffn_kernel(*refs, S, TT, nt, sample, final):
    it = iter(refs)
    x_ref = next(it); g_ref = next(it); wup_ref = next(it); cw_ref = next(it); cb_ref = next(it)
    wdn_ref = next(it)
    gfin_ref = next(it) if final else None
    st_ref = next(it) if sample else None
    y_ref = next(it); stout_ref = next(it)
    h_s = next(it); act_s = next(it); hist_s = next(it); work_s = next(it)

    t = pl.program_id(1)
    R = S * TT
    HP = SUBLANES
    H = FFN_CONV_WIDTH - 1
    CK = FFN_COL_CHUNK
    x = x_ref[...].reshape(R, D_MODEL)
    h_s[...] = _rms(x, g_ref[...]).astype(BF16)

    @pl.when(t == 0)
    def _():
        hist_s[...] = jnp.zeros(hist_s.shape, F32)
        if sample:
            hist_s[:, HP - H:HP, :] = st_ref[...]

    def conv_cols(col):
        z3 = _mm(h_s[...], wup_ref[:, col:col + CK]).reshape(S, TT, CK)
        work_s[:, 0:HP, :] = hist_s[:, :, col:col + CK]
        work_s[:, HP:HP + TT, :] = z3
        zc = cw_ref[H:H + 1, col:col + CK] * z3 + cb_ref[:, col:col + CK]
        for j in range(H):
            zc = zc + cw_ref[j:j + 1, col:col + CK] * work_s[:, HP - H + j:HP - H + j + TT, :]
        hist_s[:, :, col:col + CK] = work_s[:, TT:TT + HP, :]
        return zc.reshape(R, CK)

    for c in range(FFN_DIM // CK):
        gz = conv_cols(c * CK)
        uz = conv_cols(FFN_DIM + c * CK)
        act_s[:, c * CK:(c + 1) * CK] = (_gelu(gz) * uz).astype(BF16)

    y = _mm(act_s[...], wdn_ref[...]) + x
    if final:
        y = _rms(y, gfin_ref[...])
    y_ref[...] = y.reshape(S, TT, D_MODEL)

    @pl.when(t == nt - 1)
    def _():
        stout_ref[...] = hist_s[:, HP - H:HP, :]


def _conv_ffn(x, state, p, g_final, *, S, TT):
    B, T, D = x.shape
    sample = state is not None
    final = g_final is not None
    nb, nt = B // S, T // TT
    R = S * TT
    H = FFN_CONV_WIDTH - 1
    consts = [p['g'], p['w_up'], p['conv_w'], p['conv_b'], p['w_down']]
    if final:
        consts.append(g_final)
    in_specs = [pl.BlockSpec((S, TT, D), lambda b, t: (b, t, 0))]
    in_specs += [_const_spec(c.shape) for c in consts]
    args = [x] + consts
    if sample:
        in_specs.append(pl.BlockSpec((S, H, 2 * FFN_DIM), lambda b, t: (b, 0, 0)))
        args.append(state)
    out_shape = (jax.ShapeDtypeStruct((B, T, D), F32),
                 jax.ShapeDtypeStruct((B, H, 2 * FFN_DIM), F32))
    out_specs = (pl.BlockSpec((S, TT, D), lambda b, t: (b, t, 0)),
                 pl.BlockSpec((S, H, 2 * FFN_DIM), lambda b, t: (b, 0, 0)))
    scratch = [pltpu.VMEM((R, D), BF16),
               pltpu.VMEM((R, FFN_DIM), BF16),
               pltpu.VMEM((S, SUBLANES, 2 * FFN_DIM), F32),
               pltpu.VMEM((S, SUBLANES + TT, FFN_COL_CHUNK), F32)]
    return pl.pallas_call(
        functools.partial(_ffn_kernel, S=S, TT=TT, nt=nt, sample=sample, final=final),
        grid=(nb, nt), in_specs=in_specs, out_specs=out_specs, out_shape=out_shape,
        scratch_shapes=scratch,
        compiler_params=pltpu.CompilerParams(dimension_semantics=("parallel", "arbitrary"),
                                             vmem_limit_bytes=VMEM_LIMIT_BYTES),
        name=("ffn_sample" if sample else "ffn_prompt") + ("_final" if final else ""),
    )(*args)


def _row(v):
    return v.reshape(1, -1)


def _trunk(x, states, pos0, pe, po, pf, g_final, tiles):
    conv_a, ret, lru_conv, lru_h, ffn = states
    sample = conv_a is not None
    x, n_conv_a, n_ret = _even_mixer(x, conv_a[0] if sample else None, ret[0] if sample else None,
                                     pos0, pe, **tiles['even'])
    x, n_ffn0 = _conv_ffn(x, ffn[0] if sample else None, pf[0], None, **tiles['ffn'])
    x, n_lru_conv, n_lru_h = _odd_mixer(x, lru_conv[0] if sample else None,
                                        lru_h[0][:, None, :] if sample else None, po, **tiles['odd'])
    x, n_ffn1 = _conv_ffn(x, ffn[1] if sample else None, pf[1], g_final, **tiles['ffn'])
    return (x, n_conv_a[None], n_ret[None], n_lru_conv[None], n_lru_h[:, 0, :][None],
            jnp.stack([n_ffn0, n_ffn1]))


def kernel(x_prompt, x_sample, state_conv_a, state_ret, state_lru_conv, state_lru_h, state_ffn_conv, norm_mix, norm_ffn, norm_final, w_in_ab, conv_a_w, conv_a_b, ln_a_g, ln_a_b, gn_ret_g, w_out_ab, w_in_c, conv_c_w, conv_c_b, w_lru_a, b_lru_a, w_lru_x, b_lru_x, lru_lambda, w_out_c, w_ffn_up, ffn_conv_w, ffn_conv_b, w_ffn_down):
    past_len = 16384
    pe = dict(gmix=_row(norm_mix[0]), w_in=w_in_ab[0].astype(BF16), conv_w=conv_a_w[0],
              conv_b=_row(conv_a_b[0]), ln_g=_row(ln_a_g[0]), ln_b=_row(ln_a_b[0]),
              gn_g=_row(gn_ret_g[0]), w_out=w_out_ab[0].astype(BF16))
    po = dict(gmix=_row(norm_mix[1]), w_in=w_in_c[0].astype(BF16), conv_w=conv_c_w[0],
              conv_b=_row(conv_c_b[0]), w_a=w_lru_a[0].astype(BF16), b_a=_row(b_lru_a[0]),
              w_x=w_lru_x[0].astype(BF16), b_x=_row(b_lru_x[0]), lam=_row(lru_lambda[0]),
              w_out=w_out_c[0].astype(BF16))
    pf = [dict(g=_row(norm_ffn[l]), w_up=w_ffn_up[l].astype(BF16), conv_w=ffn_conv_w[l],
               conv_b=_row(ffn_conv_b[l]), w_down=w_ffn_down[l].astype(BF16)) for l in range(2)]
    g_final = _row(norm_final)

    prompt_tiles = dict(even=dict(S=1, TT=256), odd=dict(S=1, TT=256), ffn=dict(S=1, TT=512))
    sample_tiles = dict(even=dict(S=16, TT=8), odd=dict(S=32, TT=8), ffn=dict(S=32, TT=8))
    outs_p = _trunk(x_prompt, (None,) * 5, 0, pe, po, pf, g_final, prompt_tiles)
    outs_s = _trunk(x_sample, (state_conv_a, state_ret, state_lru_conv, state_lru_h, state_ffn_conv),
                    past_len, pe, po, pf, g_final, sample_tiles)
    return (outs_p[0], outs_s[0]) + tuple(outs_p[1:]) + tuple(outs_s[1:])
```

```python
import functools
import math

import jax
import jax.numpy as jnp
from jax import lax
from jax.experimental import pallas as pl
from jax.experimental.pallas import tpu as pltpu

F32 = jnp.float32
BF16 = jnp.bfloat16

D_MODEL = 1024
CONV_A_DIM = 512
CONV_A_WIDTH = 31
CONV_A_HIST = CONV_A_WIDTH - 1
CONV_A_HIST_PAD = 32
CONV_A_ROW_BLOCK = 32
CONV_A_SEQ_BLOCK = 4
LN_EPS = 1e-5
RET_HEADS = 4
RET_HEAD_DIM = 128
RET_DIM = RET_HEADS * RET_HEAD_DIM
RET_CHUNK = 128
ROPE_BASE = 10000.0
GN_EPS = 1e-5
LRU_DIM = 1024
LRU_BLOCKS = 8
LRU_BLOCK = LRU_DIM // LRU_BLOCKS
LRU_CONV_WIDTH = 4
LRU_C = 8.0
ODD_SUB_ROWS = 128
FFN_DIM = 2816
FFN_CONV_WIDTH = 3
RMS_EPS = 1e-6
SUBLANES = 8
SAMPLE_GROUP = RET_CHUNK // SUBLANES

VMEM_LIMIT_BYTES = 56 * 1024 * 1024


def _rms(x, g):
    return x * lax.rsqrt(jnp.mean(x * x, axis=-1, keepdims=True) + RMS_EPS) * g


def _sigmoid(x):
    return 1.0 / (1.0 + jnp.exp(-x))


def _gelu(x):
    c = math.sqrt(2.0 / math.pi)
    return 0.5 * x * (1.0 + jnp.tanh(c * (x + 0.044715 * (x * x * x))))


def _mm(a, b):
    return jnp.dot(a, b, preferred_element_type=F32)


def _const_spec(shape):
    nd = len(shape)
    return pl.BlockSpec(shape, lambda b, t: (0,) * nd, pipeline_mode=pl.Buffered(1))


def _even_kernel(*refs, S, TT, nt, sample):
    it = iter(refs)
    x_ref = next(it); cos_ref = next(it); sin_ref = next(it)
    dmask_ref = next(it); qd_ref = next(it); kd_ref = next(it); cd_ref = next(it)
    gmix_ref = next(it); win_ref = next(it); cw_ref = next(it); cb_ref = next(it)
    lng_ref = next(it); lnb_ref = next(it); gng_ref = next(it); wout_ref = next(it)
    if sample:
        convst_ref = next(it); retst_ref = next(it)
    y_ref = next(it); convout_ref = next(it); retout_ref = next(it)
    xp_s = next(it); xs_s = next(it); ya_s = next(it)

    t = pl.program_id(1)
    HP = CONV_A_HIST_PAD
    H = CONV_A_HIST

    @pl.when(t == 0)
    def _():
        xp_s[:, 0:HP, :] = jnp.zeros((S, HP, CONV_A_DIM), F32)
        if sample:
            xp_s[:, HP - H:HP, :] = convst_ref[...]
            retout_ref[...] = retst_ref[...]
        else:
            retout_ref[...] = jnp.zeros(retout_ref.shape, F32)

    if sample:
        tiles = [(s0, SAMPLE_GROUP, 0, TT) for s0 in range(0, S, SAMPLE_GROUP)]
    else:
        tiles = [(0, 1, r0, RET_CHUNK) for r0 in range(0, TT, RET_CHUNK)]
    o0 = 2 * CONV_A_DIM
    scale = RET_HEAD_DIM ** -0.5
    for s0, ns, r0, nr in tiles:
        rows = ns * nr
        row0 = s0 * TT + r0
        x = x_ref[s0:s0 + ns, r0:r0 + nr, :].reshape(rows, D_MODEL)
        h = _rms(x, gmix_ref[...]).astype(BF16)
        z = _mm(h, win_ref[...])

        u = z[:, 0:CONV_A_DIM] * _sigmoid(z[:, CONV_A_DIM:2 * CONV_A_DIM])
        xp_s[s0:s0 + ns, HP + r0:HP + r0 + nr, :] = u.reshape(ns, nr, CONV_A_DIM)
        L = nr + HP - SUBLANES
        for b in range(1, SUBLANES):
            xs_s[b - 1, s0:s0 + ns, r0:r0 + L, :] = xp_s[s0:s0 + ns, r0 + b:r0 + b + L, :]
        if sample:
            blocks = [(sb, CONV_A_SEQ_BLOCK, 0, TT) for sb in range(s0, s0 + ns, CONV_A_SEQ_BLOCK)]
        else:
            blocks = [(0, 1, rb, CONV_A_ROW_BLOCK) for rb in range(r0, r0 + nr, CONV_A_ROW_BLOCK)]
        for sb, nsb, rb, nrb in blocks:
            acc = jnp.zeros((nsb, nrb, CONV_A_DIM), F32) + cb_ref[...]
            for j in range(CONV_A_WIDTH):
                off = HP - H + j
                b = off % SUBLANES
                lo = off - b + rb
                if b == 0:
                    win = xp_s[sb:sb + nsb, lo:lo + nrb, :]
                else:
                    win = xs_s[b - 1, sb:sb + nsb, lo:lo + nrb, :]
                acc = acc + cw_ref[j:j + 1, :] * win
            cv = acc.reshape(nsb * nrb, CONV_A_DIM)
            mu = jnp.mean(cv, axis=-1, keepdims=True)
            cvc = cv - mu
            var = jnp.mean(cvc * cvc, axis=-1, keepdims=True)
            ln = cvc * lax.rsqrt(var + LN_EPS) * lng_ref[...] + lnb_ref[...]
            rowb = sb * TT + rb
            ya_s[rowb:rowb + nsb * nrb, :] = (ln * _sigmoid(ln)).astype(BF16)

        cos = cos_ref[row0:row0 + rows, :]
        sin = sin_ref[row0:row0 + rows, :]
        gate = z[:, o0 + 3 * RET_DIM:o0 + 4 * RET_DIM]
        o_parts = []
        for hh in range(RET_HEADS):
            lo = hh * RET_HEAD_DIM
            hi = lo + RET_HEAD_DIM
            qh = z[:, o0 + lo:o0 + hi]
            kh = z[:, o0 + RET_DIM + lo:o0 + RET_DIM + hi]
            qc = qh * cos + pltpu.roll(qh, RET_HEAD_DIM // 2, axis=1) * sin
            kc = (kh * cos + pltpu.roll(kh, RET_HEAD_DIM // 2, axis=1) * sin) * scale
            vc = z[:, o0 + 2 * RET_DIM + lo:o0 + 2 * RET_DIM + hi]
            qb = qc.astype(BF16)
            vb = vc.astype(BF16)
            scores = lax.dot_general(qb, kc.astype(BF16), (((1,), (1,)), ((), ())),
                                     preferred_element_type=F32) * dmask_ref[hh]
            inner = _mm(scores.astype(BF16), vb)
            if not sample:
                kdv = (kc * kd_ref[hh]).astype(BF16)
                st = retout_ref[0, hh]
                cross = _mm(qb, st.astype(BF16))
                upd = lax.dot_general(kdv, vb, (((0,), (0,)), ((), ())), preferred_element_type=F32)
                retout_ref[0, hh] = st * cd_ref[hh] + upd
            else:
                parts = []
                for sq in range(SAMPLE_GROUP):
                    sidx = s0 + sq
                    rs = slice(sq * SUBLANES, (sq + 1) * SUBLANES)
                    st = retout_ref[sidx, hh]
                    parts.append(_mm(qc[rs].astype(BF16), st.astype(BF16)))
                    kdv_s = (kc[rs] * kd_ref[hh, rs, :]).astype(BF16)
                    upd = lax.dot_general(kdv_s, vc[rs].astype(BF16), (((0,), (0,)), ((), ())),
                                          preferred_element_type=F32)
                    retout_ref[sidx, hh] = st * cd_ref[hh] + upd
                cross = jnp.concatenate(parts, axis=0)
            o = inner + cross * qd_ref[hh]
            mu_o = jnp.mean(o, axis=-1, keepdims=True)
            oc = o - mu_o
            var_o = jnp.mean(oc * oc, axis=-1, keepdims=True)
            o_parts.append(oc * lax.rsqrt(var_o + GN_EPS) * gng_ref[:, lo:hi])

        yb = jnp.concatenate(o_parts, axis=-1) * (gate * _sigmoid(gate))
        y = (_mm(ya_s[row0:row0 + rows, :], wout_ref[0:CONV_A_DIM, :])
             + _mm(yb.astype(BF16), wout_ref[CONV_A_DIM:CONV_A_DIM + RET_DIM, :]) + x)
        y_ref[s0:s0 + ns, r0:r0 + nr, :] = y.reshape(ns, nr, D_MODEL)

    xp_s[:, 0:HP, :] = xp_s[:, TT:TT + HP, :]

    @pl.when(t == nt - 1)
    def _():
        convout_ref[...] = xp_s[:, HP - H:HP, :]


def _rope_tables(pos0, T, reps):
    d = RET_HEAD_DIM
    inv_freq = ROPE_BASE ** (-jnp.arange(0, d, 2, dtype=F32) / d)
    pos = (pos0 + jnp.arange(T, dtype=jnp.int32)).astype(F32)
    ang = pos[:, None] * inv_freq[None, :]
    cos = jnp.cos(ang)
    sin = jnp.sin(ang)
    cos2 = jnp.concatenate([cos, cos], axis=-1)
    sin2 = jnp.concatenate([-sin, sin], axis=-1)
    return jnp.tile(cos2, (reps, 1)), jnp.tile(sin2, (reps, 1))


def _decay_tables(c, groups):
    nh = RET_HEADS
    log_gamma = jnp.log(1.0 - 2.0 ** (-5.0 - jnp.arange(nh, dtype=F32)))
    idx = jnp.arange(c, dtype=F32)
    rel = idx[:, None] - idx[None, :]
    dmask = jnp.where(rel >= 0, jnp.exp(jnp.maximum(rel, 0.0)[None] * log_gamma[:, None, None]), 0.0)
    qd = jnp.exp((idx + 1.0)[None, :] * log_gamma[:, None])
    kd = jnp.exp((c - 1.0 - idx)[None, :] * log_gamma[:, None])
    cd = jnp.exp(c * log_gamma)
    if groups > 1:
        eye = jnp.eye(groups, dtype=F32)
        dmask = jnp.einsum('gk,hij->hgikj', eye, dmask).reshape(nh, groups * c, groups * c)
        qd = jnp.tile(qd, (1, groups))
        kd = jnp.tile(kd, (1, groups))
    n = groups * c
    qd = jnp.broadcast_to(qd[:, :, None], (nh, n, RET_HEAD_DIM))
    kd = jnp.broadcast_to(kd[:, :, None], (nh, n, RET_HEAD_DIM))
    cd = jnp.broadcast_to(cd[:, None, None], (nh, 1, RET_HEAD_DIM))
    return dmask, qd, kd, cd


def _even_mixer(x, conv_state, ret_state, pos0, p, *, S, TT):
    B, T, D = x.shape
    sample = conv_state is not None
    nb, nt = B // S, T // TT
    R = S * TT
    if sample:
        assert TT == T == SUBLANES and S % SAMPLE_GROUP == 0
        cos, sin = _rope_tables(pos0, T, S)
        dmask, qd, kd, cd = _decay_tables(T, SAMPLE_GROUP)
    else:
        assert S == 1 and TT % RET_CHUNK == 0
        cos, sin = _rope_tables(pos0, T, 1)
        dmask, qd, kd, cd = _decay_tables(RET_CHUNK, 1)
    consts = [dmask, qd, kd, cd, p['gmix'], p['w_in'], p['conv_w'], p['conv_b'],
              p['ln_g'], p['ln_b'], p['gn_g'], p['w_out']]
    in_specs = [pl.BlockSpec((S, TT, D), lambda b, t: (b, t, 0)),
                pl.BlockSpec((R, RET_HEAD_DIM), lambda b, t: (t, 0)),
                pl.BlockSpec((R, RET_HEAD_DIM), lambda b, t: (t, 0))]
    in_specs += [_const_spec(c.shape) for c in consts]
    args = [x, cos, sin] + consts
    if sample:
        in_specs += [pl.BlockSpec((S, CONV_A_HIST, CONV_A_DIM), lambda b, t: (b, 0, 0)),
                     pl.BlockSpec((S, RET_HEADS, RET_HEAD_DIM, RET_HEAD_DIM), lambda b, t: (b, 0, 0, 0))]
        args += [conv_state, ret_state]
    out_shape = (jax.ShapeDtypeStruct((B, T, D), F32),
                 jax.ShapeDtypeStruct((B, CONV_A_HIST, CONV_A_DIM), F32),
                 jax.ShapeDtypeStruct((B, RET_HEADS, RET_HEAD_DIM, RET_HEAD_DIM), F32))
    out_specs = (pl.BlockSpec((S, TT, D), lambda b, t: (b, t, 0)),
                 pl.BlockSpec((S, CONV_A_HIST, CONV_A_DIM), lambda b, t: (b, 0, 0)),
                 pl.BlockSpec((S, RET_HEADS, RET_HEAD_DIM, RET_HEAD_DIM), lambda b, t: (b, 0, 0, 0)))
    scratch = [pltpu.VMEM((S, CONV_A_HIST_PAD + TT, CONV_A_DIM), F32),
               pltpu.VMEM((SUBLANES - 1, S, CONV_A_HIST_PAD + TT - SUBLANES, CONV_A_DIM), F32),
               pltpu.VMEM((R, CONV_A_DIM), BF16)]
    return pl.pallas_call(
        functools.partial(_even_kernel, S=S, TT=TT, nt=nt, sample=sample),
        grid=(nb, nt), in_specs=in_specs, out_specs=out_specs, out_shape=out_shape,
        scratch_shapes=scratch,
        compiler_params=pltpu.CompilerParams(dimension_semantics=("parallel", "arbitrary"),
                                             vmem_limit_bytes=VMEM_LIMIT_BYTES),
        name="even_mixer_sample" if sample else "even_mixer_prompt",
    )(*args)


def _odd_kernel(*refs, S, TT, nt, sample):
    it = iter(refs)
    x_ref = next(it); gmix_ref = next(it); win_ref = next(it); cw_ref = next(it); cb_ref = next(it)
    wax_ref = next(it); ba_ref = next(it); bx_ref = next(it)
    lam_ref = next(it); wout_ref = next(it)
    if sample:
        convst_ref = next(it); hst_ref = next(it)
    y_ref = next(it); convout_ref = next(it); hout_ref = next(it)
    xp_s = next(it); gate_s = next(it); xc_s = next(it); r_s = next(it); i_s = next(it)
    act_s = next(it)

    t = pl.program_id(1)
    R = S * TT
    HP = SUBLANES
    H = LRU_CONV_WIDTH - 1

    @pl.when(t == 0)
    def _():
        xp_s[:, 0:HP, :] = jnp.zeros((S, HP, LRU_DIM), F32)
        if sample:
            xp_s[:, HP - H:HP, :] = convst_ref[...]
            hout_ref[...] = hst_ref[...]
        else:
            hout_ref[...] = jnp.zeros(hout_ref.shape, F32)

    nlam = -lam_ref[...]
    softplus = jnp.maximum(nlam, 0.0) + jnp.log(1.0 + jnp.exp(-jnp.abs(nlam)))
    decay = jnp.broadcast_to((-LRU_C) * softplus, (SUBLANES, LRU_DIM))
    sub = lax.broadcasted_iota(jnp.int32, (SUBLANES, LRU_DIM), 0)

    def scan_group(rows, h_prev):
        r = _sigmoid(r_s[rows, :])
        ig = _sigmoid(i_s[rows, :])
        a = jnp.exp(decay * r)
        b = jnp.sqrt(jnp.maximum(1.0 - a * a, 0.0)) * (ig * xc_s[rows, :])
        for sh in (1, 2, 4):
            keep = sub >= sh
            a_sh = jnp.where(keep, pltpu.roll(a, sh, axis=0), 1.0)
            b_sh = jnp.where(keep, pltpu.roll(b, sh, axis=0), 0.0)
            b = a * b_sh + b
            a = a * a_sh
        return a * h_prev + b

    if sample:
        tiles = [(0, S, 0, TT)]
    else:
        tiles = [(0, 1, r0, min(TT, ODD_SUB_ROWS)) for r0 in range(0, TT, ODD_SUB_ROWS)]
    h_prev = jnp.broadcast_to(hout_ref[0], (SUBLANES, LRU_DIM))
    pair = 2 * LRU_BLOCK
    for s0, ns, r0, nr in tiles:
        rows = ns * nr
        row0 = s0 * TT + r0
        x = x_ref[s0:s0 + ns, r0:r0 + nr, :].reshape(rows, D_MODEL)
        h = _rms(x, gmix_ref[...]).astype(BF16)
        gate_s[row0:row0 + rows, :] = _mm(h, win_ref[:, 0:LRU_DIM])
        rec3 = _mm(h, win_ref[:, LRU_DIM:2 * LRU_DIM]).reshape(ns, nr, LRU_DIM)
        xp_s[s0:s0 + ns, HP + r0:HP + r0 + nr, :] = rec3
        acc = cw_ref[H:H + 1, :] * rec3 + cb_ref[...]
        for j in range(H):
            lo = HP - H + j + r0
            acc = acc + cw_ref[j:j + 1, :] * xp_s[s0:s0 + ns, lo:lo + nr, :]
        xc = acc.reshape(rows, LRU_DIM)
        xc_s[row0:row0 + rows, :] = xc
        xcb = xc.astype(BF16)
        for p in range(LRU_BLOCKS // 2):
            cs = slice(pair * p, pair * (p + 1))
            ri = _mm(xcb[:, cs], wax_ref[p])
            r_s[row0:row0 + rows, cs] = ri[:, 0:pair] + ba_ref[:, cs]
            i_s[row0:row0 + rows, cs] = ri[:, pair:2 * pair] + bx_ref[:, cs]
        for g0 in range(row0, row0 + rows, 2 * SUBLANES):
            parts = []
            for k in range(2):
                ga = g0 + k * SUBLANES
                if sample:
                    h_prev = jnp.broadcast_to(hout_ref[ga // TT], (SUBLANES, LRU_DIM))
                hs = scan_group(slice(ga, ga + SUBLANES), h_prev)
                h_last = hs[SUBLANES - 1:SUBLANES, :]
                if sample:
                    hout_ref[ga // TT] = h_last
                h_prev = jnp.broadcast_to(h_last, (SUBLANES, LRU_DIM))
                parts.append(hs)
            rows2 = slice(g0, g0 + 2 * SUBLANES)
            act_s[rows2, :] = (jnp.concatenate(parts, axis=0) * _gelu(gate_s[rows2, :])).astype(BF16)
        y = _mm(act_s[row0:row0 + rows, :], wout_ref[...]) + x
        y_ref[s0:s0 + ns, r0:r0 + nr, :] = y.reshape(ns, nr, D_MODEL)
    if not sample:
        hout_ref[0] = h_prev[0:1, :]
    xp_s[:, 0:HP, :] = xp_s[:, TT:TT + HP, :]

    @pl.when(t == nt - 1)
    def _():
        convout_ref[...] = xp_s[:, HP - H:HP, :]


def _odd_mixer(x, conv_state, h_state, p, *, S, TT):
    B, T, D = x.shape
    sample = conv_state is not None
    nb, nt = B // S, T // TT
    H = LRU_CONV_WIDTH - 1
    consts = [p['gmix'], p['w_in'], p['conv_w'], p['conv_b'], p['w_ax'], p['b_a'],
              p['b_x'], p['lam'], p['w_out']]
    in_specs = [pl.BlockSpec((S, TT, D), lambda b, t: (b, t, 0))]
    in_specs += [_const_spec(c.shape) for c in consts]
    args = [x] + consts
    if sample:
        in_specs += [pl.BlockSpec((S, H, LRU_DIM), lambda b, t: (b, 0, 0)),
                     pl.BlockSpec((S, 1, LRU_DIM), lambda b, t: (b, 0, 0))]
        args += [conv_state, h_state]
    out_shape = (jax.ShapeDtypeStruct((B, T, D), F32),
                 jax.ShapeDtypeStruct((B, H, LRU_DIM), F32),
                 jax.ShapeDtypeStruct((B, 1, LRU_DIM), F32))
    out_specs = (pl.BlockSpec((S, TT, D), lambda b, t: (b, t, 0)),
                 pl.BlockSpec((S, H, LRU_DIM), lambda b, t: (b, 0, 0)),
                 pl.BlockSpec((S, 1, LRU_DIM), lambda b, t: (b, 0, 0)))
    R = S * TT
    scratch = [pltpu.VMEM((S, SUBLANES + TT, LRU_DIM), F32)]
    scratch += [pltpu.VMEM((R, LRU_DIM), F32) for _ in range(4)]
    scratch += [pltpu.VMEM((R, LRU_DIM), BF16)]
    return pl.pallas_call(
        functools.partial(_odd_kernel, S=S, TT=TT, nt=nt, sample=sample),
        grid=(nb, nt), in_specs=in_specs, out_specs=out_specs, out_shape=out_shape,
        scratch_shapes=scratch,
        compiler_params=pltpu.CompilerParams(dimension_semantics=("parallel", "arbitrary"),
                                             vmem_limit_bytes=VMEM_LIMIT_BYTES),
        name="odd_mixer_sample" if sample else "odd_mixer_prompt",
    )(*args)


FFN_COL_CHUNK = 256


def _ffn_kernel(*refs, S, TT, nt, sample, final):
    it = iter(refs)
    x_ref = next(it); g_ref = next(it); wup_ref = next(it); cw_ref = next(it); cb_ref = next(it)
    wdn_ref = next(it)
    gfin_ref = next(it) if final else None
    st_ref = next(it) if sample else None
    y_ref = next(it); stout_ref = next(it)
    h_s = next(it); act_s = next(it); hist_s = next(it); work_s = next(it)

    t = pl.program_id(1)
    R = S * TT
    HP = SUBLANES
    H = FFN_CONV_WIDTH - 1
    CK = FFN_COL_CHUNK
    x = x_ref[...].reshape(R, D_MODEL)
    h_s[...] = _rms(x, g_ref[...]).astype(BF16)

    @pl.when(t == 0)
    def _():
        hist_s[...] = jnp.zeros(hist_s.shape, F32)
        if sample:
            hist_s[:, HP - H:HP, :] = st_ref[...]

    def conv_cols(col):
        z3 = _mm(h_s[...], wup_ref[:, col:col + CK]).reshape(S, TT, CK)
        work_s[:, 0:HP, :] = hist_s[:, :, col:col + CK]
        work_s[:, HP:HP + TT, :] = z3
        zc = cw_ref[H:H + 1, col:col + CK] * z3 + cb_ref[:, col:col + CK]
        for j in range(H):
            zc = zc + cw_ref[j:j + 1, col:col + CK] * work_s[:, HP - H + j:HP - H + j + TT, :]
        hist_s[:, :, col:col + CK] = work_s[:, TT:TT + HP, :]
        return zc.reshape(R, CK)

    for c in range(FFN_DIM // CK):
        gz = conv_cols(c * CK)
        uz = conv_cols(FFN_DIM + c * CK)
        act_s[:, c * CK:(c + 1) * CK] = (_gelu(gz) * uz).astype(BF16)

    y = _mm(act_s[...], wdn_ref[...]) + x
    if final:
        y = _rms(y, gfin_ref[...])
    y_ref[...] = y.reshape(S, TT, D_MODEL)

    @pl.when(t == nt - 1)
    def _():
        stout_ref[...] = hist_s[:, HP - H:HP, :]


def _conv_ffn(x, state, p, g_final, *, S, TT):
    B, T, D = x.shape
    sample = state is not None
    final = g_final is not None
    nb, nt = B // S, T // TT
    R = S * TT
    H = FFN_CONV_WIDTH - 1
    consts = [p['g'], p['w_up'], p['conv_w'], p['conv_b'], p['w_down']]
    if final:
        consts.append(g_final)
    in_specs = [pl.BlockSpec((S, TT, D), lambda b, t: (b, t, 0))]
    in_specs += [_const_spec(c.shape) for c in consts]
    args = [x] + consts
    if sample:
        in_specs.append(pl.BlockSpec((S, H, 2 * FFN_DIM), lambda b, t: (b, 0, 0)))
        args.append(state)
    out_shape = (jax.ShapeDtypeStruct((B, T, D), F32),
                 jax.ShapeDtypeStruct((B, H, 2 * FFN_DIM), F32))
    out_specs = (pl.BlockSpec((S, TT, D), lambda b, t: (b, t, 0)),
                 pl.BlockSpec((S, H, 2 * FFN_DIM), lambda b, t: (b, 0, 0)))
    scratch = [pltpu.VMEM((R, D), BF16),
               pltpu.VMEM((R, FFN_DIM), BF16),
               pltpu.VMEM((S, SUBLANES, 2 * FFN_DIM), F32),
               pltpu.VMEM((S, SUBLANES + TT, FFN_COL_CHUNK), F32)]
    return pl.pallas_call(
        functools.partial(_ffn_kernel, S=S, TT=TT, nt=nt, sample=sample, final=final),
        grid=(nb, nt), in_specs=in_specs, out_specs=out_specs, out_shape=out_shape,
        scratch_shapes=scratch,
        compiler_params=pltpu.CompilerParams(dimension_semantics=("parallel", "arbitrary"),
                                             vmem_limit_bytes=VMEM_LIMIT_BYTES),
        name=("ffn_sample" if sample else "ffn_prompt") + ("_final" if final else ""),
    )(*args)


def _row(v):
    return v.reshape(1, -1)


def _pair_block_diag(w_a, w_x):
    def pairs(w):
        w = w.reshape(LRU_BLOCKS // 2, 2, LRU_BLOCK, LRU_BLOCK)
        z = jnp.zeros_like(w[:, 0])
        top = jnp.concatenate([w[:, 0], z], axis=-1)
        bot = jnp.concatenate([z, w[:, 1]], axis=-1)
        return jnp.concatenate([top, bot], axis=-2)
    return jnp.concatenate([pairs(w_a), pairs(w_x)], axis=-1).astype(BF16)


def _trunk(x, states, pos0, pe, po, pf, g_final, tiles):
    conv_a, ret, lru_conv, lru_h, ffn = states
    sample = conv_a is not None
    x, n_conv_a, n_ret = _even_mixer(x, conv_a[0] if sample else None, ret[0] if sample else None,
                                     pos0, pe, **tiles['even'])
    x, n_ffn0 = _conv_ffn(x, ffn[0] if sample else None, pf[0], None, **tiles['ffn'])
    x, n_lru_conv, n_lru_h = _odd_mixer(x, lru_conv[0] if sample else None,
                                        lru_h[0][:, None, :] if sample else None, po, **tiles['odd'])
    x, n_ffn1 = _conv_ffn(x, ffn[1] if sample else None, pf[1], g_final, **tiles['ffn'])
    return (x, n_conv_a[None], n_ret[None], n_lru_conv[None], n_lru_h[:, 0, :][None],
            jnp.stack([n_ffn0, n_ffn1]))


def kernel(x_prompt, x_sample, state_conv_a, state_ret, state_lru_conv, state_lru_h, state_ffn_conv, norm_mix, norm_ffn, norm_final, w_in_ab, conv_a_w, conv_a_b, ln_a_g, ln_a_b, gn_ret_g, w_out_ab, w_in_c, conv_c_w, conv_c_b, w_lru_a, b_lru_a, w_lru_x, b_lru_x, lru_lambda, w_out_c, w_ffn_up, ffn_conv_w, ffn_conv_b, w_ffn_down):
    past_len = 16384
    pe = dict(gmix=_row(norm_mix[0]), w_in=w_in_ab[0].astype(BF16), conv_w=conv_a_w[0],
              conv_b=_row(conv_a_b[0]), ln_g=_row(ln_a_g[0]), ln_b=_row(ln_a_b[0]),
              gn_g=_row(gn_ret_g[0]), w_out=w_out_ab[0].astype(BF16))
    po = dict(gmix=_row(norm_mix[1]), w_in=w_in_c[0].astype(BF16), conv_w=conv_c_w[0],
              conv_b=_row(conv_c_b[0]), w_ax=_pair_block_diag(w_lru_a[0], w_lru_x[0]),
              b_a=_row(b_lru_a[0]), b_x=_row(b_lru_x[0]), lam=_row(lru_lambda[0]),
              w_out=w_out_c[0].astype(BF16))
    pf = [dict(g=_row(norm_ffn[l]), w_up=w_ffn_up[l].astype(BF16), conv_w=ffn_conv_w[l],
               conv_b=_row(ffn_conv_b[l]), w_down=w_ffn_down[l].astype(BF16)) for l in range(2)]
    g_final = _row(norm_final)

    prompt_tiles = dict(even=dict(S=1, TT=512), odd=dict(S=1, TT=512), ffn=dict(S=1, TT=512))
    sample_tiles = dict(even=dict(S=16, TT=8), odd=dict(S=32, TT=8), ffn=dict(S=32, TT=8))
    outs_p = _trunk(x_prompt, (None,) * 5, 0, pe, po, pf, g_final, prompt_tiles)
    outs_s = _trunk(x_sample, (state_conv_a, state_ret, state_lru_conv, state_lru_h, state_ffn_conv),
                    past_len, pe, po, pf, g_final, sample_tiles)
    return (outs_p[0], outs_s[0]) + tuple(outs_p[1:]) + tuple(outs_s[1:])
```

```python
import functools
import math
import types

import numpy as np
import jax
import jax.numpy as jnp
from jax import lax
from jax.experimental import pallas as pl
from jax.experimental.pallas import tpu as pltpu

F32 = jnp.float32
BF16 = jnp.bfloat16

D_MODEL = 1024
CONV_A_DIM = 512
CONV_A_WIDTH = 31
CONV_A_HIST = CONV_A_WIDTH - 1
CONV_A_HIST_PAD = 32
CONV_A_ROW_BLOCK = 32
CONV_A_SEQ_BLOCK = 4
LN_EPS = 1e-5
RET_HEADS = 4
RET_HEAD_DIM = 128
RET_DIM = RET_HEADS * RET_HEAD_DIM
RET_CHUNK = 128
ROPE_BASE = 10000.0
GN_EPS = 1e-5
IN_AB_DIM = 2 * CONV_A_DIM + 4 * RET_DIM
Z_SECTION = 512
LRU_DIM = 1024
LRU_BLOCKS = 8
LRU_BLOCK = LRU_DIM // LRU_BLOCKS
LRU_CONV_WIDTH = 4
LRU_HIST = LRU_CONV_WIDTH - 1
LRU_C = 8.0
ODD_SUB_ROWS = 128
FFN_DIM = 2816
FFN_CONV_WIDTH = 3
FFN_HIST = FFN_CONV_WIDTH - 1
FFN_COL_CHUNK = 256
RMS_EPS = 1e-6
PAST_LEN = 16384
SUBLANES = 8
SAMPLE_GROUP = RET_CHUNK // SUBLANES

VMEM_LIMIT_BYTES = 56 * 1024 * 1024


def _rms(x, g):
    return x * lax.rsqrt(jnp.mean(x * x, axis=-1, keepdims=True) + RMS_EPS) * g


def _sigmoid(x):
    return 1.0 / (1.0 + jnp.exp(-x))


def _gelu(x):
    c = math.sqrt(2.0 / math.pi)
    return 0.5 * x * (1.0 + jnp.tanh(c * (x + 0.044715 * (x * x * x))))


def _mm(a, b):
    return jnp.dot(a, b, preferred_element_type=F32)


def _const_spec(shape, grid_rank=2):
    nd = len(shape)
    assert grid_rank == 2
    return pl.BlockSpec(shape, lambda b, t: (0,) * nd, pipeline_mode=pl.Buffered(1))


def _take(it, names):
    return types.SimpleNamespace(**{n: next(it) for n in names})


EVEN_CONSTS = ('cos', 'sin', 'dmask', 'qd', 'kd', 'cd', 'gmix', 'w_in', 'conv_w', 'conv_b',
               'ln_g', 'ln_b', 'gn_g', 'w_out')
EVEN_SCRATCH = ('xp_s', 'xs_s', 'ya_s')


def _even_scratch(S, TT, sample):
    nr = TT if sample else RET_CHUNK
    return [pltpu.VMEM((S, CONV_A_HIST_PAD + TT, CONV_A_DIM), F32),
            pltpu.VMEM((SUBLANES - 1, S, nr + CONV_A_HIST_PAD - SUBLANES, CONV_A_DIM), F32),
            pltpu.VMEM((S * TT, CONV_A_DIM), BF16)]


def _even_init(r, ret_live, conv_state, ret_state):
    S = r.xp_s.shape[0]
    HP, H = CONV_A_HIST_PAD, CONV_A_HIST
    r.xp_s[:, 0:HP, :] = jnp.zeros((S, HP, CONV_A_DIM), F32)
    if conv_state is not None:
        r.xp_s[:, HP - H:HP, :] = conv_state[...]
        ret_live[...] = ret_state[...]
    else:
        ret_live[...] = jnp.zeros(ret_live.shape, F32)


def _even_tiles(r, x_src, y_dst, ret_live, S, TT, sample):
    HP, H = CONV_A_HIST_PAD, CONV_A_HIST
    if sample:
        tiles = [(s0, SAMPLE_GROUP, 0, TT) for s0 in range(0, S, SAMPLE_GROUP)]
    else:
        tiles = [(0, 1, r0, RET_CHUNK) for r0 in range(0, TT, RET_CHUNK)]
    o0 = 2 * CONV_A_DIM
    scale = RET_HEAD_DIM ** -0.5

    def run(s0, ns, r0, nr):
        rows = ns * nr
        row0 = s0 * TT + r0
        x = x_src[s0:s0 + ns, r0:r0 + nr, :].reshape(rows, D_MODEL)
        h = _rms(x, r.gmix[...]).astype(BF16)
        zs = [_mm(h, r.w_in[:, c0:c0 + Z_SECTION]) for c0 in range(0, IN_AB_DIM, Z_SECTION)]

        def zcols(lo, hi):
            k = lo // Z_SECTION
            return zs[k][:, lo - k * Z_SECTION:hi - k * Z_SECTION]

        u = zcols(0, CONV_A_DIM) * _sigmoid(zcols(CONV_A_DIM, 2 * CONV_A_DIM))
        r.xp_s[s0:s0 + ns, HP + r0:HP + r0 + nr, :] = u.reshape(ns, nr, CONV_A_DIM)
        L = nr + HP - SUBLANES
        for b in range(1, SUBLANES):
            r.xs_s[b - 1, s0:s0 + ns, :, :] = r.xp_s[s0:s0 + ns, r0 + b:r0 + b + L, :]
        if sample:
            blocks = [(sb, CONV_A_SEQ_BLOCK, 0, TT) for sb in range(s0, s0 + ns, CONV_A_SEQ_BLOCK)]
        else:
            blocks = [(0, 1, rb, CONV_A_ROW_BLOCK) for rb in range(r0, r0 + nr, CONV_A_ROW_BLOCK)]
        for sb, nsb, rb, nrb in blocks:
            acc = jnp.zeros((nsb, nrb, CONV_A_DIM), F32) + r.conv_b[...]
            for j in range(CONV_A_WIDTH):
                off = HP - H + j
                b = off % SUBLANES
                if b == 0:
                    lo = off + rb
                    win = r.xp_s[sb:sb + nsb, lo:lo + nrb, :]
                else:
                    lo = off - b + rb - r0
                    win = r.xs_s[b - 1, sb:sb + nsb, lo:lo + nrb, :]
                acc = acc + r.conv_w[j:j + 1, :] * win
            cv = acc.reshape(nsb * nrb, CONV_A_DIM)
            mu = jnp.mean(cv, axis=-1, keepdims=True)
            cvc = cv - mu
            var = jnp.mean(cvc * cvc, axis=-1, keepdims=True)
            ln = cvc * lax.rsqrt(var + LN_EPS) * r.ln_g[...] + r.ln_b[...]
            rowb = sb * TT + rb
            r.ya_s[rowb:rowb + nsb * nrb, :] = (ln * _sigmoid(ln)).astype(BF16)

        cos = r.cos[row0:row0 + rows, :]
        sin = r.sin[row0:row0 + rows, :]
        gate = zcols(o0 + 3 * RET_DIM, o0 + 4 * RET_DIM)
        o_parts = []
        for hh in range(RET_HEADS):
            lo = hh * RET_HEAD_DIM
            hi = lo + RET_HEAD_DIM
            qh = zcols(o0 + lo, o0 + hi)
            kh = zcols(o0 + RET_DIM + lo, o0 + RET_DIM + hi)
            qc = qh * cos + pltpu.roll(qh, RET_HEAD_DIM // 2, axis=1) * sin
            kc = (kh * cos + pltpu.roll(kh, RET_HEAD_DIM // 2, axis=1) * sin) * scale
            vc = zcols(o0 + 2 * RET_DIM + lo, o0 + 2 * RET_DIM + hi)
            qb = qc.astype(BF16)
            vb = vc.astype(BF16)
            scores = lax.dot_general(qb, kc.astype(BF16), (((1,), (1,)), ((), ())),
                                     preferred_element_type=F32) * r.dmask[hh]
            inner = _mm(scores.astype(BF16), vb)
            if not sample:
                kdv = (kc * r.kd[hh]).astype(BF16)
                st = ret_live[0, hh]
                cross = _mm(qb, st.astype(BF16))
                upd = lax.dot_general(kdv, vb, (((0,), (0,)), ((), ())), preferred_element_type=F32)
                ret_live[0, hh] = st * r.cd[hh] + upd
            else:
                parts = []
                for sq in range(SAMPLE_GROUP):
                    sidx = s0 + sq
                    rs = slice(sq * SUBLANES, (sq + 1) * SUBLANES)
                    st = ret_live[sidx, hh]
                    parts.append(_mm(qc[rs].astype(BF16), st.astype(BF16)))
                    kdv_s = (kc[rs] * r.kd[hh, rs, :]).astype(BF16)
                    upd = lax.dot_general(kdv_s, vc[rs].astype(BF16), (((0,), (0,)), ((), ())),
                                          preferred_element_type=F32)
                    ret_live[sidx, hh] = st * r.cd[hh] + upd
                cross = jnp.concatenate(parts, axis=0)
            o = inner + cross * r.qd[hh]
            mu_o = jnp.mean(o, axis=-1, keepdims=True)
            oc = o - mu_o
            var_o = jnp.mean(oc * oc, axis=-1, keepdims=True)
            o_parts.append(oc * lax.rsqrt(var_o + GN_EPS) * r.gn_g[:, lo:hi])

        yb = jnp.concatenate(o_parts, axis=-1) * (gate * _sigmoid(gate))
        y = (_mm(r.ya_s[row0:row0 + rows, :], r.w_out[0:CONV_A_DIM, :])
             + _mm(yb.astype(BF16), r.w_out[CONV_A_DIM:CONV_A_DIM + RET_DIM, :]) + x)
        y_dst[s0:s0 + ns, r0:r0 + nr, :] = y.reshape(ns, nr, D_MODEL)

    return [functools.partial(run, *tl) for tl in tiles]


def _even_carry(r, TT):
    HP = CONV_A_HIST_PAD
    r.xp_s[:, 0:HP, :] = r.xp_s[:, TT:TT + HP, :]


def _even_conv_state(r):
    return r.xp_s[:, CONV_A_HIST_PAD - CONV_A_HIST:CONV_A_HIST_PAD, :]


def _rope_tables(pos0, T, reps):
    d = RET_HEAD_DIM
    inv_freq = ROPE_BASE ** (-np.arange(0, d, 2, dtype=np.float64) / d)
    ang = (pos0 + np.arange(T, dtype=np.float64))[:, None] * inv_freq[None, :]
    cos = np.cos(ang)
    sin = np.sin(ang)
    cos2 = np.concatenate([cos, cos], axis=-1)
    sin2 = np.concatenate([-sin, sin], axis=-1)
    return (jnp.asarray(np.tile(cos2, (reps, 1)), F32), jnp.asarray(np.tile(sin2, (reps, 1)), F32))


def _decay_tables(c, groups):
    nh = RET_HEADS
    log_gamma = np.log(1.0 - 2.0 ** (-5.0 - np.arange(nh, dtype=np.float64)))
    idx = np.arange(c, dtype=np.float64)
    rel = idx[:, None] - idx[None, :]
    dmask = np.where(rel >= 0, np.exp(np.maximum(rel, 0.0)[None] * log_gamma[:, None, None]), 0.0)
    qd = np.exp((idx + 1.0)[None, :] * log_gamma[:, None])
    kd = np.exp((c - 1.0 - idx)[None, :] * log_gamma[:, None])
    cd = np.exp(c * log_gamma)
    if groups > 1:
        eye = np.eye(groups)
        dmask = np.einsum('gk,hij->hgikj', eye, dmask).reshape(nh, groups * c, groups * c)
        qd = np.tile(qd, (1, groups))
        kd = np.tile(kd, (1, groups))
    n = groups * c
    qd = np.broadcast_to(qd[:, :, None], (nh, n, RET_HEAD_DIM))
    kd = np.broadcast_to(kd[:, :, None], (nh, n, RET_HEAD_DIM))
    cd = np.broadcast_to(cd[:, None, None], (nh, 1, RET_HEAD_DIM))
    return tuple(jnp.asarray(np.ascontiguousarray(a), F32) for a in (dmask, qd, kd, cd))


def _even_const_args(p, pos0, T, reps, sample):
    cos, sin = _rope_tables(pos0, T, reps)
    dmask, qd, kd, cd = _decay_tables(T, SAMPLE_GROUP) if sample else _decay_tables(RET_CHUNK, 1)
    return [cos, sin, dmask, qd, kd, cd, p['gmix'], p['w_in'], p['conv_w'], p['conv_b'],
            p['ln_g'], p['ln_b'], p['gn_g'], p['w_out']]


def _even_kernel(*refs, S, TT, nt, sample):
    it = iter(refs)
    x_ref = next(it)
    r = _take(it, EVEN_CONSTS)
    convst_ref = next(it) if sample else None
    retst_ref = next(it) if sample else None
    y_ref = next(it); convout_ref = next(it); retout_ref = next(it)
    r.__dict__.update(_take(it, EVEN_SCRATCH).__dict__)
    t = pl.program_id(1)

    @pl.when(t == 0)
    def _():
        _even_init(r, retout_ref, convst_ref, retst_ref)

    for tile in _even_tiles(r, x_ref, y_ref, retout_ref, S, TT, sample):
        tile()
    _even_carry(r, TT)

    @pl.when(t == nt - 1)
    def _():
        convout_ref[...] = _even_conv_state(r)


def _even_mixer(x, conv_state, ret_state, p, *, S, TT):
    B, T, D = x.shape
    sample = conv_state is not None
    nb, nt = B // S, T // TT
    R = S * TT
    if sample:
        assert TT == T == SUBLANES and S % SAMPLE_GROUP == 0
        consts = _even_const_args(p, PAST_LEN, T, S, True)
    else:
        assert S == 1 and TT % RET_CHUNK == 0
        consts = _even_const_args(p, 0, T, 1, False)
    in_specs = [pl.BlockSpec((S, TT, D), lambda b, t: (b, t, 0)),
                pl.BlockSpec((R, RET_HEAD_DIM), lambda b, t: (t, 0)),
                pl.BlockSpec((R, RET_HEAD_DIM), lambda b, t: (t, 0))]
    in_specs += [_const_spec(c.shape, 2) for c in consts[2:]]
    args = [x] + consts
    if sample:
        in_specs += [pl.BlockSpec((None, S, CONV_A_HIST, CONV_A_DIM), lambda b, t: (0, b, 0, 0)),
                     pl.BlockSpec((None, S, RET_HEADS, RET_HEAD_DIM, RET_HEAD_DIM),
                                  lambda b, t: (0, b, 0, 0, 0))]
        args += [conv_state, ret_state]
    out_shape = (jax.ShapeDtypeStruct((B, T, D), F32),
                 jax.ShapeDtypeStruct((B, CONV_A_HIST, CONV_A_DIM), F32),
                 jax.ShapeDtypeStruct((B, RET_HEADS, RET_HEAD_DIM, RET_HEAD_DIM), F32))
    out_specs = (pl.BlockSpec((S, TT, D), lambda b, t: (b, t, 0)),
                 pl.BlockSpec((S, CONV_A_HIST, CONV_A_DIM), lambda b, t: (b, 0, 0)),
                 pl.BlockSpec((S, RET_HEADS, RET_HEAD_DIM, RET_HEAD_DIM), lambda b, t: (b, 0, 0, 0)))
    return pl.pallas_call(
        functools.partial(_even_kernel, S=S, TT=TT, nt=nt, sample=sample),
        grid=(nb, nt), in_specs=in_specs, out_specs=out_specs, out_shape=out_shape,
        scratch_shapes=_even_scratch(S, TT, sample),
        compiler_params=pltpu.CompilerParams(dimension_semantics=("arbitrary", "arbitrary"),
                                             vmem_limit_bytes=VMEM_LIMIT_BYTES),
        name="even_mixer_sample" if sample else "even_mixer_prompt",
    )(*args)


ODD_CONSTS = ('gmix', 'w_in', 'conv_w', 'conv_b', 'w_ax', 'b_a', 'b_x', 'lam', 'w_out')
ODD_SCRATCH = ('xp_s', 'gate_s', 'xc_s', 'r_s', 'i_s', 'act_s')


def _odd_scratch(S, TT):
    R = S * TT
    return ([pltpu.VMEM((S, SUBLANES + TT, LRU_DIM), F32)]
            + [pltpu.VMEM((R, LRU_DIM), F32) for _ in range(4)]
            + [pltpu.VMEM((R, LRU_DIM), BF16)])


def _odd_init(r, h_live, conv_state, h_state):
    S = r.xp_s.shape[0]
    HP, H = SUBLANES, LRU_HIST
    r.xp_s[:, 0:HP, :] = jnp.zeros((S, HP, LRU_DIM), F32)
    if conv_state is not None:
        r.xp_s[:, HP - H:HP, :] = conv_state[...]
        h_live[...] = h_state[...]
    else:
        h_live[...] = jnp.zeros(h_live.shape, F32)


def _odd_tiles(r, x_src, y_dst, h_live, S, TT, sample):
    HP, H = SUBLANES, LRU_HIST
    if sample:
        tiles = [(0, S, 0, TT)]
    else:
        tiles = [(0, 1, r0, min(TT, ODD_SUB_ROWS)) for r0 in range(0, TT, ODD_SUB_ROWS)]
    pair = 2 * LRU_BLOCK

    def scan_group(rows, h_prev, decay, sub):
        rg = _sigmoid(r.r_s[rows, :])
        ig = _sigmoid(r.i_s[rows, :])
        a = jnp.exp(decay * rg)
        b = jnp.sqrt(jnp.maximum(1.0 - a * a, 0.0)) * (ig * r.xc_s[rows, :])
        for sh in (1, 2, 4):
            keep = sub >= sh
            a_sh = jnp.where(keep, pltpu.roll(a, sh, axis=0), 1.0)
            b_sh = jnp.where(keep, pltpu.roll(b, sh, axis=0), 0.0)
            b = a * b_sh + b
            a = a * a_sh
        return a * h_prev + b

    def run(s0, ns, r0, nr):
        rows = ns * nr
        row0 = s0 * TT + r0
        nlam = -r.lam[...]
        softplus = jnp.maximum(nlam, 0.0) + jnp.log(1.0 + jnp.exp(-jnp.abs(nlam)))
        decay = jnp.broadcast_to((-LRU_C) * softplus, (SUBLANES, LRU_DIM))
        sub = lax.broadcasted_iota(jnp.int32, (SUBLANES, LRU_DIM), 0)
        x = x_src[s0:s0 + ns, r0:r0 + nr, :].reshape(rows, D_MODEL)
        h = _rms(x, r.gmix[...]).astype(BF16)
        r.gate_s[row0:row0 + rows, :] = _mm(h, r.w_in[:, 0:LRU_DIM])
        rec3 = _mm(h, r.w_in[:, LRU_DIM:2 * LRU_DIM]).reshape(ns, nr, LRU_DIM)
        r.xp_s[s0:s0 + ns, HP + r0:HP + r0 + nr, :] = rec3
        acc = r.conv_w[H:H + 1, :] * rec3 + r.conv_b[...]
        for j in range(H):
            lo = HP - H + j + r0
            acc = acc + r.conv_w[j:j + 1, :] * r.xp_s[s0:s0 + ns, lo:lo + nr, :]
        xc = acc.reshape(rows, LRU_DIM)
        r.xc_s[row0:row0 + rows, :] = xc
        xcb = xc.astype(BF16)
        for p in range(LRU_BLOCKS // 2):
            cs = slice(pair * p, pair * (p + 1))
            ri = _mm(xcb[:, cs], r.w_ax[p])
            r.r_s[row0:row0 + rows, cs] = ri[:, 0:pair] + r.b_a[:, cs]
            r.i_s[row0:row0 + rows, cs] = ri[:, pair:2 * pair] + r.b_x[:, cs]
        h_prev = None if sample else jnp.broadcast_to(h_live[0], (SUBLANES, LRU_DIM))
        for g0 in range(row0, row0 + rows, 2 * SUBLANES):
            parts = []
            for k in range(2):
                ga = g0 + k * SUBLANES
                if sample:
                    h_prev = jnp.broadcast_to(h_live[ga // TT], (SUBLANES, LRU_DIM))
                hs = scan_group(slice(ga, ga + SUBLANES), h_prev, decay, sub)
                h_last = hs[SUBLANES - 1:SUBLANES, :]
                if sample:
                    h_live[ga // TT] = h_last
                h_prev = jnp.broadcast_to(h_last, (SUBLANES, LRU_DIM))
                parts.append(hs)
            rows2 = slice(g0, g0 + 2 * SUBLANES)
            r.act_s[rows2, :] = (jnp.concatenate(parts, axis=0) * _gelu(r.gate_s[rows2, :])).astype(BF16)
        if not sample:
            h_live[0] = h_prev[0:1, :]
        y = _mm(r.act_s[row0:row0 + rows, :], r.w_out[...]) + x
        y_dst[s0:s0 + ns, r0:r0 + nr, :] = y.reshape(ns, nr, D_MODEL)

    return [functools.partial(run, *tl) for tl in tiles]


def _odd_carry(r, TT):
    HP = SUBLANES
    r.xp_s[:, 0:HP, :] = r.xp_s[:, TT:TT + HP, :]


def _odd_conv_state(r):
    return r.xp_s[:, SUBLANES - LRU_HIST:SUBLANES, :]


def _odd_const_args(p):
    return [p[n] for n in ODD_CONSTS]


def _odd_kernel(*refs, S, TT, nt, sample):
    it = iter(refs)
    x_ref = next(it)
    r = _take(it, ODD_CONSTS)
    convst_ref = next(it) if sample else None
    hst_ref = next(it) if sample else None
    y_ref = next(it); convout_ref = next(it); hout_ref = next(it)
    r.__dict__.update(_take(it, ODD_SCRATCH).__dict__)
    t = pl.program_id(1)

    @pl.when(t == 0)
    def _():
        _odd_init(r, hout_ref, convst_ref, hst_ref)

    for tile in _odd_tiles(r, x_ref, y_ref, hout_ref, S, TT, sample):
        tile()
    _odd_carry(r, TT)

    @pl.when(t == nt - 1)
    def _():
        convout_ref[...] = _odd_conv_state(r)


def _odd_mixer(x, conv_state, h_state, p, *, S, TT):
    B, T, D = x.shape
    sample = conv_state is not None
    nb, nt = B // S, T // TT
    consts = _odd_const_args(p)
    in_specs = [pl.BlockSpec((S, TT, D), lambda b, t: (b, t, 0))]
    in_specs += [_const_spec(c.shape, 2) for c in consts]
    args = [x] + consts
    if sample:
        in_specs += [pl.BlockSpec((None, S, LRU_HIST, LRU_DIM), lambda b, t: (0, b, 0, 0)),
                     pl.BlockSpec((S, 1, LRU_DIM), lambda b, t: (b, 0, 0))]
        args += [conv_state, h_state]
    out_shape = (jax.ShapeDtypeStruct((B, T, D), F32),
                 jax.ShapeDtypeStruct((B, LRU_HIST, LRU_DIM), F32),
                 jax.ShapeDtypeStruct((B, 1, LRU_DIM), F32))
    out_specs = (pl.BlockSpec((S, TT, D), lambda b, t: (b, t, 0)),
                 pl.BlockSpec((S, LRU_HIST, LRU_DIM), lambda b, t: (b, 0, 0)),
                 pl.BlockSpec((S, 1, LRU_DIM), lambda b, t: (b, 0, 0)))
    return pl.pallas_call(
        functools.partial(_odd_kernel, S=S, TT=TT, nt=nt, sample=sample),
        grid=(nb, nt), in_specs=in_specs, out_specs=out_specs, out_shape=out_shape,
        scratch_shapes=_odd_scratch(S, TT),
        compiler_params=pltpu.CompilerParams(dimension_semantics=("arbitrary", "arbitrary"),
                                             vmem_limit_bytes=VMEM_LIMIT_BYTES),
        name="odd_mixer_sample" if sample else "odd_mixer_prompt",
    )(*args)


FFN_CONSTS = ('g', 'w_up', 'conv_w', 'conv_b', 'w_down')
FFN_SCRATCH = ('h_s', 'act_s', 'hist_s', 'work_s')


def _ffn_scratch(S, TT):
    R = S * TT
    return [pltpu.VMEM((R, D_MODEL), BF16),
            pltpu.VMEM((R, FFN_DIM), BF16),
            pltpu.VMEM((S, SUBLANES, 2 * FFN_DIM), F32),
            pltpu.VMEM((S, SUBLANES + TT, FFN_COL_CHUNK), F32)]


def _ffn_init(r, state):
    r.hist_s[...] = jnp.zeros(r.hist_s.shape, F32)
    if state is not None:
        r.hist_s[:, SUBLANES - FFN_HIST:SUBLANES, :] = state[...]


def _ffn_stages(r, x_src, y_dst, g_final, S, TT):
    R = S * TT
    HP, H, CK = SUBLANES, FFN_HIST, FFN_COL_CHUNK

    def prologue():
        x = x_src[...].reshape(R, D_MODEL)
        r.h_s[...] = _rms(x, r.g[...]).astype(BF16)

    def conv_cols(col):
        z3 = _mm(r.h_s[...], r.w_up[:, col:col + CK]).reshape(S, TT, CK)
        r.work_s[:, 0:HP, :] = r.hist_s[:, :, col:col + CK]
        r.work_s[:, HP:HP + TT, :] = z3
        zc = r.conv_w[H:H + 1, col:col + CK] * z3 + r.conv_b[:, col:col + CK]
        for j in range(H):
            zc = zc + r.conv_w[j:j + 1, col:col + CK] * r.work_s[:, HP - H + j:HP - H + j + TT, :]
        r.hist_s[:, :, col:col + CK] = r.work_s[:, TT:TT + HP, :]
        return zc.reshape(R, CK)

    def chunk(c):
        gz = conv_cols(c * CK)
        uz = conv_cols(FFN_DIM + c * CK)
        r.act_s[:, c * CK:(c + 1) * CK] = (_gelu(gz) * uz).astype(BF16)

    def epilogue():
        y = _mm(r.act_s[...], r.w_down[...]) + x_src[...].reshape(R, D_MODEL)
        if g_final is not None:
            y = _rms(y, g_final[...])
        y_dst[...] = y.reshape(S, TT, D_MODEL)

    return prologue, [functools.partial(chunk, c) for c in range(FFN_DIM // CK)], epilogue


def _ffn_state(r):
    return r.hist_s[:, SUBLANES - FFN_HIST:SUBLANES, :]


def _ffn_const_args(p, g_final):
    return [p[n] for n in FFN_CONSTS] + ([g_final] if g_final is not None else [])


def _ffn_kernel(*refs, S, TT, nt, sample, final):
    it = iter(refs)
    x_ref = next(it)
    r = _take(it, FFN_CONSTS)
    gfin_ref = next(it) if final else None
    st_ref = next(it) if sample else None
    y_ref = next(it); stout_ref = next(it)
    r.__dict__.update(_take(it, FFN_SCRATCH).__dict__)
    t = pl.program_id(1)

    @pl.when(t == 0)
    def _():
        _ffn_init(r, st_ref)

    prologue, chunks, epilogue = _ffn_stages(r, x_ref, y_ref, gfin_ref, S, TT)
    prologue()
    for ch in chunks:
        ch()
    epilogue()

    @pl.when(t == nt - 1)
    def _():
        stout_ref[...] = _ffn_state(r)


def _conv_ffn(x, state, layer, p, g_final, *, S, TT):
    B, T, D = x.shape
    sample = state is not None
    final = g_final is not None
    nb, nt = B // S, T // TT
    consts = _ffn_const_args(p, g_final)
    in_specs = [pl.BlockSpec((S, TT, D), lambda b, t: (b, t, 0))]
    in_specs += [_const_spec(c.shape, 2) for c in consts]
    args = [x] + consts
    if sample:
        in_specs.append(pl.BlockSpec((None, S, FFN_HIST, 2 * FFN_DIM), lambda b, t: (layer, b, 0, 0)))
        args.append(state)
    out_shape = (jax.ShapeDtypeStruct((B, T, D), F32),
                 jax.ShapeDtypeStruct((B, FFN_HIST, 2 * FFN_DIM), F32))
    out_specs = (pl.BlockSpec((S, TT, D), lambda b, t: (b, t, 0)),
                 pl.BlockSpec((S, FFN_HIST, 2 * FFN_DIM), lambda b, t: (b, 0, 0)))
    return pl.pallas_call(
        functools.partial(_ffn_kernel, S=S, TT=TT, nt=nt, sample=sample, final=final),
        grid=(nb, nt), in_specs=in_specs, out_specs=out_specs, out_shape=out_shape,
        scratch_shapes=_ffn_scratch(S, TT),
        compiler_params=pltpu.CompilerParams(dimension_semantics=("arbitrary", "arbitrary"),
                                             vmem_limit_bytes=VMEM_LIMIT_BYTES),
        name=("ffn_sample" if sample else "ffn_prompt") + ("_final" if final else ""),
    )(*args)


PROMPT_TILES = dict(even=dict(S=1, TT=512), odd=dict(S=1, TT=512), ffn=dict(S=1, TT=1024))
SAMPLE_TILES = dict(even=dict(S=16, TT=8), odd=dict(S=32, TT=8), ffn=dict(S=32, TT=8))


def _row(v):
    return v.reshape(1, -1)


def _pair_block_diag(w_a, w_x):
    def pairs(w):
        w = w.reshape(LRU_BLOCKS // 2, 2, LRU_BLOCK, LRU_BLOCK)
        z = jnp.zeros_like(w[:, 0])
        top = jnp.concatenate([w[:, 0], z], axis=-1)
        bot = jnp.concatenate([z, w[:, 1]], axis=-1)
        return jnp.concatenate([top, bot], axis=-2)
    return jnp.concatenate([pairs(w_a), pairs(w_x)], axis=-1).astype(BF16)


def kernel(x_prompt, x_sample, state_conv_a, state_ret, state_lru_conv, state_lru_h, state_ffn_conv, norm_mix, norm_ffn, norm_final, w_in_ab, conv_a_w, conv_a_b, ln_a_g, ln_a_b, gn_ret_g, w_out_ab, w_in_c, conv_c_w, conv_c_b, w_lru_a, b_lru_a, w_lru_x, b_lru_x, lru_lambda, w_out_c, w_ffn_up, ffn_conv_w, ffn_conv_b, w_ffn_down):
    pe = dict(gmix=_row(norm_mix[0]), w_in=w_in_ab[0].astype(BF16), conv_w=conv_a_w[0],
              conv_b=_row(conv_a_b[0]), ln_g=_row(ln_a_g[0]), ln_b=_row(ln_a_b[0]),
              gn_g=_row(gn_ret_g[0]), w_out=w_out_ab[0].astype(BF16))
    po = dict(gmix=_row(norm_mix[1]), w_in=w_in_c[0].astype(BF16), conv_w=conv_c_w[0],
              conv_b=_row(conv_c_b[0]), w_ax=_pair_block_diag(w_lru_a[0], w_lru_x[0]),
              b_a=_row(b_lru_a[0]), b_x=_row(b_lru_x[0]), lam=_row(lru_lambda[0]),
              w_out=w_out_c[0].astype(BF16))
    pf = [dict(g=_row(norm_ffn[l]), w_up=w_ffn_up[l].astype(BF16), conv_w=ffn_conv_w[l],
               conv_b=_row(ffn_conv_b[l]), w_down=w_ffn_down[l].astype(BF16)) for l in range(2)]
    g_final = _row(norm_final)

    xp, p_conv_a, p_ret = _even_mixer(x_prompt, None, None, pe, **PROMPT_TILES['even'])
    xp, p_ffn0 = _conv_ffn(xp, None, 0, pf[0], None, **PROMPT_TILES['ffn'])
    xp, p_lru_conv, p_lru_h = _odd_mixer(xp, None, None, po, **PROMPT_TILES['odd'])
    y_prompt, p_ffn1 = _conv_ffn(xp, None, 1, pf[1], g_final, **PROMPT_TILES['ffn'])

    xs, s_conv_a, s_ret = _even_mixer(x_sample, state_conv_a, state_ret, pe, **SAMPLE_TILES['even'])
    xs, s_ffn0 = _conv_ffn(xs, state_ffn_conv, 0, pf[0], None, **SAMPLE_TILES['ffn'])
    xs, s_lru_conv, s_lru_h = _odd_mixer(xs, state_lru_conv, state_lru_h[0][:, None, :], po,
                                         **SAMPLE_TILES['odd'])
    y_sample, s_ffn1 = _conv_ffn(xs, state_ffn_conv, 1, pf[1], g_final, **SAMPLE_TILES['ffn'])

    return (y_prompt, y_sample,
            p_conv_a[None], p_ret[None], p_lru_conv[None], p_lru_h[:, 0, :][None],
            jnp.stack([p_ffn0, p_ffn1]),
            s_conv_a[None], s_ret[None], s_lru_conv[None], s_lru_h[:, 0, :][None],
            jnp.stack([s_ffn0, s_ffn1]))
```

```python
import functools
import math
import types

import numpy as np
import jax
import jax.numpy as jnp
from jax import lax
from jax.experimental import pallas as pl
from jax.experimental.pallas import tpu as pltpu

F32 = jnp.float32
BF16 = jnp.bfloat16

D_MODEL = 1024
CONV_A_DIM = 512
CONV_A_WIDTH = 31
CONV_A_HIST = CONV_A_WIDTH - 1
CONV_A_HIST_PAD = 32
CONV_A_ROW_BLOCK = 32
CONV_A_SEQ_BLOCK = 4
LN_EPS = 1e-5
RET_HEADS = 4
RET_HEAD_DIM = 128
RET_DIM = RET_HEADS * RET_HEAD_DIM
RET_CHUNK = 128
ROPE_BASE = 10000.0
GN_EPS = 1e-5
IN_AB_DIM = 2 * CONV_A_DIM + 4 * RET_DIM
Z_SECTION = 512
LRU_DIM = 1024
LRU_BLOCKS = 8
LRU_BLOCK = LRU_DIM // LRU_BLOCKS
LRU_CONV_WIDTH = 4
LRU_HIST = LRU_CONV_WIDTH - 1
LRU_C = 8.0
ODD_SUB_ROWS = 128
FFN_DIM = 2816
FFN_CONV_WIDTH = 3
FFN_HIST = FFN_CONV_WIDTH - 1
FFN_COL_CHUNK = 256
RMS_EPS = 1e-6
PAST_LEN = 16384
SUBLANES = 8
SAMPLE_GROUP = RET_CHUNK // SUBLANES

VMEM_LIMIT_BYTES = 56 * 1024 * 1024


def _rms(x, g):
    return x * lax.rsqrt(jnp.mean(x * x, axis=-1, keepdims=True) + RMS_EPS) * g


def _sigmoid(x):
    return 1.0 / (1.0 + jnp.exp(-x))


def _gelu(x):
    c = math.sqrt(2.0 / math.pi)
    return 0.5 * x * (1.0 + jnp.tanh(c * (x + 0.044715 * (x * x * x))))


def _mm(a, b):
    return jnp.dot(a, b, preferred_element_type=F32)


def _const_spec(shape, grid_rank=2):
    nd = len(shape)
    assert grid_rank == 2
    return pl.BlockSpec(shape, lambda b, t: (0,) * nd, pipeline_mode=pl.Buffered(1))


def _take(it, names):
    return types.SimpleNamespace(**{n: next(it) for n in names})


EVEN_CONSTS = ('cos', 'sin', 'dmask', 'qd', 'kd', 'cd', 'gmix', 'w_in', 'conv_w', 'conv_b',
               'ln_g', 'ln_b', 'gn_g', 'w_out')
EVEN_SCRATCH = ('xp_s', 'xs_s', 'ya_s')


def _even_scratch(S, TT, sample):
    nr = TT if sample else RET_CHUNK
    return [pltpu.VMEM((S, CONV_A_HIST_PAD + TT, CONV_A_DIM), F32),
            pltpu.VMEM((SUBLANES - 1, S, nr + CONV_A_HIST_PAD - SUBLANES, CONV_A_DIM), F32),
            pltpu.VMEM((S * TT, CONV_A_DIM), BF16)]


def _even_init(r, ret_live, conv_state, ret_state):
    S = r.xp_s.shape[0]
    HP, H = CONV_A_HIST_PAD, CONV_A_HIST
    r.xp_s[:, 0:HP, :] = jnp.zeros((S, HP, CONV_A_DIM), F32)
    if conv_state is not None:
        r.xp_s[:, HP - H:HP, :] = conv_state[...]
        ret_live[...] = ret_state[...]
    else:
        ret_live[...] = jnp.zeros(ret_live.shape, F32)


def _even_tiles(r, x_src, y_dst, ret_live, S, TT, sample):
    HP, H = CONV_A_HIST_PAD, CONV_A_HIST
    if sample:
        tiles = [(s0, SAMPLE_GROUP, 0, TT) for s0 in range(0, S, SAMPLE_GROUP)]
    else:
        tiles = [(0, 1, r0, RET_CHUNK) for r0 in range(0, TT, RET_CHUNK)]
    o0 = 2 * CONV_A_DIM
    scale = RET_HEAD_DIM ** -0.5

    def run(s0, ns, r0, nr):
        rows = ns * nr
        row0 = s0 * TT + r0
        x = x_src[s0:s0 + ns, r0:r0 + nr, :].reshape(rows, D_MODEL)
        h = _rms(x, r.gmix[...]).astype(BF16)
        zs = [_mm(h, r.w_in[:, c0:c0 + Z_SECTION]) for c0 in range(0, IN_AB_DIM, Z_SECTION)]

        def zcols(lo, hi):
            k = lo // Z_SECTION
            return zs[k][:, lo - k * Z_SECTION:hi - k * Z_SECTION]

        u = zcols(0, CONV_A_DIM) * _sigmoid(zcols(CONV_A_DIM, 2 * CONV_A_DIM))
        r.xp_s[s0:s0 + ns, HP + r0:HP + r0 + nr, :] = u.reshape(ns, nr, CONV_A_DIM)
        L = nr + HP - SUBLANES
        for b in range(1, SUBLANES):
            r.xs_s[b - 1, s0:s0 + ns, :, :] = r.xp_s[s0:s0 + ns, r0 + b:r0 + b + L, :]
        if sample:
            blocks = [(sb, CONV_A_SEQ_BLOCK, 0, TT) for sb in range(s0, s0 + ns, CONV_A_SEQ_BLOCK)]
        else:
            blocks = [(0, 1, rb, CONV_A_ROW_BLOCK) for rb in range(r0, r0 + nr, CONV_A_ROW_BLOCK)]
        for sb, nsb, rb, nrb in blocks:
            acc = jnp.zeros((nsb, nrb, CONV_A_DIM), F32) + r.conv_b[...]
            for j in range(CONV_A_WIDTH):
                off = HP - H + j
                b = off % SUBLANES
                if b == 0:
                    lo = off + rb
                    win = r.xp_s[sb:sb + nsb, lo:lo + nrb, :]
                else:
                    lo = off - b + rb - r0
                    win = r.xs_s[b - 1, sb:sb + nsb, lo:lo + nrb, :]
                acc = acc + r.conv_w[j:j + 1, :] * win
            cv = acc.reshape(nsb * nrb, CONV_A_DIM)
            mu = jnp.mean(cv, axis=-1, keepdims=True)
            cvc = cv - mu
            var = jnp.mean(cvc * cvc, axis=-1, keepdims=True)
            ln = cvc * lax.rsqrt(var + LN_EPS) * r.ln_g[...] + r.ln_b[...]
            rowb = sb * TT + rb
            r.ya_s[rowb:rowb + nsb * nrb, :] = (ln * _sigmoid(ln)).astype(BF16)

        cos = r.cos[row0:row0 + rows, :]
        sin = r.sin[row0:row0 + rows, :]
        gate = zcols(o0 + 3 * RET_DIM, o0 + 4 * RET_DIM)
        o_parts = []
        for hh in range(RET_HEADS):
            lo = hh * RET_HEAD_DIM
            hi = lo + RET_HEAD_DIM
            qh = zcols(o0 + lo, o0 + hi)
            kh = zcols(o0 + RET_DIM + lo, o0 + RET_DIM + hi)
            qc = qh * cos + pltpu.roll(qh, RET_HEAD_DIM // 2, axis=1) * sin
            kc = (kh * cos + pltpu.roll(kh, RET_HEAD_DIM // 2, axis=1) * sin) * scale
            vc = zcols(o0 + 2 * RET_DIM + lo, o0 + 2 * RET_DIM + hi)
            qb = qc.astype(BF16)
            vb = vc.astype(BF16)
            scores = lax.dot_general(qb, kc.astype(BF16), (((1,), (1,)), ((), ())),
                                     preferred_element_type=F32) * r.dmask[hh]
            inner = _mm(scores.astype(BF16), vb)
            if not sample:
                kdv = (kc * r.kd[hh]).astype(BF16)
                st = ret_live[0, hh]
                cross = _mm(qb, st.astype(BF16))
                upd = lax.dot_general(kdv, vb, (((0,), (0,)), ((), ())), preferred_element_type=F32)
                ret_live[0, hh] = st * r.cd[hh] + upd
            else:
                parts = []
                for sq in range(SAMPLE_GROUP):
                    sidx = s0 + sq
                    rs = slice(sq * SUBLANES, (sq + 1) * SUBLANES)
                    st = ret_live[sidx, hh]
                    parts.append(_mm(qc[rs].astype(BF16), st.astype(BF16)))
                    kdv_s = (kc[rs] * r.kd[hh, rs, :]).astype(BF16)
                    upd = lax.dot_general(kdv_s, vc[rs].astype(BF16), (((0,), (0,)), ((), ())),
                                          preferred_element_type=F32)
                    ret_live[sidx, hh] = st * r.cd[hh] + upd
                cross = jnp.concatenate(parts, axis=0)
            o = inner + cross * r.qd[hh]
            mu_o = jnp.mean(o, axis=-1, keepdims=True)
            oc = o - mu_o
            var_o = jnp.mean(oc * oc, axis=-1, keepdims=True)
            o_parts.append(oc * lax.rsqrt(var_o + GN_EPS) * r.gn_g[:, lo:hi])

        yb = jnp.concatenate(o_parts, axis=-1) * (gate * _sigmoid(gate))
        y = (_mm(r.ya_s[row0:row0 + rows, :], r.w_out[0:CONV_A_DIM, :])
             + _mm(yb.astype(BF16), r.w_out[CONV_A_DIM:CONV_A_DIM + RET_DIM, :]) + x)
        y_dst[s0:s0 + ns, r0:r0 + nr, :] = y.reshape(ns, nr, D_MODEL)

    return [functools.partial(run, *tl) for tl in tiles]


def _even_carry(r, TT):
    HP = CONV_A_HIST_PAD
    r.xp_s[:, 0:HP, :] = r.xp_s[:, TT:TT + HP, :]


def _even_conv_state(r):
    return r.xp_s[:, CONV_A_HIST_PAD - CONV_A_HIST:CONV_A_HIST_PAD, :]


def _rope_tables(pos0, T, reps):
    d = RET_HEAD_DIM
    inv_freq = ROPE_BASE ** (-np.arange(0, d, 2, dtype=np.float64) / d)
    ang = (pos0 + np.arange(T, dtype=np.float64))[:, None] * inv_freq[None, :]
    cos = np.cos(ang)
    sin = np.sin(ang)
    cos2 = np.concatenate([cos, cos], axis=-1)
    sin2 = np.concatenate([-sin, sin], axis=-1)
    return (jnp.asarray(np.tile(cos2, (reps, 1)), F32), jnp.asarray(np.tile(sin2, (reps, 1)), F32))


def _decay_tables(c, groups):
    nh = RET_HEADS
    log_gamma = np.log(1.0 - 2.0 ** (-5.0 - np.arange(nh, dtype=np.float64)))
    idx = np.arange(c, dtype=np.float64)
    rel = idx[:, None] - idx[None, :]
    dmask = np.where(rel >= 0, np.exp(np.maximum(rel, 0.0)[None] * log_gamma[:, None, None]), 0.0)
    qd = np.exp((idx + 1.0)[None, :] * log_gamma[:, None])
    kd = np.exp((c - 1.0 - idx)[None, :] * log_gamma[:, None])
    cd = np.exp(c * log_gamma)
    if groups > 1:
        eye = np.eye(groups)
        dmask = np.einsum('gk,hij->hgikj', eye, dmask).reshape(nh, groups * c, groups * c)
        qd = np.tile(qd, (1, groups))
        kd = np.tile(kd, (1, groups))
    n = groups * c
    qd = np.broadcast_to(qd[:, :, None], (nh, n, RET_HEAD_DIM))
    kd = np.broadcast_to(kd[:, :, None], (nh, n, RET_HEAD_DIM))
    cd = np.broadcast_to(cd[:, None, None], (nh, 1, RET_HEAD_DIM))
    return tuple(jnp.asarray(np.ascontiguousarray(a), F32) for a in (dmask, qd, kd, cd))


def _even_const_args(p, pos0, T, reps, sample):
    cos, sin = _rope_tables(pos0, T, reps)
    dmask, qd, kd, cd = _decay_tables(T, SAMPLE_GROUP) if sample else _decay_tables(RET_CHUNK, 1)
    return [cos, sin, dmask, qd, kd, cd, p['gmix'], p['w_in'], p['conv_w'], p['conv_b'],
            p['ln_g'], p['ln_b'], p['gn_g'], p['w_out']]


def _even_kernel(*refs, S, TT, nt, sample):
    it = iter(refs)
    x_ref = next(it)
    r = _take(it, EVEN_CONSTS)
    convst_ref = next(it) if sample else None
    retst_ref = next(it) if sample else None
    y_ref = next(it); convout_ref = next(it); retout_ref = next(it)
    r.__dict__.update(_take(it, EVEN_SCRATCH).__dict__)
    t = pl.program_id(1)

    @pl.when(t == 0)
    def _():
        _even_init(r, retout_ref, convst_ref, retst_ref)

    for tile in _even_tiles(r, x_ref, y_ref, retout_ref, S, TT, sample):
        tile()
    _even_carry(r, TT)

    @pl.when(t == nt - 1)
    def _():
        convout_ref[...] = _even_conv_state(r)


def _even_mixer(x, conv_state, ret_state, p, *, S, TT):
    B, T, D = x.shape
    sample = conv_state is not None
    nb, nt = B // S, T // TT
    R = S * TT
    if sample:
        assert TT == T == SUBLANES and S % SAMPLE_GROUP == 0
        consts = _even_const_args(p, PAST_LEN, T, S, True)
    else:
        assert S == 1 and TT % RET_CHUNK == 0
        consts = _even_const_args(p, 0, T, 1, False)
    in_specs = [pl.BlockSpec((S, TT, D), lambda b, t: (b, t, 0)),
                pl.BlockSpec((R, RET_HEAD_DIM), lambda b, t: (t, 0)),
                pl.BlockSpec((R, RET_HEAD_DIM), lambda b, t: (t, 0))]
    in_specs += [_const_spec(c.shape, 2) for c in consts[2:]]
    args = [x] + consts
    if sample:
        in_specs += [pl.BlockSpec((None, S, CONV_A_HIST, CONV_A_DIM), lambda b, t: (0, b, 0, 0)),
                     pl.BlockSpec((None, S, RET_HEADS, RET_HEAD_DIM, RET_HEAD_DIM),
                                  lambda b, t: (0, b, 0, 0, 0))]
        args += [conv_state, ret_state]
    out_shape = (jax.ShapeDtypeStruct((B, T, D), F32),
                 jax.ShapeDtypeStruct((B, CONV_A_HIST, CONV_A_DIM), F32),
                 jax.ShapeDtypeStruct((B, RET_HEADS, RET_HEAD_DIM, RET_HEAD_DIM), F32))
    out_specs = (pl.BlockSpec((S, TT, D), lambda b, t: (b, t, 0)),
                 pl.BlockSpec((S, CONV_A_HIST, CONV_A_DIM), lambda b, t: (b, 0, 0)),
                 pl.BlockSpec((S, RET_HEADS, RET_HEAD_DIM, RET_HEAD_DIM), lambda b, t: (b, 0, 0, 0)))
    return pl.pallas_call(
        functools.partial(_even_kernel, S=S, TT=TT, nt=nt, sample=sample),
        grid=(nb, nt), in_specs=in_specs, out_specs=out_specs, out_shape=out_shape,
        scratch_shapes=_even_scratch(S, TT, sample),
        compiler_params=pltpu.CompilerParams(dimension_semantics=("arbitrary", "arbitrary"),
                                             vmem_limit_bytes=VMEM_LIMIT_BYTES),
        name="even_mixer_sample" if sample else "even_mixer_prompt",
    )(*args)


ODD_CONSTS = ('gmix', 'w_in', 'conv_w', 'conv_b', 'w_ax', 'b_a', 'b_x', 'lam', 'w_out')
ODD_SCRATCH = ('xp_s', 'gate_s', 'xc_s', 'r_s', 'i_s', 'act_s')


def _odd_scratch(S, TT):
    R = S * TT
    return ([pltpu.VMEM((S, SUBLANES + TT, LRU_DIM), F32)]
            + [pltpu.VMEM((R, LRU_DIM), F32) for _ in range(4)]
            + [pltpu.VMEM((R, LRU_DIM), BF16)])


def _odd_init(r, h_live, conv_state, h_state):
    S = r.xp_s.shape[0]
    HP, H = SUBLANES, LRU_HIST
    r.xp_s[:, 0:HP, :] = jnp.zeros((S, HP, LRU_DIM), F32)
    if conv_state is not None:
        r.xp_s[:, HP - H:HP, :] = conv_state[...]
        h_live[...] = h_state[...]
    else:
        h_live[...] = jnp.zeros(h_live.shape, F32)


def _odd_tiles(r, x_src, y_dst, h_live, S, TT, sample):
    HP, H = SUBLANES, LRU_HIST
    if sample:
        tiles = [(0, S, 0, TT)]
    else:
        tiles = [(0, 1, r0, min(TT, ODD_SUB_ROWS)) for r0 in range(0, TT, ODD_SUB_ROWS)]
    pair = 2 * LRU_BLOCK

    def scan_group(rows, h_prev, decay, sub):
        rg = _sigmoid(r.r_s[rows, :])
        ig = _sigmoid(r.i_s[rows, :])
        a = jnp.exp(decay * rg)
        b = jnp.sqrt(jnp.maximum(1.0 - a * a, 0.0)) * (ig * r.xc_s[rows, :])
        for sh in (1, 2, 4):
            keep = sub >= sh
            a_sh = jnp.where(keep, pltpu.roll(a, sh, axis=0), 1.0)
            b_sh = jnp.where(keep, pltpu.roll(b, sh, axis=0), 0.0)
            b = a * b_sh + b
            a = a * a_sh
        return a * h_prev + b

    def run(s0, ns, r0, nr):
        rows = ns * nr
        row0 = s0 * TT + r0
        nlam = -r.lam[...]
        softplus = jnp.maximum(nlam, 0.0) + jnp.log(1.0 + jnp.exp(-jnp.abs(nlam)))
        decay = jnp.broadcast_to((-LRU_C) * softplus, (SUBLANES, LRU_DIM))
        sub = lax.broadcasted_iota(jnp.int32, (SUBLANES, LRU_DIM), 0)
        x = x_src[s0:s0 + ns, r0:r0 + nr, :].reshape(rows, D_MODEL)
        h = _rms(x, r.gmix[...]).astype(BF16)
        r.gate_s[row0:row0 + rows, :] = _mm(h, r.w_in[:, 0:LRU_DIM])
        rec3 = _mm(h, r.w_in[:, LRU_DIM:2 * LRU_DIM]).reshape(ns, nr, LRU_DIM)
        r.xp_s[s0:s0 + ns, HP + r0:HP + r0 + nr, :] = rec3
        acc = r.conv_w[H:H + 1, :] * rec3 + r.conv_b[...]
        for j in range(H):
            lo = HP - H + j + r0
            acc = acc + r.conv_w[j:j + 1, :] * r.xp_s[s0:s0 + ns, lo:lo + nr, :]
        xc = acc.reshape(rows, LRU_DIM)
        r.xc_s[row0:row0 + rows, :] = xc
        xcb = xc.astype(BF16)
        for p in range(LRU_BLOCKS // 2):
            cs = slice(pair * p, pair * (p + 1))
            ri = _mm(xcb[:, cs], r.w_ax[p])
            r.r_s[row0:row0 + rows, cs] = ri[:, 0:pair] + r.b_a[:, cs]
            r.i_s[row0:row0 + rows, cs] = ri[:, pair:2 * pair] + r.b_x[:, cs]
        h_prev = None if sample else jnp.broadcast_to(h_live[0], (SUBLANES, LRU_DIM))
        for g0 in range(row0, row0 + rows, 2 * SUBLANES):
            parts = []
            for k in range(2):
                ga = g0 + k * SUBLANES
                if sample:
                    h_prev = jnp.broadcast_to(h_live[ga // TT], (SUBLANES, LRU_DIM))
                hs = scan_group(slice(ga, ga + SUBLANES), h_prev, decay, sub)
                h_last = hs[SUBLANES - 1:SUBLANES, :]
                if sample:
                    h_live[ga // TT] = h_last
                h_prev = jnp.broadcast_to(h_last, (SUBLANES, LRU_DIM))
                parts.append(hs)
            rows2 = slice(g0, g0 + 2 * SUBLANES)
            r.act_s[rows2, :] = (jnp.concatenate(parts, axis=0) * _gelu(r.gate_s[rows2, :])).astype(BF16)
        if not sample:
            h_live[0] = h_prev[0:1, :]
        y = _mm(r.act_s[row0:row0 + rows, :], r.w_out[...]) + x
        y_dst[s0:s0 + ns, r0:r0 + nr, :] = y.reshape(ns, nr, D_MODEL)

    return [functools.partial(run, *tl) for tl in tiles]


def _odd_carry(r, TT):
    HP = SUBLANES
    r.xp_s[:, 0:HP, :] = r.xp_s[:, TT:TT + HP, :]


def _odd_conv_state(r):
    return r.xp_s[:, SUBLANES - LRU_HIST:SUBLANES, :]


def _odd_const_args(p):
    return [p[n] for n in ODD_CONSTS]


def _odd_kernel(*refs, S, TT, nt, sample):
    it = iter(refs)
    x_ref = next(it)
    r = _take(it, ODD_CONSTS)
    convst_ref = next(it) if sample else None
    hst_ref = next(it) if sample else None
    y_ref = next(it); convout_ref = next(it); hout_ref = next(it)
    r.__dict__.update(_take(it, ODD_SCRATCH).__dict__)
    t = pl.program_id(1)

    @pl.when(t == 0)
    def _():
        _odd_init(r, hout_ref, convst_ref, hst_ref)

    for tile in _odd_tiles(r, x_ref, y_ref, hout_ref, S, TT, sample):
        tile()
    _odd_carry(r, TT)

    @pl.when(t == nt - 1)
    def _():
        convout_ref[...] = _odd_conv_state(r)


def _odd_mixer(x, conv_state, h_state, p, *, S, TT):
    B, T, D = x.shape
    sample = conv_state is not None
    nb, nt = B // S, T // TT
    consts = _odd_const_args(p)
    in_specs = [pl.BlockSpec((S, TT, D), lambda b, t: (b, t, 0))]
    in_specs += [_const_spec(c.shape, 2) for c in consts]
    args = [x] + consts
    if sample:
        in_specs += [pl.BlockSpec((None, S, LRU_HIST, LRU_DIM), lambda b, t: (0, b, 0, 0)),
                     pl.BlockSpec((S, 1, LRU_DIM), lambda b, t: (b, 0, 0))]
        args += [conv_state, h_state]
    out_shape = (jax.ShapeDtypeStruct((B, T, D), F32),
                 jax.ShapeDtypeStruct((B, LRU_HIST, LRU_DIM), F32),
                 jax.ShapeDtypeStruct((B, 1, LRU_DIM), F32))
    out_specs = (pl.BlockSpec((S, TT, D), lambda b, t: (b, t, 0)),
                 pl.BlockSpec((S, LRU_HIST, LRU_DIM), lambda b, t: (b, 0, 0)),
                 pl.BlockSpec((S, 1, LRU_DIM), lambda b, t: (b, 0, 0)))
    return pl.pallas_call(
        functools.partial(_odd_kernel, S=S, TT=TT, nt=nt, sample=sample),
        grid=(nb, nt), in_specs=in_specs, out_specs=out_specs, out_shape=out_shape,
        scratch_shapes=_odd_scratch(S, TT),
        compiler_params=pltpu.CompilerParams(dimension_semantics=("arbitrary", "arbitrary"),
                                             vmem_limit_bytes=VMEM_LIMIT_BYTES),
        name="odd_mixer_sample" if sample else "odd_mixer_prompt",
    )(*args)


FFN_CONSTS = ('g', 'w_up', 'conv_w', 'conv_b', 'w_down')
FFN_SCRATCH = ('h_s', 'act_s', 'hist_s', 'work_s')


def _ffn_scratch(S, TT):
    R = S * TT
    return [pltpu.VMEM((R, D_MODEL), BF16),
            pltpu.VMEM((R, FFN_DIM), BF16),
            pltpu.VMEM((S, SUBLANES, 2 * FFN_DIM), F32),
            pltpu.VMEM((S, SUBLANES + TT, FFN_COL_CHUNK), F32)]


def _ffn_init(r, state):
    r.hist_s[...] = jnp.zeros(r.hist_s.shape, F32)
    if state is not None:
        r.hist_s[:, SUBLANES - FFN_HIST:SUBLANES, :] = state[...]


def _ffn_stages(r, x_src, y_dst, g_final, S, TT):
    R = S * TT
    HP, H, CK = SUBLANES, FFN_HIST, FFN_COL_CHUNK

    def prologue():
        x = x_src[...].reshape(R, D_MODEL)
        r.h_s[...] = _rms(x, r.g[...]).astype(BF16)

    def conv_cols(col):
        z3 = _mm(r.h_s[...], r.w_up[:, col:col + CK]).reshape(S, TT, CK)
        r.work_s[:, 0:HP, :] = r.hist_s[:, :, col:col + CK]
        r.work_s[:, HP:HP + TT, :] = z3
        zc = r.conv_w[H:H + 1, col:col + CK] * z3 + r.conv_b[:, col:col + CK]
        for j in range(H):
            zc = zc + r.conv_w[j:j + 1, col:col + CK] * r.work_s[:, HP - H + j:HP - H + j + TT, :]
        r.hist_s[:, :, col:col + CK] = r.work_s[:, TT:TT + HP, :]
        return zc.reshape(R, CK)

    def chunk(c):
        gz = conv_cols(c * CK)
        uz = conv_cols(FFN_DIM + c * CK)
        r.act_s[:, c * CK:(c + 1) * CK] = (_gelu(gz) * uz).astype(BF16)

    def epilogue():
        y = _mm(r.act_s[...], r.w_down[...]) + x_src[...].reshape(R, D_MODEL)
        if g_final is not None:
            y = _rms(y, g_final[...])
        y_dst[...] = y.reshape(S, TT, D_MODEL)

    return prologue, [functools.partial(chunk, c) for c in range(FFN_DIM // CK)], epilogue


def _ffn_state(r):
    return r.hist_s[:, SUBLANES - FFN_HIST:SUBLANES, :]


def _ffn_const_args(p, g_final):
    return [p[n] for n in FFN_CONSTS] + ([g_final] if g_final is not None else [])


def _ffn_kernel(*refs, S, TT, nt, sample, final):
    it = iter(refs)
    x_ref = next(it)
    r = _take(it, FFN_CONSTS)
    gfin_ref = next(it) if final else None
    st_ref = next(it) if sample else None
    y_ref = next(it); stout_ref = next(it)
    r.__dict__.update(_take(it, FFN_SCRATCH).__dict__)
    t = pl.program_id(1)

    @pl.when(t == 0)
    def _():
        _ffn_init(r, st_ref)

    prologue, chunks, epilogue = _ffn_stages(r, x_ref, y_ref, gfin_ref, S, TT)
    prologue()
    for ch in chunks:
        ch()
    epilogue()

    @pl.when(t == nt - 1)
    def _():
        stout_ref[...] = _ffn_state(r)


def _layer_spec(shape, layer):
    nd = len(shape) - 1
    return pl.BlockSpec((None,) + tuple(shape[1:]), lambda b, t: (layer,) + (0,) * nd,
                        pipeline_mode=pl.Buffered(1))


def _conv_ffn(x, state, layer, p, g_final, *, S, TT):
    B, T, D = x.shape
    sample = state is not None
    final = g_final is not None
    nb, nt = B // S, T // TT
    consts = [p[n] for n in FFN_CONSTS]
    in_specs = [pl.BlockSpec((S, TT, D), lambda b, t: (b, t, 0))]
    in_specs += [_layer_spec(c.shape, layer) for c in consts]
    args = [x] + consts
    if final:
        in_specs.append(_const_spec(g_final.shape, 2))
        args.append(g_final)
    if sample:
        in_specs.append(pl.BlockSpec((None, S, FFN_HIST, 2 * FFN_DIM), lambda b, t: (layer, b, 0, 0)))
        args.append(state)
    out_shape = (jax.ShapeDtypeStruct((B, T, D), F32),
                 jax.ShapeDtypeStruct((B, FFN_HIST, 2 * FFN_DIM), F32))
    out_specs = (pl.BlockSpec((S, TT, D), lambda b, t: (b, t, 0)),
                 pl.BlockSpec((S, FFN_HIST, 2 * FFN_DIM), lambda b, t: (b, 0, 0)))
    return pl.pallas_call(
        functools.partial(_ffn_kernel, S=S, TT=TT, nt=nt, sample=sample, final=final),
        grid=(nb, nt), in_specs=in_specs, out_specs=out_specs, out_shape=out_shape,
        scratch_shapes=_ffn_scratch(S, TT),
        compiler_params=pltpu.CompilerParams(dimension_semantics=("arbitrary", "arbitrary"),
                                             vmem_limit_bytes=VMEM_LIMIT_BYTES),
        name=("ffn_sample" if sample else "ffn_prompt") + ("_final" if final else ""),
    )(*args)


def _ffn_cols_kernel(*refs, B, TT, nc, final):
    it = iter(refs)
    x_ref = next(it); g_ref = next(it)
    wg_ref = next(it); wu_ref = next(it)
    cwg_ref = next(it); cwu_ref = next(it); cbg_ref = next(it); cbu_ref = next(it)
    wdn_ref = next(it)
    gfin_ref = next(it) if final else None
    stg_ref = next(it); stu_ref = next(it)
    y_ref = next(it); stgout_ref = next(it); stuout_ref = next(it)
    h_s = next(it); acc_s = next(it)

    c = pl.program_id(0)
    R = B * TT
    CK = FFN_COL_CHUNK
    H = FFN_HIST

    @pl.when(c == 0)
    def _():
        x = x_ref[...].reshape(R, D_MODEL)
        h_s[...] = _rms(x, g_ref[...]).astype(BF16)
        acc_s[...] = jnp.zeros((R, D_MODEL), F32)

    tpos = lax.broadcasted_iota(jnp.int32, (R, CK), 0) & (TT - 1)

    def conv(w_ref, cw_ref, cb_ref, st_ref, stout_ref):
        z = _mm(h_s[...], w_ref[...])
        z3 = z.reshape(B, TT, CK)
        stout_ref[...] = z3[:, TT - H:TT, :]
        prev1 = jnp.broadcast_to(st_ref[:, 1:2, :], (B, TT, CK)).reshape(R, CK)
        prev2 = jnp.broadcast_to(st_ref[:, 0:1, :], (B, TT, CK)).reshape(R, CK)
        zm1 = jnp.where(tpos >= 1, pltpu.roll(z, 1, axis=0), prev1)
        zm2 = jnp.where(tpos >= 2, pltpu.roll(z, 2, axis=0), jnp.where(tpos == 1, prev1, prev2))
        return cw_ref[2:3, :] * z + cw_ref[1:2, :] * zm1 + cw_ref[0:1, :] * zm2 + cb_ref[...]

    gz = conv(wg_ref, cwg_ref, cbg_ref, stg_ref, stgout_ref)
    uz = conv(wu_ref, cwu_ref, cbu_ref, stu_ref, stuout_ref)
    acc_s[...] += _mm((_gelu(gz) * uz).astype(BF16), wdn_ref[...])

    @pl.when(c == nc - 1)
    def _():
        y = acc_s[...] + x_ref[...].reshape(R, D_MODEL)
        if final:
            y = _rms(y, gfin_ref[...])
        y_ref[...] = y.reshape(B, TT, D_MODEL)


def _conv_ffn_sample(x, state, layer, p, g_final):
    B, T, D = x.shape
    assert T == SUBLANES and FFN_CONV_WIDTH == 3
    final = g_final is not None
    CK = FFN_COL_CHUNK
    nc = FFN_DIM // CK

    def cols(shape, off):
        nd = len(shape) - 2
        return pl.BlockSpec((None,) + tuple(shape[1:-1]) + (CK,), lambda c: (layer,) + (0,) * nd + (off + c,))

    in_specs = [pl.BlockSpec((B, T, D), lambda c: (0, 0, 0), pipeline_mode=pl.Buffered(1)),
                pl.BlockSpec((None, 1, D), lambda c: (layer, 0, 0), pipeline_mode=pl.Buffered(1)),
                cols(p['w_up'].shape, 0), cols(p['w_up'].shape, nc),
                cols(p['conv_w'].shape, 0), cols(p['conv_w'].shape, nc),
                cols(p['conv_b'].shape, 0), cols(p['conv_b'].shape, nc),
                pl.BlockSpec((None, CK, D), lambda c: (layer, c, 0))]
    args = [x, p['g'], p['w_up'], p['w_up'], p['conv_w'], p['conv_w'], p['conv_b'], p['conv_b'], p['w_down']]
    if final:
        in_specs.append(pl.BlockSpec(g_final.shape, lambda c: (0, 0), pipeline_mode=pl.Buffered(1)))
        args.append(g_final)
    in_specs += [cols(state.shape, 0), cols(state.shape, nc)]
    args += [state, state]
    out_shape = (jax.ShapeDtypeStruct((B, T, D), F32),
                 jax.ShapeDtypeStruct((B, FFN_HIST, FFN_DIM), F32),
                 jax.ShapeDtypeStruct((B, FFN_HIST, FFN_DIM), F32))
    out_specs = (pl.BlockSpec((B, T, D), lambda c: (0, 0, 0)),
                 pl.BlockSpec((B, FFN_HIST, CK), lambda c: (0, 0, c)),
                 pl.BlockSpec((B, FFN_HIST, CK), lambda c: (0, 0, c)))
    y, st_g, st_u = pl.pallas_call(
        functools.partial(_ffn_cols_kernel, B=B, TT=T, nc=nc, final=final),
        grid=(nc,), in_specs=in_specs, out_specs=out_specs, out_shape=out_shape,
        scratch_shapes=[pltpu.VMEM((B * T, D), BF16), pltpu.VMEM((B * T, D), F32)],
        compiler_params=pltpu.CompilerParams(dimension_semantics=("arbitrary",),
                                             vmem_limit_bytes=VMEM_LIMIT_BYTES),
        name="ffn_sample_final" if final else "ffn_sample",
    )(*args)
    return y, jnp.concatenate([st_g, st_u], axis=-1)


PROMPT_TILES = dict(even=dict(S=1, TT=1024), odd=dict(S=1, TT=1024), ffn=dict(S=1, TT=1024))
SAMPLE_TILES = dict(even=dict(S=16, TT=8), odd=dict(S=32, TT=8))


def _row(v):
    return v.reshape(1, -1)


def _pair_block_diag(w_a, w_x):
    def pairs(w):
        w = w.reshape(LRU_BLOCKS // 2, 2, LRU_BLOCK, LRU_BLOCK)
        z = jnp.zeros_like(w[:, 0])
        top = jnp.concatenate([w[:, 0], z], axis=-1)
        bot = jnp.concatenate([z, w[:, 1]], axis=-1)
        return jnp.concatenate([top, bot], axis=-2)
    return jnp.concatenate([pairs(w_a), pairs(w_x)], axis=-1).astype(BF16)


def kernel(x_prompt, x_sample, state_conv_a, state_ret, state_lru_conv, state_lru_h, state_ffn_conv, norm_mix, norm_ffn, norm_final, w_in_ab, conv_a_w, conv_a_b, ln_a_g, ln_a_b, gn_ret_g, w_out_ab, w_in_c, conv_c_w, conv_c_b, w_lru_a, b_lru_a, w_lru_x, b_lru_x, lru_lambda, w_out_c, w_ffn_up, ffn_conv_w, ffn_conv_b, w_ffn_down):
    pe = dict(gmix=_row(norm_mix[0]), w_in=w_in_ab[0].astype(BF16), conv_w=conv_a_w[0],
              conv_b=_row(conv_a_b[0]), ln_g=_row(ln_a_g[0]), ln_b=_row(ln_a_b[0]),
              gn_g=_row(gn_ret_g[0]), w_out=w_out_ab[0].astype(BF16))
    po = dict(gmix=_row(norm_mix[1]), w_in=w_in_c[0].astype(BF16), conv_w=conv_c_w[0],
              conv_b=_row(conv_c_b[0]), w_ax=_pair_block_diag(w_lru_a[0], w_lru_x[0]),
              b_a=_row(b_lru_a[0]), b_x=_row(b_lru_x[0]), lam=_row(lru_lambda[0]),
              w_out=w_out_c[0].astype(BF16))
    pf = dict(g=norm_ffn[:, None, :], w_up=w_ffn_up.astype(BF16), conv_w=ffn_conv_w,
              conv_b=ffn_conv_b[:, None, :], w_down=w_ffn_down.astype(BF16))
    g_final = _row(norm_final)

    xp, p_conv_a, p_ret = _even_mixer(x_prompt, None, None, pe, **PROMPT_TILES['even'])
    xp, p_ffn0 = _conv_ffn(xp, None, 0, pf, None, **PROMPT_TILES['ffn'])
    xp, p_lru_conv, p_lru_h = _odd_mixer(xp, None, None, po, **PROMPT_TILES['odd'])
    y_prompt, p_ffn1 = _conv_ffn(xp, None, 1, pf, g_final, **PROMPT_TILES['ffn'])

    xs, s_conv_a, s_ret = _even_mixer(x_sample, state_conv_a, state_ret, pe, **SAMPLE_TILES['even'])
    xs, s_ffn0 = _conv_ffn_sample(xs, state_ffn_conv, 0, pf, None)
    xs, s_lru_conv, s_lru_h = _odd_mixer(xs, state_lru_conv, state_lru_h[0][:, None, :], po,
                                         **SAMPLE_TILES['odd'])
    y_sample, s_ffn1 = _conv_ffn_sample(xs, state_ffn_conv, 1, pf, g_final)

    return (y_prompt, y_sample,
            p_conv_a[None], p_ret[None], p_lru_conv[None], p_lru_h[:, 0, :][None],
            jnp.stack([p_ffn0, p_ffn1]),
            s_conv_a[None], s_ret[None], s_lru_conv[None], s_lru_h[:, 0, :][None],
            jnp.stack([s_ffn0, s_ffn1]))
```

```python
import functools
import math
import types

import numpy as np
import jax
import jax.numpy as jnp
from jax import lax
from jax.experimental import pallas as pl
from jax.experimental.pallas import tpu as pltpu

F32 = jnp.float32
BF16 = jnp.bfloat16

D_MODEL = 1024
CONV_A_DIM = 512
CONV_A_WIDTH = 31
CONV_A_HIST = CONV_A_WIDTH - 1
CONV_A_HIST_PAD = 32
CONV_A_ROW_BLOCK = 32
CONV_A_SEQ_BLOCK = 4
LN_EPS = 1e-5
RET_HEADS = 4
RET_HEAD_DIM = 128
RET_DIM = RET_HEADS * RET_HEAD_DIM
RET_CHUNK = 128
ROPE_BASE = 10000.0
GN_EPS = 1e-5
IN_AB_DIM = 2 * CONV_A_DIM + 4 * RET_DIM
Z_SECTION = 512
LRU_DIM = 1024
LRU_BLOCKS = 8
LRU_BLOCK = LRU_DIM // LRU_BLOCKS
LRU_CONV_WIDTH = 4
LRU_HIST = LRU_CONV_WIDTH - 1
LRU_C = 8.0
ODD_SUB_ROWS = 128
FFN_DIM = 2816
FFN_CONV_WIDTH = 3
FFN_HIST = FFN_CONV_WIDTH - 1
FFN_COL_CHUNK = 256
RMS_EPS = 1e-6
PAST_LEN = 16384
SUBLANES = 8
SAMPLE_GROUP = RET_CHUNK // SUBLANES

VMEM_LIMIT_BYTES = 56 * 1024 * 1024


def _rms(x, g):
    return x * lax.rsqrt(jnp.mean(x * x, axis=-1, keepdims=True) + RMS_EPS) * g


def _sigmoid(x):
    return 1.0 / (1.0 + jnp.exp(-x))


def _gelu(x):
    c = math.sqrt(2.0 / math.pi)
    return 0.5 * x * (1.0 + jnp.tanh(c * (x + 0.044715 * (x * x * x))))


def _mm(a, b):
    return jnp.dot(a, b, preferred_element_type=F32)


def _const_spec(shape, grid_rank=2):
    nd = len(shape)
    assert grid_rank == 2
    return pl.BlockSpec(shape, lambda b, t: (0,) * nd, pipeline_mode=pl.Buffered(1))


def _take(it, names):
    return types.SimpleNamespace(**{n: next(it) for n in names})


EVEN_CONSTS = ('cos', 'sin', 'dmask', 'qd', 'kd', 'cd', 'gmix', 'w_in', 'conv_w', 'conv_wd', 'conv_b',
               'ln_g', 'ln_b', 'gn_g', 'w_out')
EVEN_SCRATCH = ('xp_s', 'xs_s', 'yb_s', 'cv_s', 'ya_s')


def _conv_a_tap(j):
    off = CONV_A_HIST_PAD - CONV_A_HIST + j
    return off % SUBLANES, off - off % SUBLANES


CONV_A_VPU_TAPS = tuple(j for j in range(CONV_A_WIDTH) if _conv_a_tap(j)[0] <= 2)
CONV_A_MXU_TAPS = tuple(j for j in range(CONV_A_WIDTH) if _conv_a_tap(j)[0] > 2)


def _even_scratch(S, TT, sample):
    return [pltpu.VMEM((S, CONV_A_HIST_PAD + TT, CONV_A_DIM), F32),
            pltpu.VMEM((SUBLANES - 1, S, TT + CONV_A_HIST_PAD - SUBLANES, CONV_A_DIM), F32),
            pltpu.VMEM((S * TT, RET_DIM), BF16),
            pltpu.VMEM((S * TT, CONV_A_DIM), F32),
            pltpu.VMEM((S * TT, CONV_A_DIM), BF16)]


def _even_init(r, ret_live, conv_state, ret_state):
    S = r.xp_s.shape[0]
    HP, H = CONV_A_HIST_PAD, CONV_A_HIST
    r.xp_s[:, 0:HP, :] = jnp.zeros((S, HP, CONV_A_DIM), F32)
    if conv_state is not None:
        r.xp_s[:, HP - H:HP, :] = conv_state[...]
        ret_live[...] = ret_state[...]
    else:
        ret_live[...] = jnp.zeros(ret_live.shape, F32)


def _even_tiles(r, x_src, y_dst, ret_live, S, TT, sample):
    HP, H = CONV_A_HIST_PAD, CONV_A_HIST
    if sample:
        tiles = [(s0, SAMPLE_GROUP, 0, TT) for s0 in range(0, S, SAMPLE_GROUP)]
    else:
        tiles = [(0, 1, r0, RET_CHUNK) for r0 in range(0, TT, RET_CHUNK)]
    o0 = 2 * CONV_A_DIM
    scale = RET_HEAD_DIM ** -0.5

    def run(s0, ns, r0, nr):
        rows = ns * nr
        row0 = s0 * TT + r0
        x = x_src[s0:s0 + ns, r0:r0 + nr, :].reshape(rows, D_MODEL)
        h = _rms(x, r.gmix[...]).astype(BF16)
        zs = [_mm(h, r.w_in[:, c0:c0 + Z_SECTION]) for c0 in range(0, IN_AB_DIM, Z_SECTION)]

        def zcols(lo, hi):
            k = lo // Z_SECTION
            return zs[k][:, lo - k * Z_SECTION:hi - k * Z_SECTION]

        u = zcols(0, CONV_A_DIM) * _sigmoid(zcols(CONV_A_DIM, 2 * CONV_A_DIM))
        r.xp_s[s0:s0 + ns, HP + r0:HP + r0 + nr, :] = u.reshape(ns, nr, CONV_A_DIM)

        cos = r.cos[row0:row0 + rows, :]
        sin = r.sin[row0:row0 + rows, :]
        gate = zcols(o0 + 3 * RET_DIM, o0 + 4 * RET_DIM)
        o_parts = []
        for hh in range(RET_HEADS):
            lo = hh * RET_HEAD_DIM
            hi = lo + RET_HEAD_DIM
            qh = zcols(o0 + lo, o0 + hi)
            kh = zcols(o0 + RET_DIM + lo, o0 + RET_DIM + hi)
            qc = qh * cos + pltpu.roll(qh, RET_HEAD_DIM // 2, axis=1) * sin
            kc = (kh * cos + pltpu.roll(kh, RET_HEAD_DIM // 2, axis=1) * sin) * scale
            vc = zcols(o0 + 2 * RET_DIM + lo, o0 + 2 * RET_DIM + hi)
            qb = qc.astype(BF16)
            vb = vc.astype(BF16)
            scores = lax.dot_general(qb, kc.astype(BF16), (((1,), (1,)), ((), ())),
                                     preferred_element_type=F32) * r.dmask[hh]
            inner = _mm(scores.astype(BF16), vb)
            if not sample:
                kdv = (kc * r.kd[hh]).astype(BF16)
                st = ret_live[0, hh]
                cross = _mm(qb, st.astype(BF16))
                upd = lax.dot_general(kdv, vb, (((0,), (0,)), ((), ())), preferred_element_type=F32)
                ret_live[0, hh] = st * r.cd[hh] + upd
            else:
                parts = []
                for sq in range(SAMPLE_GROUP):
                    sidx = s0 + sq
                    rs = slice(sq * SUBLANES, (sq + 1) * SUBLANES)
                    st = ret_live[sidx, hh]
                    parts.append(_mm(qc[rs].astype(BF16), st.astype(BF16)))
                    kdv_s = (kc[rs] * r.kd[hh, rs, :]).astype(BF16)
                    upd = lax.dot_general(kdv_s, vc[rs].astype(BF16), (((0,), (0,)), ((), ())),
                                          preferred_element_type=F32)
                    ret_live[sidx, hh] = st * r.cd[hh] + upd
                cross = jnp.concatenate(parts, axis=0)
            o = inner + cross * r.qd[hh]
            mu_o = jnp.mean(o, axis=-1, keepdims=True)
            oc = o - mu_o
            var_o = jnp.mean(oc * oc, axis=-1, keepdims=True)
            o_parts.append(oc * lax.rsqrt(var_o + GN_EPS) * r.gn_g[:, lo:hi])

        yb = jnp.concatenate(o_parts, axis=-1) * (gate * _sigmoid(gate))
        r.yb_s[row0:row0 + rows, :] = yb.astype(BF16)

    def conv_and_project():
        R = S * TT
        L = TT + HP - SUBLANES
        for b in range(1, SUBLANES):
            r.xs_s[b - 1] = r.xp_s[:, b:b + L, :]
        half = CONV_A_DIM // 2
        if sample:
            blocks = [(sb, CONV_A_SEQ_BLOCK, 0, TT) for sb in range(0, S, CONV_A_SEQ_BLOCK)]
        else:
            blocks = [(0, 1, rb, CONV_A_ROW_BLOCK) for rb in range(0, TT, CONV_A_ROW_BLOCK)]
        for sb, nsb, rb, nrb in blocks:
            rowb = sb * TT + rb
            acc = jnp.zeros((nsb, nrb, CONV_A_DIM), F32) + r.conv_b[...]
            for j in CONV_A_VPU_TAPS:
                b, lo = _conv_a_tap(j)
                if b == 0:
                    win = r.xp_s[sb:sb + nsb, lo + rb:lo + rb + nrb, :]
                else:
                    win = r.xs_s[b - 1, sb:sb + nsb, lo + rb:lo + rb + nrb, :]
                acc = acc + r.conv_w[j:j + 1, :] * win
            r.cv_s[rowb:rowb + nsb * nrb, :] = acc.reshape(nsb * nrb, CONV_A_DIM)
        mxu_parts = []
        for p in range(2):
            cs = slice(p * half, (p + 1) * half)
            acc = jnp.zeros((R, half), F32)
            for k, j in enumerate(CONV_A_MXU_TAPS):
                b, lo = _conv_a_tap(j)
                win = r.xs_s[b - 1, :, lo:lo + TT, cs]
                acc = acc + _mm(win.reshape(R, half).astype(BF16), r.conv_wd[k, p])
            mxu_parts.append(acc)
        cv_mxu = jnp.concatenate(mxu_parts, axis=-1)
        for sb, nsb, rb, nrb in blocks:
            rowb = sb * TT + rb
            rows_b = slice(rowb, rowb + nsb * nrb)
            cv = r.cv_s[rows_b, :] + cv_mxu[rows_b, :]
            mu = jnp.mean(cv, axis=-1, keepdims=True)
            cvc = cv - mu
            var = jnp.mean(cvc * cvc, axis=-1, keepdims=True)
            ln = cvc * lax.rsqrt(var + LN_EPS) * r.ln_g[...] + r.ln_b[...]
            r.ya_s[rowb:rowb + nsb * nrb, :] = (ln * _sigmoid(ln)).astype(BF16)
        y = (_mm(r.ya_s[...], r.w_out[0:CONV_A_DIM, :])
             + _mm(r.yb_s[...], r.w_out[CONV_A_DIM:CONV_A_DIM + RET_DIM, :])
             + x_src[...].reshape(R, D_MODEL))
        y_dst[...] = y.reshape(S, TT, D_MODEL)

    return [functools.partial(run, *tl) for tl in tiles] + [conv_and_project]


def _even_carry(r, TT):
    HP = CONV_A_HIST_PAD
    r.xp_s[:, 0:HP, :] = r.xp_s[:, TT:TT + HP, :]


def _even_conv_state(r):
    return r.xp_s[:, CONV_A_HIST_PAD - CONV_A_HIST:CONV_A_HIST_PAD, :]


def _rope_tables(pos0, T, reps):
    d = RET_HEAD_DIM
    inv_freq = ROPE_BASE ** (-np.arange(0, d, 2, dtype=np.float64) / d)
    ang = (pos0 + np.arange(T, dtype=np.float64))[:, None] * inv_freq[None, :]
    cos = np.cos(ang)
    sin = np.sin(ang)
    cos2 = np.concatenate([cos, cos], axis=-1)
    sin2 = np.concatenate([-sin, sin], axis=-1)
    return (jnp.asarray(np.tile(cos2, (reps, 1)), F32), jnp.asarray(np.tile(sin2, (reps, 1)), F32))


def _decay_tables(c, groups):
    nh = RET_HEADS
    log_gamma = np.log(1.0 - 2.0 ** (-5.0 - np.arange(nh, dtype=np.float64)))
    idx = np.arange(c, dtype=np.float64)
    rel = idx[:, None] - idx[None, :]
    dmask = np.where(rel >= 0, np.exp(np.maximum(rel, 0.0)[None] * log_gamma[:, None, None]), 0.0)
    qd = np.exp((idx + 1.0)[None, :] * log_gamma[:, None])
    kd = np.exp((c - 1.0 - idx)[None, :] * log_gamma[:, None])
    cd = np.exp(c * log_gamma)
    if groups > 1:
        eye = np.eye(groups)
        dmask = np.einsum('gk,hij->hgikj', eye, dmask).reshape(nh, groups * c, groups * c)
        qd = np.tile(qd, (1, groups))
        kd = np.tile(kd, (1, groups))
    n = groups * c
    qd = np.broadcast_to(qd[:, :, None], (nh, n, RET_HEAD_DIM))
    kd = np.broadcast_to(kd[:, :, None], (nh, n, RET_HEAD_DIM))
    cd = np.broadcast_to(cd[:, None, None], (nh, 1, RET_HEAD_DIM))
    return tuple(jnp.asarray(np.ascontiguousarray(a), F32) for a in (dmask, qd, kd, cd))


def _even_const_args(p, pos0, T, reps, sample):
    cos, sin = _rope_tables(pos0, T, reps)
    dmask, qd, kd, cd = _decay_tables(T, SAMPLE_GROUP) if sample else _decay_tables(RET_CHUNK, 1)
    return [cos, sin, dmask, qd, kd, cd, p['gmix'], p['w_in'], p['conv_w'], p['conv_wd'], p['conv_b'],
            p['ln_g'], p['ln_b'], p['gn_g'], p['w_out']]


def _even_kernel(*refs, S, TT, nt, sample):
    it = iter(refs)
    x_ref = next(it)
    r = _take(it, EVEN_CONSTS)
    convst_ref = next(it) if sample else None
    retst_ref = next(it) if sample else None
    y_ref = next(it); convout_ref = next(it); retout_ref = next(it)
    r.__dict__.update(_take(it, EVEN_SCRATCH).__dict__)
    t = pl.program_id(1)

    @pl.when(t == 0)
    def _():
        _even_init(r, retout_ref, convst_ref, retst_ref)

    for tile in _even_tiles(r, x_ref, y_ref, retout_ref, S, TT, sample):
        tile()
    _even_carry(r, TT)

    @pl.when(t == nt - 1)
    def _():
        convout_ref[...] = _even_conv_state(r)


def _even_mixer(x, conv_state, ret_state, p, *, S, TT):
    B, T, D = x.shape
    sample = conv_state is not None
    nb, nt = B // S, T // TT
    R = S * TT
    if sample:
        assert TT == T == SUBLANES and S % SAMPLE_GROUP == 0
        consts = _even_const_args(p, PAST_LEN, T, S, True)
    else:
        assert S == 1 and TT % RET_CHUNK == 0
        consts = _even_const_args(p, 0, T, 1, False)
    in_specs = [pl.BlockSpec((S, TT, D), lambda b, t: (b, t, 0)),
                pl.BlockSpec((R, RET_HEAD_DIM), lambda b, t: (t, 0)),
                pl.BlockSpec((R, RET_HEAD_DIM), lambda b, t: (t, 0))]
    in_specs += [_const_spec(c.shape, 2) for c in consts[2:]]
    args = [x] + consts
    if sample:
        in_specs += [pl.BlockSpec((None, S, CONV_A_HIST, CONV_A_DIM), lambda b, t: (0, b, 0, 0)),
                     pl.BlockSpec((None, S, RET_HEADS, RET_HEAD_DIM, RET_HEAD_DIM),
                                  lambda b, t: (0, b, 0, 0, 0))]
        args += [conv_state, ret_state]
    out_shape = (jax.ShapeDtypeStruct((B, T, D), F32),
                 jax.ShapeDtypeStruct((B, CONV_A_HIST, CONV_A_DIM), F32),
                 jax.ShapeDtypeStruct((B, RET_HEADS, RET_HEAD_DIM, RET_HEAD_DIM), F32))
    out_specs = (pl.BlockSpec((S, TT, D), lambda b, t: (b, t, 0)),
                 pl.BlockSpec((S, CONV_A_HIST, CONV_A_DIM), lambda b, t: (b, 0, 0)),
                 pl.BlockSpec((S, RET_HEADS, RET_HEAD_DIM, RET_HEAD_DIM), lambda b, t: (b, 0, 0, 0)))
    return pl.pallas_call(
        functools.partial(_even_kernel, S=S, TT=TT, nt=nt, sample=sample),
        grid=(nb, nt), in_specs=in_specs, out_specs=out_specs, out_shape=out_shape,
        scratch_shapes=_even_scratch(S, TT, sample),
        compiler_params=pltpu.CompilerParams(dimension_semantics=("arbitrary", "arbitrary"),
                                             vmem_limit_bytes=VMEM_LIMIT_BYTES),
        name="even_mixer_sample" if sample else "even_mixer_prompt",
    )(*args)


ODD_CONSTS = ('gmix', 'w_in', 'conv_w', 'conv_b', 'w_ax', 'b_a', 'b_x', 'lam', 'w_out')
ODD_SCRATCH = ('xp_s', 'gate_s', 'xc_s', 'r_s', 'i_s', 'act_s')


def _odd_scratch(S, TT):
    R = S * TT
    return ([pltpu.VMEM((S, SUBLANES + TT, LRU_DIM), F32)]
            + [pltpu.VMEM((R, LRU_DIM), F32) for _ in range(4)]
            + [pltpu.VMEM((R, LRU_DIM), BF16)])


def _odd_init(r, h_live, conv_state, h_state):
    S = r.xp_s.shape[0]
    HP, H = SUBLANES, LRU_HIST
    r.xp_s[:, 0:HP, :] = jnp.zeros((S, HP, LRU_DIM), F32)
    if conv_state is not None:
        r.xp_s[:, HP - H:HP, :] = conv_state[...]
        h_live[...] = h_state[...]
    else:
        h_live[...] = jnp.zeros(h_live.shape, F32)


def _odd_tiles(r, x_src, y_dst, h_live, S, TT, sample):
    HP, H = SUBLANES, LRU_HIST
    if sample:
        tiles = [(0, S, 0, TT)]
    else:
        tiles = [(0, 1, r0, min(TT, ODD_SUB_ROWS)) for r0 in range(0, TT, ODD_SUB_ROWS)]
    pair = 2 * LRU_BLOCK

    def scan_group(rows, h_prev, decay, sub):
        rg = _sigmoid(r.r_s[rows, :])
        ig = _sigmoid(r.i_s[rows, :])
        a = jnp.exp(decay * rg)
        b = jnp.sqrt(jnp.maximum(1.0 - a * a, 0.0)) * (ig * r.xc_s[rows, :])
        for sh in (1, 2, 4):
            keep = sub >= sh
            a_sh = jnp.where(keep, pltpu.roll(a, sh, axis=0), 1.0)
            b_sh = jnp.where(keep, pltpu.roll(b, sh, axis=0), 0.0)
            b = a * b_sh + b
            a = a * a_sh
        return a * h_prev + b

    def run(s0, ns, r0, nr):
        rows = ns * nr
        row0 = s0 * TT + r0
        nlam = -r.lam[...]
        softplus = jnp.maximum(nlam, 0.0) + jnp.log(1.0 + jnp.exp(-jnp.abs(nlam)))
        decay = jnp.broadcast_to((-LRU_C) * softplus, (SUBLANES, LRU_DIM))
        sub = lax.broadcasted_iota(jnp.int32, (SUBLANES, LRU_DIM), 0)
        x = x_src[s0:s0 + ns, r0:r0 + nr, :].reshape(rows, D_MODEL)
        h = _rms(x, r.gmix[...]).astype(BF16)
        r.gate_s[row0:row0 + rows, :] = _mm(h, r.w_in[:, 0:LRU_DIM])
        rec3 = _mm(h, r.w_in[:, LRU_DIM:2 * LRU_DIM]).reshape(ns, nr, LRU_DIM)
        r.xp_s[s0:s0 + ns, HP + r0:HP + r0 + nr, :] = rec3
        acc = r.conv_w[H:H + 1, :] * rec3 + r.conv_b[...]
        for j in range(H):
            lo = HP - H + j + r0
            acc = acc + r.conv_w[j:j + 1, :] * r.xp_s[s0:s0 + ns, lo:lo + nr, :]
        xc = acc.reshape(rows, LRU_DIM)
        r.xc_s[row0:row0 + rows, :] = xc
        xcb = xc.astype(BF16)
        for p in range(LRU_BLOCKS // 2):
            cs = slice(pair * p, pair * (p + 1))
            ri = _mm(xcb[:, cs], r.w_ax[p])
            r.r_s[row0:row0 + rows, cs] = ri[:, 0:pair] + r.b_a[:, cs]
            r.i_s[row0:row0 + rows, cs] = ri[:, pair:2 * pair] + r.b_x[:, cs]
        h_prev = None if sample else jnp.broadcast_to(h_live[0], (SUBLANES, LRU_DIM))
        for g0 in range(row0, row0 + rows, 2 * SUBLANES):
            parts = []
            for k in range(2):
                ga = g0 + k * SUBLANES
                if sample:
                    h_prev = jnp.broadcast_to(h_live[ga // TT], (SUBLANES, LRU_DIM))
                hs = scan_group(slice(ga, ga + SUBLANES), h_prev, decay, sub)
                h_last = hs[SUBLANES - 1:SUBLANES, :]
                if sample:
                    h_live[ga // TT] = h_last
                h_prev = jnp.broadcast_to(h_last, (SUBLANES, LRU_DIM))
                parts.append(hs)
            rows2 = slice(g0, g0 + 2 * SUBLANES)
            r.act_s[rows2, :] = (jnp.concatenate(parts, axis=0) * _gelu(r.gate_s[rows2, :])).astype(BF16)
        if not sample:
            h_live[0] = h_prev[0:1, :]
        y = _mm(r.act_s[row0:row0 + rows, :], r.w_out[...]) + x
        y_dst[s0:s0 + ns, r0:r0 + nr, :] = y.reshape(ns, nr, D_MODEL)

    return [functools.partial(run, *tl) for tl in tiles]


def _odd_carry(r, TT):
    HP = SUBLANES
    r.xp_s[:, 0:HP, :] = r.xp_s[:, TT:TT + HP, :]


def _odd_conv_state(r):
    return r.xp_s[:, SUBLANES - LRU_HIST:SUBLANES, :]


def _odd_const_args(p):
    return [p[n] for n in ODD_CONSTS]


def _odd_kernel(*refs, S, TT, nt, sample):
    it = iter(refs)
    x_ref = next(it)
    r = _take(it, ODD_CONSTS)
    convst_ref = next(it) if sample else None
    hst_ref = next(it) if sample else None
    y_ref = next(it); convout_ref = next(it); hout_ref = next(it)
    r.__dict__.update(_take(it, ODD_SCRATCH).__dict__)
    t = pl.program_id(1)

    @pl.when(t == 0)
    def _():
        _odd_init(r, hout_ref, convst_ref, hst_ref)

    for tile in _odd_tiles(r, x_ref, y_ref, hout_ref, S, TT, sample):
        tile()
    _odd_carry(r, TT)

    @pl.when(t == nt - 1)
    def _():
        convout_ref[...] = _odd_conv_state(r)


def _odd_mixer(x, conv_state, h_state, p, *, S, TT):
    B, T, D = x.shape
    sample = conv_state is not None
    nb, nt = B // S, T // TT
    consts = _odd_const_args(p)
    in_specs = [pl.BlockSpec((S, TT, D), lambda b, t: (b, t, 0))]
    in_specs += [_const_spec(c.shape, 2) for c in consts]
    args = [x] + consts
    if sample:
        in_specs += [pl.BlockSpec((None, S, LRU_HIST, LRU_DIM), lambda b, t: (0, b, 0, 0)),
                     pl.BlockSpec((S, 1, LRU_DIM), lambda b, t: (b, 0, 0))]
        args += [conv_state, h_state]
    out_shape = (jax.ShapeDtypeStruct((B, T, D), F32),
                 jax.ShapeDtypeStruct((B, LRU_HIST, LRU_DIM), F32),
                 jax.ShapeDtypeStruct((B, 1, LRU_DIM), F32))
    out_specs = (pl.BlockSpec((S, TT, D), lambda b, t: (b, t, 0)),
                 pl.BlockSpec((S, LRU_HIST, LRU_DIM), lambda b, t: (b, 0, 0)),
                 pl.BlockSpec((S, 1, LRU_DIM), lambda b, t: (b, 0, 0)))
    return pl.pallas_call(
        functools.partial(_odd_kernel, S=S, TT=TT, nt=nt, sample=sample),
        grid=(nb, nt), in_specs=in_specs, out_specs=out_specs, out_shape=out_shape,
        scratch_shapes=_odd_scratch(S, TT),
        compiler_params=pltpu.CompilerParams(dimension_semantics=("arbitrary", "arbitrary"),
                                             vmem_limit_bytes=VMEM_LIMIT_BYTES),
        name="odd_mixer_sample" if sample else "odd_mixer_prompt",
    )(*args)


FFN_CONSTS = ('g', 'w_up', 'conv_w', 'conv_b', 'w_down')
FFN_SCRATCH = ('h_s', 'act_s', 'hist_s', 'work_s')


def _ffn_scratch(S, TT):
    R = S * TT
    return [pltpu.VMEM((R, D_MODEL), BF16),
            pltpu.VMEM((R, FFN_DIM), BF16),
            pltpu.VMEM((S, SUBLANES, 2 * FFN_DIM), F32),
            pltpu.VMEM((S, SUBLANES + TT, FFN_COL_CHUNK), F32)]


def _ffn_init(r, state):
    r.hist_s[...] = jnp.zeros(r.hist_s.shape, F32)
    if state is not None:
        r.hist_s[:, SUBLANES - FFN_HIST:SUBLANES, :] = state[...]


def _ffn_stages(r, x_src, y_dst, g_final, S, TT):
    R = S * TT
    HP, H, CK = SUBLANES, FFN_HIST, FFN_COL_CHUNK

    def prologue():
        x = x_src[...].reshape(R, D_MODEL)
        r.h_s[...] = _rms(x, r.g[...]).astype(BF16)

    def conv_cols(col):
        z3 = _mm(r.h_s[...], r.w_up[:, col:col + CK]).reshape(S, TT, CK)
        r.work_s[:, 0:HP, :] = r.hist_s[:, :, col:col + CK]
        r.work_s[:, HP:HP + TT, :] = z3
        zc = r.conv_w[H:H + 1, col:col + CK] * z3 + r.conv_b[:, col:col + CK]
        for j in range(H):
            zc = zc + r.conv_w[j:j + 1, col:col + CK] * r.work_s[:, HP - H + j:HP - H + j + TT, :]
        r.hist_s[:, :, col:col + CK] = r.work_s[:, TT:TT + HP, :]
        return zc.reshape(R, CK)

    def chunk(c):
        gz = conv_cols(c * CK)
        uz = conv_cols(FFN_DIM + c * CK)
        r.act_s[:, c * CK:(c + 1) * CK] = (_gelu(gz) * uz).astype(BF16)

    def epilogue():
        y = _mm(r.act_s[...], r.w_down[...]) + x_src[...].reshape(R, D_MODEL)
        if g_final is not None:
            y = _rms(y, g_final[...])
        y_dst[...] = y.reshape(S, TT, D_MODEL)

    return prologue, [functools.partial(chunk, c) for c in range(FFN_DIM // CK)], epilogue


def _ffn_state(r):
    return r.hist_s[:, SUBLANES - FFN_HIST:SUBLANES, :]


def _ffn_const_args(p, g_final):
    return [p[n] for n in FFN_CONSTS] + ([g_final] if g_final is not None else [])


def _ffn_kernel(*refs, S, TT, nt, sample, final):
    it = iter(refs)
    x_ref = next(it)
    r = _take(it, FFN_CONSTS)
    gfin_ref = next(it) if final else None
    st_ref = next(it) if sample else None
    y_ref = next(it); stout_ref = next(it)
    r.__dict__.update(_take(it, FFN_SCRATCH).__dict__)
    t = pl.program_id(1)

    @pl.when(t == 0)
    def _():
        _ffn_init(r, st_ref)

    prologue, chunks, epilogue = _ffn_stages(r, x_ref, y_ref, gfin_ref, S, TT)
    prologue()
    for ch in chunks:
        ch()
    epilogue()

    @pl.when(t == nt - 1)
    def _():
        stout_ref[...] = _ffn_state(r)


def _layer_spec(shape, layer):
    nd = len(shape) - 1
    return pl.BlockSpec((None,) + tuple(shape[1:]), lambda b, t: (layer,) + (0,) * nd,
                        pipeline_mode=pl.Buffered(1))


def _conv_ffn(x, state, layer, p, g_final, *, S, TT):
    B, T, D = x.shape
    sample = state is not None
    final = g_final is not None
    nb, nt = B // S, T // TT
    consts = [p[n] for n in FFN_CONSTS]
    in_specs = [pl.BlockSpec((S, TT, D), lambda b, t: (b, t, 0))]
    in_specs += [_layer_spec(c.shape, layer) for c in consts]
    args = [x] + consts
    if final:
        in_specs.append(_const_spec(g_final.shape, 2))
        args.append(g_final)
    if sample:
        in_specs.append(pl.BlockSpec((None, S, FFN_HIST, 2 * FFN_DIM), lambda b, t: (layer, b, 0, 0)))
        args.append(state)
    out_shape = (jax.ShapeDtypeStruct((B, T, D), F32),
                 jax.ShapeDtypeStruct((B, FFN_HIST, 2 * FFN_DIM), F32))
    out_specs = (pl.BlockSpec((S, TT, D), lambda b, t: (b, t, 0)),
                 pl.BlockSpec((S, FFN_HIST, 2 * FFN_DIM), lambda b, t: (b, 0, 0)))
    return pl.pallas_call(
        functools.partial(_ffn_kernel, S=S, TT=TT, nt=nt, sample=sample, final=final),
        grid=(nb, nt), in_specs=in_specs, out_specs=out_specs, out_shape=out_shape,
        scratch_shapes=_ffn_scratch(S, TT),
        compiler_params=pltpu.CompilerParams(dimension_semantics=("arbitrary", "arbitrary"),
                                             vmem_limit_bytes=VMEM_LIMIT_BYTES),
        name=("ffn_sample" if sample else "ffn_prompt") + ("_final" if final else ""),
    )(*args)


def _ffn_cols_kernel(*refs, B, TT, nc, final):
    it = iter(refs)
    x_ref = next(it); g_ref = next(it)
    wg_ref = next(it); wu_ref = next(it)
    cwg_ref = next(it); cwu_ref = next(it); cbg_ref = next(it); cbu_ref = next(it)
    wdn_ref = next(it)
    gfin_ref = next(it) if final else None
    stg_ref = next(it); stu_ref = next(it)
    y_ref = next(it); stgout_ref = next(it); stuout_ref = next(it)
    h_s = next(it); acc_s = next(it)

    c = pl.program_id(0)
    R = B * TT
    CK = FFN_COL_CHUNK
    H = FFN_HIST

    @pl.when(c == 0)
    def _():
        x = x_ref[...].reshape(R, D_MODEL)
        h_s[...] = _rms(x, g_ref[...]).astype(BF16)
        acc_s[...] = jnp.zeros((R, D_MODEL), F32)

    tpos = lax.broadcasted_iota(jnp.int32, (R, CK), 0) & (TT - 1)

    def conv(w_ref, cw_ref, cb_ref, st_ref, stout_ref):
        z = _mm(h_s[...], w_ref[...])
        z3 = z.reshape(B, TT, CK)
        stout_ref[...] = z3[:, TT - H:TT, :]
        prev1 = jnp.broadcast_to(st_ref[:, 1:2, :], (B, TT, CK)).reshape(R, CK)
        prev2 = jnp.broadcast_to(st_ref[:, 0:1, :], (B, TT, CK)).reshape(R, CK)
        zm1 = jnp.where(tpos >= 1, pltpu.roll(z, 1, axis=0), prev1)
        zm2 = jnp.where(tpos >= 2, pltpu.roll(z, 2, axis=0), jnp.where(tpos == 1, prev1, prev2))
        return cw_ref[2:3, :] * z + cw_ref[1:2, :] * zm1 + cw_ref[0:1, :] * zm2 + cb_ref[...]

    gz = conv(wg_ref, cwg_ref, cbg_ref, stg_ref, stgout_ref)
    uz = conv(wu_ref, cwu_ref, cbu_ref, stu_ref, stuout_ref)
    acc_s[...] += _mm((_gelu(gz) * uz).astype(BF16), wdn_ref[...])

    @pl.when(c == nc - 1)
    def _():
        y = acc_s[...] + x_ref[...].reshape(R, D_MODEL)
        if final:
            y = _rms(y, gfin_ref[...])
        y_ref[...] = y.reshape(B, TT, D_MODEL)


def _conv_ffn_sample(x, state, layer, p, g_final):
    B, T, D = x.shape
    assert T == SUBLANES and FFN_CONV_WIDTH == 3
    final = g_final is not None
    CK = FFN_COL_CHUNK
    nc = FFN_DIM // CK

    def cols(shape, off):
        nd = len(shape) - 2
        return pl.BlockSpec((None,) + tuple(shape[1:-1]) + (CK,), lambda c: (layer,) + (0,) * nd + (off + c,))

    in_specs = [pl.BlockSpec((B, T, D), lambda c: (0, 0, 0), pipeline_mode=pl.Buffered(1)),
                pl.BlockSpec((None, 1, D), lambda c: (layer, 0, 0), pipeline_mode=pl.Buffered(1)),
                cols(p['w_up'].shape, 0), cols(p['w_up'].shape, nc),
                cols(p['conv_w'].shape, 0), cols(p['conv_w'].shape, nc),
                cols(p['conv_b'].shape, 0), cols(p['conv_b'].shape, nc),
                pl.BlockSpec((None, CK, D), lambda c: (layer, c, 0))]
    args = [x, p['g'], p['w_up'], p['w_up'], p['conv_w'], p['conv_w'], p['conv_b'], p['conv_b'], p['w_down']]
    if final:
        in_specs.append(pl.BlockSpec(g_final.shape, lambda c: (0, 0), pipeline_mode=pl.Buffered(1)))
        args.append(g_final)
    in_specs += [cols(state.shape, 0), cols(state.shape, nc)]
    args += [state, state]
    out_shape = (jax.ShapeDtypeStruct((B, T, D), F32),
                 jax.ShapeDtypeStruct((B, FFN_HIST, FFN_DIM), F32),
                 jax.ShapeDtypeStruct((B, FFN_HIST, FFN_DIM), F32))
    out_specs = (pl.BlockSpec((B, T, D), lambda c: (0, 0, 0)),
                 pl.BlockSpec((B, FFN_HIST, CK), lambda c: (0, 0, c)),
                 pl.BlockSpec((B, FFN_HIST, CK), lambda c: (0, 0, c)))
    y, st_g, st_u = pl.pallas_call(
        functools.partial(_ffn_cols_kernel, B=B, TT=T, nc=nc, final=final),
        grid=(nc,), in_specs=in_specs, out_specs=out_specs, out_shape=out_shape,
        scratch_shapes=[pltpu.VMEM((B * T, D), BF16), pltpu.VMEM((B * T, D), F32)],
        compiler_params=pltpu.CompilerParams(dimension_semantics=("arbitrary",),
                                             vmem_limit_bytes=VMEM_LIMIT_BYTES),
        name="ffn_sample_final" if final else "ffn_sample",
    )(*args)
    return y, jnp.concatenate([st_g, st_u], axis=-1)


PROMPT_TILES = dict(even=dict(S=1, TT=512), odd=dict(S=1, TT=1024), ffn=dict(S=1, TT=1024))
SAMPLE_TILES = dict(even=dict(S=16, TT=8), odd=dict(S=32, TT=8))


def _row(v):
    return v.reshape(1, -1)


def _diag_taps(w):
    half = CONV_A_DIM // 2
    w = w[np.array(CONV_A_MXU_TAPS)].reshape(len(CONV_A_MXU_TAPS), 2, 1, half)
    return (jnp.eye(half, dtype=F32)[None, None] * w).astype(BF16)


def _pair_block_diag(w_a, w_x):
    def pairs(w):
        w = w.reshape(LRU_BLOCKS // 2, 2, LRU_BLOCK, LRU_BLOCK)
        z = jnp.zeros_like(w[:, 0])
        top = jnp.concatenate([w[:, 0], z], axis=-1)
        bot = jnp.concatenate([z, w[:, 1]], axis=-1)
        return jnp.concatenate([top, bot], axis=-2)
    return jnp.concatenate([pairs(w_a), pairs(w_x)], axis=-1).astype(BF16)


def kernel(x_prompt, x_sample, state_conv_a, state_ret, state_lru_conv, state_lru_h, state_ffn_conv, norm_mix, norm_ffn, norm_final, w_in_ab, conv_a_w, conv_a_b, ln_a_g, ln_a_b, gn_ret_g, w_out_ab, w_in_c, conv_c_w, conv_c_b, w_lru_a, b_lru_a, w_lru_x, b_lru_x, lru_lambda, w_out_c, w_ffn_up, ffn_conv_w, ffn_conv_b, w_ffn_down):
    pe = dict(gmix=_row(norm_mix[0]), w_in=w_in_ab[0].astype(BF16), conv_w=conv_a_w[0],
              conv_wd=_diag_taps(conv_a_w[0]),
              conv_b=_row(conv_a_b[0]), ln_g=_row(ln_a_g[0]), ln_b=_row(ln_a_b[0]),
              gn_g=_row(gn_ret_g[0]), w_out=w_out_ab[0].astype(BF16))
    po = dict(gmix=_row(norm_mix[1]), w_in=w_in_c[0].astype(BF16), conv_w=conv_c_w[0],
              conv_b=_row(conv_c_b[0]), w_ax=_pair_block_diag(w_lru_a[0], w_lru_x[0]),
              b_a=_row(b_lru_a[0]), b_x=_row(b_lru_x[0]), lam=_row(lru_lambda[0]),
              w_out=w_out_c[0].astype(BF16))
    pf = dict(g=norm_ffn[:, None, :], w_up=w_ffn_up.astype(BF16), conv_w=ffn_conv_w,
              conv_b=ffn_conv_b[:, None, :], w_down=w_ffn_down.astype(BF16))
    g_final = _row(norm_final)

    xp, p_conv_a, p_ret = _even_mixer(x_prompt, None, None, pe, **PROMPT_TILES['even'])
    xp, p_ffn0 = _conv_ffn(xp, None, 0, pf, None, **PROMPT_TILES['ffn'])
    xp, p_lru_conv, p_lru_h = _odd_mixer(xp, None, None, po, **PROMPT_TILES['odd'])
    y_prompt, p_ffn1 = _conv_ffn(xp, None, 1, pf, g_final, **PROMPT_TILES['ffn'])

    xs, s_conv_a, s_ret = _even_mixer(x_sample, state_conv_a, state_ret, pe, **SAMPLE_TILES['even'])
    xs, s_ffn0 = _conv_ffn_sample(xs, state_ffn_conv, 0, pf, None)
    xs, s_lru_conv, s_lru_h = _odd_mixer(xs, state_lru_conv, state_lru_h[0][:, None, :], po,
                                         **SAMPLE_TILES['odd'])
    y_sample, s_ffn1 = _conv_ffn_sample(xs, state_ffn_conv, 1, pf, g_final)

    return (y_prompt, y_sample,
            p_conv_a[None], p_ret[None], p_lru_conv[None], p_lru_h[:, 0, :][None],
            jnp.stack([p_ffn0, p_ffn1]),
            s_conv_a[None], s_ret[None], s_lru_conv[None], s_lru_h[:, 0, :][None],
            jnp.stack([s_ffn0, s_ffn1]))
```

```python
import functools
import math
import types

import numpy as np
import jax
import jax.numpy as jnp
from jax import lax
from jax.experimental import pallas as pl
from jax.experimental.pallas import tpu as pltpu

F32 = jnp.float32
BF16 = jnp.bfloat16

D_MODEL = 1024
CONV_A_DIM = 512
CONV_A_WIDTH = 31
CONV_A_HIST = CONV_A_WIDTH - 1
CONV_A_HIST_PAD = 32
CONV_A_ROW_BLOCK = 32
CONV_A_SEQ_BLOCK = 4
LN_EPS = 1e-5
RET_HEADS = 4
RET_HEAD_DIM = 128
RET_DIM = RET_HEADS * RET_HEAD_DIM
RET_CHUNK = 128
ROPE_BASE = 10000.0
GN_EPS = 1e-5
IN_AB_DIM = 2 * CONV_A_DIM + 4 * RET_DIM
Z_SECTION = 512
LRU_DIM = 1024
LRU_BLOCKS = 8
LRU_BLOCK = LRU_DIM // LRU_BLOCKS
LRU_CONV_WIDTH = 4
LRU_HIST = LRU_CONV_WIDTH - 1
LRU_C = 8.0
ODD_SUB_ROWS = 128
FFN_DIM = 2816
FFN_CONV_WIDTH = 3
FFN_HIST = FFN_CONV_WIDTH - 1
FFN_COL_CHUNK = 256
RMS_EPS = 1e-6
PAST_LEN = 16384
SUBLANES = 8
SAMPLE_GROUP = RET_CHUNK // SUBLANES

VMEM_LIMIT_BYTES = 56 * 1024 * 1024


def _rms(x, g):
    return x * lax.rsqrt(jnp.mean(x * x, axis=-1, keepdims=True) + RMS_EPS) * g


def _sigmoid(x):
    return 1.0 / (1.0 + jnp.exp(-x))


def _gelu(x):
    c = math.sqrt(2.0 / math.pi)
    return x * (0.5 + 0.5 * jnp.tanh(x * (c + (c * 0.044715) * (x * x))))


def _mm(a, b):
    return jnp.dot(a, b, preferred_element_type=F32)


def _const_spec(shape, grid_rank=2):
    nd = len(shape)
    assert grid_rank == 2
    return pl.BlockSpec(shape, lambda b, t: (0,) * nd, pipeline_mode=pl.Buffered(1))


def _take(it, names):
    return types.SimpleNamespace(**{n: next(it) for n in names})


EVEN_CONSTS = ('cos', 'sin', 'dmask', 'qd', 'kd', 'cd', 'gmix', 'w_in', 'conv_w', 'conv_wd', 'conv_b',
               'ln_g', 'ln_b', 'gn_g', 'w_out')
EVEN_SCRATCH = ('xp_s', 'xs_s', 'yb_s', 'cv_s', 'ya_s')


def _conv_a_tap(j):
    off = CONV_A_HIST_PAD - CONV_A_HIST + j
    return off % SUBLANES, off - off % SUBLANES


CONV_A_VPU_TAPS = tuple(j for j in range(CONV_A_WIDTH) if _conv_a_tap(j)[0] <= 3)
CONV_A_MXU_TAPS = tuple(j for j in range(CONV_A_WIDTH) if _conv_a_tap(j)[0] > 3)


def _even_scratch(S, TT, sample):
    return [pltpu.VMEM((S, CONV_A_HIST_PAD + TT, CONV_A_DIM), F32),
            pltpu.VMEM((SUBLANES - 1, S, TT + CONV_A_HIST_PAD - SUBLANES, CONV_A_DIM), F32),
            pltpu.VMEM((S * TT, RET_DIM), BF16),
            pltpu.VMEM((S * TT, CONV_A_DIM), F32),
            pltpu.VMEM((S * TT, CONV_A_DIM), BF16)]


def _even_init(r, ret_live, conv_state, ret_state):
    S = r.xp_s.shape[0]
    HP, H = CONV_A_HIST_PAD, CONV_A_HIST
    r.xp_s[:, 0:HP, :] = jnp.zeros((S, HP, CONV_A_DIM), F32)
    if conv_state is not None:
        r.xp_s[:, HP - H:HP, :] = conv_state[...]
        ret_live[...] = ret_state[...]
    else:
        ret_live[...] = jnp.zeros(ret_live.shape, F32)


def _even_tiles(r, x_src, y_dst, ret_live, S, TT, sample):
    HP, H = CONV_A_HIST_PAD, CONV_A_HIST
    if sample:
        tiles = [(s0, SAMPLE_GROUP, 0, TT) for s0 in range(0, S, SAMPLE_GROUP)]
    else:
        tiles = [(0, 1, r0, RET_CHUNK) for r0 in range(0, TT, RET_CHUNK)]
    o0 = 2 * CONV_A_DIM
    scale = RET_HEAD_DIM ** -0.5

    def run(s0, ns, r0, nr):
        rows = ns * nr
        row0 = s0 * TT + r0
        x = x_src[s0:s0 + ns, r0:r0 + nr, :].reshape(rows, D_MODEL)
        h = _rms(x, r.gmix[...]).astype(BF16)
        zs = [_mm(h, r.w_in[:, c0:c0 + Z_SECTION]) for c0 in range(0, IN_AB_DIM, Z_SECTION)]

        def zcols(lo, hi):
            k = lo // Z_SECTION
            return zs[k][:, lo - k * Z_SECTION:hi - k * Z_SECTION]

        u = zcols(0, CONV_A_DIM) * _sigmoid(zcols(CONV_A_DIM, 2 * CONV_A_DIM))
        r.xp_s[s0:s0 + ns, HP + r0:HP + r0 + nr, :] = u.reshape(ns, nr, CONV_A_DIM)

        cos = r.cos[row0:row0 + rows, :]
        sin = r.sin[row0:row0 + rows, :]
        gate = zcols(o0 + 3 * RET_DIM, o0 + 4 * RET_DIM)
        o_parts = []
        for hh in range(RET_HEADS):
            lo = hh * RET_HEAD_DIM
            hi = lo + RET_HEAD_DIM
            qh = zcols(o0 + lo, o0 + hi)
            kh = zcols(o0 + RET_DIM + lo, o0 + RET_DIM + hi)
            qc = qh * cos + pltpu.roll(qh, RET_HEAD_DIM // 2, axis=1) * sin
            kc = (kh * cos + pltpu.roll(kh, RET_HEAD_DIM // 2, axis=1) * sin) * scale
            vc = zcols(o0 + 2 * RET_DIM + lo, o0 + 2 * RET_DIM + hi)
            qb = qc.astype(BF16)
            vb = vc.astype(BF16)
            scores = lax.dot_general(qb, kc.astype(BF16), (((1,), (1,)), ((), ())),
                                     preferred_element_type=F32) * r.dmask[hh]
            inner = _mm(scores.astype(BF16), vb)
            if not sample:
                kdv = (kc * r.kd[hh]).astype(BF16)
                st = ret_live[0, hh]
                cross = _mm(qb, st.astype(BF16))
                upd = lax.dot_general(kdv, vb, (((0,), (0,)), ((), ())), preferred_element_type=F32)
                ret_live[0, hh] = st * r.cd[hh] + upd
            else:
                parts = []
                for sq in range(SAMPLE_GROUP):
                    sidx = s0 + sq
                    rs = slice(sq * SUBLANES, (sq + 1) * SUBLANES)
                    st = ret_live[sidx, hh]
                    parts.append(_mm(qc[rs].astype(BF16), st.astype(BF16)))
                    kdv_s = (kc[rs] * r.kd[hh, rs, :]).astype(BF16)
                    upd = lax.dot_general(kdv_s, vc[rs].astype(BF16), (((0,), (0,)), ((), ())),
                                          preferred_element_type=F32)
                    ret_live[sidx, hh] = st * r.cd[hh] + upd
                cross = jnp.concatenate(parts, axis=0)
            o = inner + cross * r.qd[hh]
            mu_o = jnp.mean(o, axis=-1, keepdims=True)
            oc = o - mu_o
            var_o = jnp.mean(oc * oc, axis=-1, keepdims=True)
            o_parts.append(oc * lax.rsqrt(var_o + GN_EPS) * r.gn_g[:, lo:hi])

        yb = jnp.concatenate(o_parts, axis=-1) * (gate * _sigmoid(gate))
        r.yb_s[row0:row0 + rows, :] = yb.astype(BF16)

    def conv_and_project():
        R = S * TT
        L = TT + HP - SUBLANES
        for b in range(1, SUBLANES):
            r.xs_s[b - 1] = r.xp_s[:, b:b + L, :]
        half = CONV_A_DIM // 2
        if sample:
            blocks = [(sb, CONV_A_SEQ_BLOCK, 0, TT) for sb in range(0, S, CONV_A_SEQ_BLOCK)]
        else:
            blocks = [(0, 1, rb, CONV_A_ROW_BLOCK) for rb in range(0, TT, CONV_A_ROW_BLOCK)]
        for sb, nsb, rb, nrb in blocks:
            rowb = sb * TT + rb
            acc = jnp.zeros((nsb, nrb, CONV_A_DIM), F32) + r.conv_b[...]
            for j in CONV_A_VPU_TAPS:
                b, lo = _conv_a_tap(j)
                if b == 0:
                    win = r.xp_s[sb:sb + nsb, lo + rb:lo + rb + nrb, :]
                else:
                    win = r.xs_s[b - 1, sb:sb + nsb, lo + rb:lo + rb + nrb, :]
                acc = acc + r.conv_w[j:j + 1, :] * win
            r.cv_s[rowb:rowb + nsb * nrb, :] = acc.reshape(nsb * nrb, CONV_A_DIM)
        mxu_parts = []
        for p in range(2):
            cs = slice(p * half, (p + 1) * half)
            acc = jnp.zeros((R, half), F32)
            for k, j in enumerate(CONV_A_MXU_TAPS):
                b, lo = _conv_a_tap(j)
                win = r.xs_s[b - 1, :, lo:lo + TT, cs]
                acc = acc + _mm(win.reshape(R, half).astype(BF16), r.conv_wd[k, p])
            mxu_parts.append(acc)
        cv_mxu = jnp.concatenate(mxu_parts, axis=-1)
        for sb, nsb, rb, nrb in blocks:
            rowb = sb * TT + rb
            rows_b = slice(rowb, rowb + nsb * nrb)
            cv = r.cv_s[rows_b, :] + cv_mxu[rows_b, :]
            mu = jnp.mean(cv, axis=-1, keepdims=True)
            cvc = cv - mu
            var = jnp.mean(cvc * cvc, axis=-1, keepdims=True)
            ln = cvc * lax.rsqrt(var + LN_EPS) * r.ln_g[...] + r.ln_b[...]
            r.ya_s[rowb:rowb + nsb * nrb, :] = (ln * _sigmoid(ln)).astype(BF16)
        y = (_mm(r.ya_s[...], r.w_out[0:CONV_A_DIM, :])
             + _mm(r.yb_s[...], r.w_out[CONV_A_DIM:CONV_A_DIM + RET_DIM, :])
             + x_src[...].reshape(R, D_MODEL))
        y_dst[...] = y.reshape(S, TT, D_MODEL)

    return [functools.partial(run, *tl) for tl in tiles] + [conv_and_project]


def _even_carry(r, TT):
    HP = CONV_A_HIST_PAD
    r.xp_s[:, 0:HP, :] = r.xp_s[:, TT:TT + HP, :]


def _even_conv_state(r):
    return r.xp_s[:, CONV_A_HIST_PAD - CONV_A_HIST:CONV_A_HIST_PAD, :]


def _rope_tables(pos0, T, reps):
    d = RET_HEAD_DIM
    inv_freq = ROPE_BASE ** (-np.arange(0, d, 2, dtype=np.float64) / d)
    ang = (pos0 + np.arange(T, dtype=np.float64))[:, None] * inv_freq[None, :]
    cos = np.cos(ang)
    sin = np.sin(ang)
    cos2 = np.concatenate([cos, cos], axis=-1)
    sin2 = np.concatenate([-sin, sin], axis=-1)
    return (jnp.asarray(np.tile(cos2, (reps, 1)), F32), jnp.asarray(np.tile(sin2, (reps, 1)), F32))


def _decay_tables(c, groups):
    nh = RET_HEADS
    log_gamma = np.log(1.0 - 2.0 ** (-5.0 - np.arange(nh, dtype=np.float64)))
    idx = np.arange(c, dtype=np.float64)
    rel = idx[:, None] - idx[None, :]
    dmask = np.where(rel >= 0, np.exp(np.maximum(rel, 0.0)[None] * log_gamma[:, None, None]), 0.0)
    qd = np.exp((idx + 1.0)[None, :] * log_gamma[:, None])
    kd = np.exp((c - 1.0 - idx)[None, :] * log_gamma[:, None])
    cd = np.exp(c * log_gamma)
    if groups > 1:
        eye = np.eye(groups)
        dmask = np.einsum('gk,hij->hgikj', eye, dmask).reshape(nh, groups * c, groups * c)
        qd = np.tile(qd, (1, groups))
        kd = np.tile(kd, (1, groups))
    n = groups * c
    qd = np.broadcast_to(qd[:, :, None], (nh, n, RET_HEAD_DIM))
    kd = np.broadcast_to(kd[:, :, None], (nh, n, RET_HEAD_DIM))
    cd = np.broadcast_to(cd[:, None, None], (nh, 1, RET_HEAD_DIM))
    return tuple(jnp.asarray(np.ascontiguousarray(a), F32) for a in (dmask, qd, kd, cd))


def _even_const_args(p, pos0, T, reps, sample):
    cos, sin = _rope_tables(pos0, T, reps)
    dmask, qd, kd, cd = _decay_tables(T, SAMPLE_GROUP) if sample else _decay_tables(RET_CHUNK, 1)
    return [cos, sin, dmask, qd, kd, cd, p['gmix'], p['w_in'], p['conv_w'], p['conv_wd'], p['conv_b'],
            p['ln_g'], p['ln_b'], p['gn_g'], p['w_out']]


def _even_kernel(*refs, S, TT, nt, sample):
    it = iter(refs)
    x_ref = next(it)
    r = _take(it, EVEN_CONSTS)
    convst_ref = next(it) if sample else None
    retst_ref = next(it) if sample else None
    y_ref = next(it); convout_ref = next(it); retout_ref = next(it)
    r.__dict__.update(_take(it, EVEN_SCRATCH).__dict__)
    t = pl.program_id(1)

    @pl.when(t == 0)
    def _():
        _even_init(r, retout_ref, convst_ref, retst_ref)

    for tile in _even_tiles(r, x_ref, y_ref, retout_ref, S, TT, sample):
        tile()
    _even_carry(r, TT)

    @pl.when(t == nt - 1)
    def _():
        convout_ref[...] = _even_conv_state(r)


def _even_mixer(x, conv_state, ret_state, p, *, S, TT):
    B, T, D = x.shape
    sample = conv_state is not None
    nb, nt = B // S, T // TT
    R = S * TT
    if sample:
        assert TT == T == SUBLANES and S % SAMPLE_GROUP == 0
        consts = _even_const_args(p, PAST_LEN, T, S, True)
    else:
        assert S == 1 and TT % RET_CHUNK == 0
        consts = _even_const_args(p, 0, T, 1, False)
    in_specs = [pl.BlockSpec((S, TT, D), lambda b, t: (b, t, 0)),
                pl.BlockSpec((R, RET_HEAD_DIM), lambda b, t: (t, 0)),
                pl.BlockSpec((R, RET_HEAD_DIM), lambda b, t: (t, 0))]
    in_specs += [_const_spec(c.shape, 2) for c in consts[2:]]
    args = [x] + consts
    if sample:
        in_specs += [pl.BlockSpec((None, S, CONV_A_HIST, CONV_A_DIM), lambda b, t: (0, b, 0, 0)),
                     pl.BlockSpec((None, S, RET_HEADS, RET_HEAD_DIM, RET_HEAD_DIM),
                                  lambda b, t: (0, b, 0, 0, 0))]
        args += [conv_state, ret_state]
    out_shape = (jax.ShapeDtypeStruct((B, T, D), F32),
                 jax.ShapeDtypeStruct((B, CONV_A_HIST, CONV_A_DIM), F32),
                 jax.ShapeDtypeStruct((B, RET_HEADS, RET_HEAD_DIM, RET_HEAD_DIM), F32))
    out_specs = (pl.BlockSpec((S, TT, D), lambda b, t: (b, t, 0)),
                 pl.BlockSpec((S, CONV_A_HIST, CONV_A_DIM), lambda b, t: (b, 0, 0)),
                 pl.BlockSpec((S, RET_HEADS, RET_HEAD_DIM, RET_HEAD_DIM), lambda b, t: (b, 0, 0, 0)))
    return pl.pallas_call(
        functools.partial(_even_kernel, S=S, TT=TT, nt=nt, sample=sample),
        grid=(nb, nt), in_specs=in_specs, out_specs=out_specs, out_shape=out_shape,
        scratch_shapes=_even_scratch(S, TT, sample),
        compiler_params=pltpu.CompilerParams(dimension_semantics=("arbitrary", "arbitrary"),
                                             vmem_limit_bytes=VMEM_LIMIT_BYTES),
        name="even_mixer_sample" if sample else "even_mixer_prompt",
    )(*args)


ODD_CONSTS = ('gmix', 'w_in', 'conv_w', 'conv_b', 'w_ax', 'b_a', 'b_x', 'lam', 'w_out')
ODD_SCRATCH = ('xp_s', 'gate_s', 'xc_s', 'r_s', 'i_s', 'act_s')


def _odd_scratch(S, TT):
    R = S * TT
    return ([pltpu.VMEM((S, SUBLANES + TT, LRU_DIM), F32)]
            + [pltpu.VMEM((R, LRU_DIM), F32) for _ in range(4)]
            + [pltpu.VMEM((R, LRU_DIM), BF16)])


def _odd_init(r, h_live, conv_state, h_state):
    S = r.xp_s.shape[0]
    HP, H = SUBLANES, LRU_HIST
    r.xp_s[:, 0:HP, :] = jnp.zeros((S, HP, LRU_DIM), F32)
    if conv_state is not None:
        r.xp_s[:, HP - H:HP, :] = conv_state[...]
        h_live[...] = h_state[...]
    else:
        h_live[...] = jnp.zeros(h_live.shape, F32)


def _odd_tiles(r, x_src, y_dst, h_live, S, TT, sample):
    HP, H = SUBLANES, LRU_HIST
    if sample:
        tiles = [(0, S, 0, TT)]
    else:
        tiles = [(0, 1, r0, min(TT, ODD_SUB_ROWS)) for r0 in range(0, TT, ODD_SUB_ROWS)]
    pair = 2 * LRU_BLOCK

    def scan_group(rows, h_prev, decay, sub):
        rg = _sigmoid(r.r_s[rows, :])
        ig = _sigmoid(r.i_s[rows, :])
        a = jnp.exp2(decay * rg)
        om = jnp.maximum(1.0 - a * a, 0.0)
        root = jnp.where(om > 0.0, om * lax.rsqrt(om), 0.0)
        b = root * (ig * r.xc_s[rows, :])
        for sh in (1, 2, 4):
            keep = sub >= sh
            a_sh = jnp.where(keep, pltpu.roll(a, sh, axis=0), 1.0)
            b_sh = jnp.where(keep, pltpu.roll(b, sh, axis=0), 0.0)
            b = a * b_sh + b
            a = a * a_sh
        return a * h_prev + b

    def run(s0, ns, r0, nr):
        rows = ns * nr
        row0 = s0 * TT + r0
        nlam = -r.lam[...]
        softplus = jnp.maximum(nlam, 0.0) + jnp.log(1.0 + jnp.exp(-jnp.abs(nlam)))
        decay = jnp.broadcast_to((-LRU_C * math.log2(math.e)) * softplus, (SUBLANES, LRU_DIM))
        sub = lax.broadcasted_iota(jnp.int32, (SUBLANES, LRU_DIM), 0)
        x = x_src[s0:s0 + ns, r0:r0 + nr, :].reshape(rows, D_MODEL)
        h = _rms(x, r.gmix[...]).astype(BF16)
        r.gate_s[row0:row0 + rows, :] = _mm(h, r.w_in[:, 0:LRU_DIM])
        rec3 = _mm(h, r.w_in[:, LRU_DIM:2 * LRU_DIM]).reshape(ns, nr, LRU_DIM)
        r.xp_s[s0:s0 + ns, HP + r0:HP + r0 + nr, :] = rec3
        acc = r.conv_w[H:H + 1, :] * rec3 + r.conv_b[...]
        for j in range(H):
            lo = HP - H + j + r0
            acc = acc + r.conv_w[j:j + 1, :] * r.xp_s[s0:s0 + ns, lo:lo + nr, :]
        xc = acc.reshape(rows, LRU_DIM)
        r.xc_s[row0:row0 + rows, :] = xc
        xcb = xc.astype(BF16)
        for p in range(LRU_BLOCKS // 2):
            cs = slice(pair * p, pair * (p + 1))
            ri = _mm(xcb[:, cs], r.w_ax[p])
            r.r_s[row0:row0 + rows, cs] = ri[:, 0:pair] + r.b_a[:, cs]
            r.i_s[row0:row0 + rows, cs] = ri[:, pair:2 * pair] + r.b_x[:, cs]
        h_prev = None if sample else jnp.broadcast_to(h_live[0], (SUBLANES, LRU_DIM))
        for g0 in range(row0, row0 + rows, 2 * SUBLANES):
            parts = []
            for k in range(2):
                ga = g0 + k * SUBLANES
                if sample:
                    h_prev = jnp.broadcast_to(h_live[ga // TT], (SUBLANES, LRU_DIM))
                hs = scan_group(slice(ga, ga + SUBLANES), h_prev, decay, sub)
                h_last = hs[SUBLANES - 1:SUBLANES, :]
                if sample:
                    h_live[ga // TT] = h_last
                h_prev = jnp.broadcast_to(h_last, (SUBLANES, LRU_DIM))
                parts.append(hs)
            rows2 = slice(g0, g0 + 2 * SUBLANES)
            r.act_s[rows2, :] = (jnp.concatenate(parts, axis=0) * _gelu(r.gate_s[rows2, :])).astype(BF16)
        if not sample:
            h_live[0] = h_prev[0:1, :]
        y = _mm(r.act_s[row0:row0 + rows, :], r.w_out[...]) + x
        y_dst[s0:s0 + ns, r0:r0 + nr, :] = y.reshape(ns, nr, D_MODEL)

    return [functools.partial(run, *tl) for tl in tiles]


def _odd_carry(r, TT):
    HP = SUBLANES
    r.xp_s[:, 0:HP, :] = r.xp_s[:, TT:TT + HP, :]


def _odd_conv_state(r):
    return r.xp_s[:, SUBLANES - LRU_HIST:SUBLANES, :]


def _odd_const_args(p):
    return [p[n] for n in ODD_CONSTS]


def _odd_kernel(*refs, S, TT, nt, sample):
    it = iter(refs)
    x_ref = next(it)
    r = _take(it, ODD_CONSTS)
    convst_ref = next(it) if sample else None
    hst_ref = next(it) if sample else None
    y_ref = next(it); convout_ref = next(it); hout_ref = next(it)
    r.__dict__.update(_take(it, ODD_SCRATCH).__dict__)
    t = pl.program_id(1)

    @pl.when(t == 0)
    def _():
        _odd_init(r, hout_ref, convst_ref, hst_ref)

    for tile in _odd_tiles(r, x_ref, y_ref, hout_ref, S, TT, sample):
        tile()
    _odd_carry(r, TT)

    @pl.when(t == nt - 1)
    def _():
        convout_ref[...] = _odd_conv_state(r)


def _odd_mixer(x, conv_state, h_state, p, *, S, TT):
    B, T, D = x.shape
    sample = conv_state is not None
    nb, nt = B // S, T // TT
    consts = _odd_const_args(p)
    in_specs = [pl.BlockSpec((S, TT, D), lambda b, t: (b, t, 0))]
    in_specs += [_const_spec(c.shape, 2) for c in consts]
    args = [x] + consts
    if sample:
        in_specs += [pl.BlockSpec((None, S, LRU_HIST, LRU_DIM), lambda b, t: (0, b, 0, 0)),
                     pl.BlockSpec((S, 1, LRU_DIM), lambda b, t: (b, 0, 0))]
        args += [conv_state, h_state]
    out_shape = (jax.ShapeDtypeStruct((B, T, D), F32),
                 jax.ShapeDtypeStruct((B, LRU_HIST, LRU_DIM), F32),
                 jax.ShapeDtypeStruct((B, 1, LRU_DIM), F32))
    out_specs = (pl.BlockSpec((S, TT, D), lambda b, t: (b, t, 0)),
                 pl.BlockSpec((S, LRU_HIST, LRU_DIM), lambda b, t: (b, 0, 0)),
                 pl.BlockSpec((S, 1, LRU_DIM), lambda b, t: (b, 0, 0)))
    return pl.pallas_call(
        functools.partial(_odd_kernel, S=S, TT=TT, nt=nt, sample=sample),
        grid=(nb, nt), in_specs=in_specs, out_specs=out_specs, out_shape=out_shape,
        scratch_shapes=_odd_scratch(S, TT),
        compiler_params=pltpu.CompilerParams(dimension_semantics=("arbitrary", "arbitrary"),
                                             vmem_limit_bytes=VMEM_LIMIT_BYTES),
        name="odd_mixer_sample" if sample else "odd_mixer_prompt",
    )(*args)


FFN_CONSTS = ('g', 'w_up', 'conv_w', 'conv_b', 'w_down')
FFN_SCRATCH = ('h_s', 'act_s', 'hist_s', 'work_s')


def _ffn_scratch(S, TT):
    R = S * TT
    return [pltpu.VMEM((R, D_MODEL), BF16),
            pltpu.VMEM((R, FFN_DIM), BF16),
            pltpu.VMEM((S, SUBLANES, 2 * FFN_DIM), F32),
            pltpu.VMEM((S, SUBLANES + TT, FFN_COL_CHUNK), F32)]


def _ffn_init(r, state):
    r.hist_s[...] = jnp.zeros(r.hist_s.shape, F32)
    if state is not None:
        r.hist_s[:, SUBLANES - FFN_HIST:SUBLANES, :] = state[...]


def _ffn_stages(r, x_src, y_dst, g_final, S, TT):
    R = S * TT
    HP, H, CK = SUBLANES, FFN_HIST, FFN_COL_CHUNK

    def prologue():
        x = x_src[...].reshape(R, D_MODEL)
        r.h_s[...] = _rms(x, r.g[...]).astype(BF16)

    def conv_cols(col):
        z3 = _mm(r.h_s[...], r.w_up[:, col:col + CK]).reshape(S, TT, CK)
        r.work_s[:, 0:HP, :] = r.hist_s[:, :, col:col + CK]
        r.work_s[:, HP:HP + TT, :] = z3
        zc = r.conv_w[H:H + 1, col:col + CK] * z3 + r.conv_b[:, col:col + CK]
        for j in range(H):
            zc = zc + r.conv_w[j:j + 1, col:col + CK] * r.work_s[:, HP - H + j:HP - H + j + TT, :]
        r.hist_s[:, :, col:col + CK] = r.work_s[:, TT:TT + HP, :]
        return zc.reshape(R, CK)

    def chunk(c):
        gz = conv_cols(c * CK)
        uz = conv_cols(FFN_DIM + c * CK)
        r.act_s[:, c * CK:(c + 1) * CK] = (_gelu(gz) * uz).astype(BF16)

    def epilogue():
        y = _mm(r.act_s[...], r.w_down[...]) + x_src[...].reshape(R, D_MODEL)
        if g_final is not None:
            y = _rms(y, g_final[...])
        y_dst[...] = y.reshape(S, TT, D_MODEL)

    return prologue, [functools.partial(chunk, c) for c in range(FFN_DIM // CK)], epilogue


def _ffn_state(r):
    return r.hist_s[:, SUBLANES - FFN_HIST:SUBLANES, :]


def _ffn_const_args(p, g_final):
    return [p[n] for n in FFN_CONSTS] + ([g_final] if g_final is not None else [])


def _ffn_kernel(*refs, S, TT, nt, sample, final):
    it = iter(refs)
    x_ref = next(it)
    r = _take(it, FFN_CONSTS)
    gfin_ref = next(it) if final else None
    st_ref = next(it) if sample else None
    y_ref = next(it); stout_ref = next(it)
    r.__dict__.update(_take(it, FFN_SCRATCH).__dict__)
    t = pl.program_id(1)

    @pl.when(t == 0)
    def _():
        _ffn_init(r, st_ref)

    prologue, chunks, epilogue = _ffn_stages(r, x_ref, y_ref, gfin_ref, S, TT)
    prologue()
    for ch in chunks:
        ch()
    epilogue()

    @pl.when(t == nt - 1)
    def _():
        stout_ref[...] = _ffn_state(r)


def _layer_spec(shape, layer):
    nd = len(shape) - 1
    return pl.BlockSpec((None,) + tuple(shape[1:]), lambda b, t: (layer,) + (0,) * nd,
                        pipeline_mode=pl.Buffered(1))


def _conv_ffn(x, state, layer, p, g_final, *, S, TT):
    B, T, D = x.shape
    sample = state is not None
    final = g_final is not None
    nb, nt = B // S, T // TT
    consts = [p[n] for n in FFN_CONSTS]
    in_specs = [pl.BlockSpec((S, TT, D), lambda b, t: (b, t, 0))]
    in_specs += [_layer_spec(c.shape, layer) for c in consts]
    args = [x] + consts
    if final:
        in_specs.append(_const_spec(g_final.shape, 2))
        args.append(g_final)
    if sample:
        in_specs.append(pl.BlockSpec((None, S, FFN_HIST, 2 * FFN_DIM), lambda b, t: (layer, b, 0, 0)))
        args.append(state)
    out_shape = (jax.ShapeDtypeStruct((B, T, D), F32),
                 jax.ShapeDtypeStruct((B, FFN_HIST, 2 * FFN_DIM), F32))
    out_specs = (pl.BlockSpec((S, TT, D), lambda b, t: (b, t, 0)),
                 pl.BlockSpec((S, FFN_HIST, 2 * FFN_DIM), lambda b, t: (b, 0, 0)))
    return pl.pallas_call(
        functools.partial(_ffn_kernel, S=S, TT=TT, nt=nt, sample=sample, final=final),
        grid=(nb, nt), in_specs=in_specs, out_specs=out_specs, out_shape=out_shape,
        scratch_shapes=_ffn_scratch(S, TT),
        compiler_params=pltpu.CompilerParams(dimension_semantics=("arbitrary", "arbitrary"),
                                             vmem_limit_bytes=VMEM_LIMIT_BYTES),
        name=("ffn_sample" if sample else "ffn_prompt") + ("_final" if final else ""),
    )(*args)


def _ffn_cols_kernel(*refs, B, TT, nc, final):
    it = iter(refs)
    x_ref = next(it); g_ref = next(it)
    wg_ref = next(it); wu_ref = next(it)
    cwg_ref = next(it); cwu_ref = next(it); cbg_ref = next(it); cbu_ref = next(it)
    wdn_ref = next(it)
    gfin_ref = next(it) if final else None
    stg_ref = next(it); stu_ref = next(it)
    y_ref = next(it); stgout_ref = next(it); stuout_ref = next(it)
    h_s = next(it); act_s = next(it)

    c = pl.program_id(0)
    R = B * TT
    CK = FFN_COL_CHUNK
    H = FFN_HIST

    @pl.when(c == 0)
    def _():
        x = x_ref[...].reshape(R, D_MODEL)
        h_s[...] = _rms(x, g_ref[...]).astype(BF16)

    tpos = lax.broadcasted_iota(jnp.int32, (B, TT, CK), 1)

    def conv(w_ref, cw_ref, cb_ref, st_ref, stout_ref):
        z3 = _mm(h_s[...], w_ref[...]).reshape(B, TT, CK)
        stout_ref[...] = z3[:, TT - H:TT, :]
        prev1 = st_ref[:, 1:2, :]
        prev2 = st_ref[:, 0:1, :]
        zm1 = jnp.where(tpos >= 1, pltpu.roll(z3, 1, axis=1), prev1)
        zm2 = jnp.where(tpos >= 2, pltpu.roll(z3, 2, axis=1), jnp.where(tpos == 1, prev1, prev2))
        zc = cw_ref[2:3, :] * z3 + cw_ref[1:2, :] * zm1 + cw_ref[0:1, :] * zm2 + cb_ref[...]
        return zc.reshape(R, CK)

    gz = conv(wg_ref, cwg_ref, cbg_ref, stg_ref, stgout_ref)
    uz = conv(wu_ref, cwu_ref, cbu_ref, stu_ref, stuout_ref)
    act_s[c] = (_gelu(gz) * uz).astype(BF16)

    @pl.when(c == nc - 1)
    def _():
        act = jnp.concatenate([act_s[k] for k in range(nc)], axis=-1)
        y = _mm(act, wdn_ref[...]) + x_ref[...].reshape(R, D_MODEL)
        if final:
            y = _rms(y, gfin_ref[...])
        y_ref[...] = y.reshape(B, TT, D_MODEL)


def _conv_ffn_sample(x, state, layer, p, g_final):
    B, T, D = x.shape
    assert T == SUBLANES and FFN_CONV_WIDTH == 3
    final = g_final is not None
    CK = FFN_COL_CHUNK
    nc = FFN_DIM // CK

    def cols(shape, off):
        nd = len(shape) - 2
        return pl.BlockSpec((None,) + tuple(shape[1:-1]) + (CK,), lambda c: (layer,) + (0,) * nd + (off + c,))

    in_specs = [pl.BlockSpec((B, T, D), lambda c: (0, 0, 0), pipeline_mode=pl.Buffered(1)),
                pl.BlockSpec((None, 1, D), lambda c: (layer, 0, 0), pipeline_mode=pl.Buffered(1)),
                cols(p['w_up'].shape, 0), cols(p['w_up'].shape, nc),
                cols(p['conv_w'].shape, 0), cols(p['conv_w'].shape, nc),
                cols(p['conv_b'].shape, 0), cols(p['conv_b'].shape, nc),
                pl.BlockSpec((None, FFN_DIM, D), lambda c: (layer, 0, 0), pipeline_mode=pl.Buffered(1))]
    args = [x, p['g'], p['w_up'], p['w_up'], p['conv_w'], p['conv_w'], p['conv_b'], p['conv_b'], p['w_down']]
    if final:
        in_specs.append(pl.BlockSpec(g_final.shape, lambda c: (0, 0), pipeline_mode=pl.Buffered(1)))
        args.append(g_final)
    in_specs += [cols(state.shape, 0), cols(state.shape, nc)]
    args += [state, state]
    out_shape = (jax.ShapeDtypeStruct((B, T, D), F32),
                 jax.ShapeDtypeStruct((B, FFN_HIST, FFN_DIM), F32),
                 jax.ShapeDtypeStruct((B, FFN_HIST, FFN_DIM), F32))
    out_specs = (pl.BlockSpec((B, T, D), lambda c: (0, 0, 0)),
                 pl.BlockSpec((B, FFN_HIST, CK), lambda c: (0, 0, c)),
                 pl.BlockSpec((B, FFN_HIST, CK), lambda c: (0, 0, c)))
    y, st_g, st_u = pl.pallas_call(
        functools.partial(_ffn_cols_kernel, B=B, TT=T, nc=nc, final=final),
        grid=(nc,), in_specs=in_specs, out_specs=out_specs, out_shape=out_shape,
        scratch_shapes=[pltpu.VMEM((B * T, D), BF16), pltpu.VMEM((nc, B * T, CK), BF16)],
        compiler_params=pltpu.CompilerParams(dimension_semantics=("arbitrary",),
                                             vmem_limit_bytes=VMEM_LIMIT_BYTES),
        name="ffn_sample_final" if final else "ffn_sample",
    )(*args)
    return y, jnp.concatenate([st_g, st_u], axis=-1)


PROMPT_TILES = dict(even=dict(S=1, TT=512), odd=dict(S=1, TT=1024), ffn=dict(S=1, TT=1024))
SAMPLE_TILES = dict(even=dict(S=16, TT=8), odd=dict(S=32, TT=8))


def _row(v):
    return v.reshape(1, -1)


def _diag_taps(w):
    half = CONV_A_DIM // 2
    w = w[np.array(CONV_A_MXU_TAPS)].reshape(len(CONV_A_MXU_TAPS), 2, 1, half)
    return (jnp.eye(half, dtype=F32)[None, None] * w).astype(BF16)


def _pair_block_diag(w_a, w_x):
    def pairs(w):
        w = w.reshape(LRU_BLOCKS // 2, 2, LRU_BLOCK, LRU_BLOCK)
        z = jnp.zeros_like(w[:, 0])
        top = jnp.concatenate([w[:, 0], z], axis=-1)
        bot = jnp.concatenate([z, w[:, 1]], axis=-1)
        return jnp.concatenate([top, bot], axis=-2)
    return jnp.concatenate([pairs(w_a), pairs(w_x)], axis=-1).astype(BF16)


def kernel(x_prompt, x_sample, state_conv_a, state_ret, state_lru_conv, state_lru_h, state_ffn_conv, norm_mix, norm_ffn, norm_final, w_in_ab, conv_a_w, conv_a_b, ln_a_g, ln_a_b, gn_ret_g, w_out_ab, w_in_c, conv_c_w, conv_c_b, w_lru_a, b_lru_a, w_lru_x, b_lru_x, lru_lambda, w_out_c, w_ffn_up, ffn_conv_w, ffn_conv_b, w_ffn_down):
    pe = dict(gmix=_row(norm_mix[0]), w_in=w_in_ab[0].astype(BF16), conv_w=conv_a_w[0],
              conv_wd=_diag_taps(conv_a_w[0]),
              conv_b=_row(conv_a_b[0]), ln_g=_row(ln_a_g[0]), ln_b=_row(ln_a_b[0]),
              gn_g=_row(gn_ret_g[0]), w_out=w_out_ab[0].astype(BF16))
    po = dict(gmix=_row(norm_mix[1]), w_in=w_in_c[0].astype(BF16), conv_w=conv_c_w[0],
              conv_b=_row(conv_c_b[0]), w_ax=_pair_block_diag(w_lru_a[0], w_lru_x[0]),
              b_a=_row(b_lru_a[0]), b_x=_row(b_lru_x[0]), lam=_row(lru_lambda[0]),
              w_out=w_out_c[0].astype(BF16))
    pf = dict(g=norm_ffn[:, None, :], w_up=w_ffn_up.astype(BF16), conv_w=ffn_conv_w,
              conv_b=ffn_conv_b[:, None, :], w_down=w_ffn_down.astype(BF16))
    g_final = _row(norm_final)

    xp, p_conv_a, p_ret = _even_mixer(x_prompt, None, None, pe, **PROMPT_TILES['even'])
    xp, p_ffn0 = _conv_ffn(xp, None, 0, pf, None, **PROMPT_TILES['ffn'])
    xp, p_lru_conv, p_lru_h = _odd_mixer(xp, None, None, po, **PROMPT_TILES['odd'])
    y_prompt, p_ffn1 = _conv_ffn(xp, None, 1, pf, g_final, **PROMPT_TILES['ffn'])

    xs, s_conv_a, s_ret = _even_mixer(x_sample, state_conv_a, state_ret, pe, **SAMPLE_TILES['even'])
    xs, s_ffn0 = _conv_ffn_sample(xs, state_ffn_conv, 0, pf, None)
    xs, s_lru_conv, s_lru_h = _odd_mixer(xs, state_lru_conv, state_lru_h[0][:, None, :], po,
                                         **SAMPLE_TILES['odd'])
    y_sample, s_ffn1 = _conv_ffn_sample(xs, state_ffn_conv, 1, pf, g_final)

    return (y_prompt, y_sample,
            p_conv_a[None], p_ret[None], p_lru_conv[None], p_lru_h[:, 0, :][None],
            jnp.stack([p_ffn0, p_ffn1]),
            s_conv_a[None], s_ret[None], s_lru_conv[None], s_lru_h[:, 0, :][None],
            jnp.stack([s_ffn0, s_ffn1]))
```

```python
import functools
import math
import types

import numpy as np
import jax
import jax.numpy as jnp
from jax import lax
from jax.experimental import pallas as pl
from jax.experimental.pallas import tpu as pltpu

F32 = jnp.float32
BF16 = jnp.bfloat16

D_MODEL = 1024
CONV_A_DIM = 512
CONV_A_WIDTH = 31
CONV_A_HIST = CONV_A_WIDTH - 1
CONV_A_HIST_PAD = 32
CONV_A_ROW_BLOCK = 32
CONV_A_SEQ_BLOCK = 4
LN_EPS = 1e-5
RET_HEADS = 4
RET_HEAD_DIM = 128
RET_DIM = RET_HEADS * RET_HEAD_DIM
RET_CHUNK = 128
ROPE_BASE = 10000.0
GN_EPS = 1e-5
IN_AB_DIM = 2 * CONV_A_DIM + 4 * RET_DIM
Z_SECTION = 512
LRU_DIM = 1024
LRU_BLOCKS = 8
LRU_BLOCK = LRU_DIM // LRU_BLOCKS
LRU_CONV_WIDTH = 4
LRU_HIST = LRU_CONV_WIDTH - 1
LRU_C = 8.0
ODD_SUB_ROWS = 256
FFN_DIM = 2816
FFN_CONV_WIDTH = 3
FFN_HIST = FFN_CONV_WIDTH - 1
FFN_COL_CHUNK = 256
RMS_EPS = 1e-6
PAST_LEN = 16384
SUBLANES = 8
SAMPLE_GROUP = RET_CHUNK // SUBLANES

VMEM_LIMIT_BYTES = 56 * 1024 * 1024


def _rms(x, g):
    return x * lax.rsqrt(jnp.mean(x * x, axis=-1, keepdims=True) + RMS_EPS) * g


def _sigmoid(x):
    return 1.0 / (1.0 + jnp.exp(-x))


def _gelu(x):
    c = math.sqrt(2.0 / math.pi)
    return x * (0.5 + 0.5 * jnp.tanh(x * (c + (c * 0.044715) * (x * x))))


def _mm(a, b):
    return jnp.dot(a, b, preferred_element_type=F32)


def _const_spec(shape):
    nd = len(shape)
    return pl.BlockSpec(shape, lambda b, t: (0,) * nd, pipeline_mode=pl.Buffered(1))


def _take(it, names):
    return types.SimpleNamespace(**{n: next(it) for n in names})


EVEN_CONSTS = ('cos', 'sin', 'dmask', 'qd', 'kd', 'cd', 'gmix', 'w_in', 'conv_w', 'conv_wd', 'conv_b',
               'ln_g', 'ln_b', 'gn_g', 'w_out')
EVEN_SCRATCH = ('xp_s', 'xs_s', 'yb_s', 'cv_s', 'ya_s')


def _conv_a_tap(j):
    off = CONV_A_HIST_PAD - CONV_A_HIST + j
    return off % SUBLANES, off - off % SUBLANES


CONV_A_VPU_TAPS = tuple(j for j in range(CONV_A_WIDTH) if _conv_a_tap(j)[0] <= 4)
CONV_A_MXU_TAPS = tuple(j for j in range(CONV_A_WIDTH) if _conv_a_tap(j)[0] > 4)


def _even_scratch(S, TT, sample):
    return [pltpu.VMEM((S, CONV_A_HIST_PAD + TT, CONV_A_DIM), F32),
            pltpu.VMEM((SUBLANES - 1, S, TT + CONV_A_HIST_PAD - SUBLANES, CONV_A_DIM), F32),
            pltpu.VMEM((S * TT, RET_DIM), BF16),
            pltpu.VMEM((S * TT, CONV_A_DIM), F32),
            pltpu.VMEM((S * TT, CONV_A_DIM), BF16)]


def _even_init(r, ret_live, conv_state, ret_state):
    S = r.xp_s.shape[0]
    HP, H = CONV_A_HIST_PAD, CONV_A_HIST
    r.xp_s[:, 0:HP, :] = jnp.zeros((S, HP, CONV_A_DIM), F32)
    if conv_state is not None:
        r.xp_s[:, HP - H:HP, :] = conv_state[...]
        ret_live[...] = ret_state[...]
    else:
        ret_live[...] = jnp.zeros(ret_live.shape, F32)


def _even_tiles(r, x_src, y_dst, ret_live, S, TT, sample):
    HP, H = CONV_A_HIST_PAD, CONV_A_HIST
    if sample:
        tiles = [(s0, SAMPLE_GROUP, 0, TT) for s0 in range(0, S, SAMPLE_GROUP)]
    else:
        tiles = [(0, 1, r0, RET_CHUNK) for r0 in range(0, TT, RET_CHUNK)]
    o0 = 2 * CONV_A_DIM
    scale = RET_HEAD_DIM ** -0.5

    def run(s0, ns, r0, nr):
        rows = ns * nr
        row0 = s0 * TT + r0
        x = x_src[s0:s0 + ns, r0:r0 + nr, :].reshape(rows, D_MODEL)
        h = _rms(x, r.gmix[...]).astype(BF16)
        zs = [_mm(h, r.w_in[:, c0:c0 + Z_SECTION]) for c0 in range(0, IN_AB_DIM, Z_SECTION)]

        def zcols(lo, hi):
            k = lo // Z_SECTION
            return zs[k][:, lo - k * Z_SECTION:hi - k * Z_SECTION]

        u = zcols(0, CONV_A_DIM) * _sigmoid(zcols(CONV_A_DIM, 2 * CONV_A_DIM))
        r.xp_s[s0:s0 + ns, HP + r0:HP + r0 + nr, :] = u.reshape(ns, nr, CONV_A_DIM)

        cos = r.cos[row0:row0 + rows, :]
        sin = r.sin[row0:row0 + rows, :]
        gate = zcols(o0 + 3 * RET_DIM, o0 + 4 * RET_DIM)
        o_parts = []
        for hh in range(RET_HEADS):
            lo = hh * RET_HEAD_DIM
            hi = lo + RET_HEAD_DIM
            qh = zcols(o0 + lo, o0 + hi)
            kh = zcols(o0 + RET_DIM + lo, o0 + RET_DIM + hi)
            qc = qh * cos + pltpu.roll(qh, RET_HEAD_DIM // 2, axis=1) * sin
            kc = (kh * cos + pltpu.roll(kh, RET_HEAD_DIM // 2, axis=1) * sin) * scale
            vc = zcols(o0 + 2 * RET_DIM + lo, o0 + 2 * RET_DIM + hi)
            qb = qc.astype(BF16)
            vb = vc.astype(BF16)
            scores = lax.dot_general(qb, kc.astype(BF16), (((1,), (1,)), ((), ())),
                                     preferred_element_type=F32) * r.dmask[hh]
            inner = _mm(scores.astype(BF16), vb)
            if not sample:
                kdv = (kc * r.kd[hh]).astype(BF16)
                st = ret_live[0, hh]
                cross = _mm(qb, st.astype(BF16))
                upd = lax.dot_general(kdv, vb, (((0,), (0,)), ((), ())), preferred_element_type=F32)
                ret_live[0, hh] = st * r.cd[hh] + upd
            else:
                parts = []
                for sq in range(SAMPLE_GROUP):
                    sidx = s0 + sq
                    rs = slice(sq * SUBLANES, (sq + 1) * SUBLANES)
                    st = ret_live[sidx, hh]
                    parts.append(_mm(qc[rs].astype(BF16), st.astype(BF16)))
                    kdv_s = (kc[rs] * r.kd[hh, rs, :]).astype(BF16)
                    upd = lax.dot_general(kdv_s, vc[rs].astype(BF16), (((0,), (0,)), ((), ())),
                                          preferred_element_type=F32)
                    ret_live[sidx, hh] = st * r.cd[hh] + upd
                cross = jnp.concatenate(parts, axis=0)
            o = inner + cross * r.qd[hh]
            mu_o = jnp.mean(o, axis=-1, keepdims=True)
            oc = o - mu_o
            var_o = jnp.mean(oc * oc, axis=-1, keepdims=True)
            o_parts.append(oc * lax.rsqrt(var_o + GN_EPS) * r.gn_g[:, lo:hi])

        yb = jnp.concatenate(o_parts, axis=-1) * (gate * _sigmoid(gate))
        r.yb_s[row0:row0 + rows, :] = yb.astype(BF16)

    def conv_and_project():
        R = S * TT
        L = TT + HP - SUBLANES
        for b in range(1, SUBLANES):
            r.xs_s[b - 1] = r.xp_s[:, b:b + L, :]
        half = CONV_A_DIM // 2
        if sample:
            blocks = [(sb, CONV_A_SEQ_BLOCK, 0, TT) for sb in range(0, S, CONV_A_SEQ_BLOCK)]
        else:
            blocks = [(0, 1, rb, CONV_A_ROW_BLOCK) for rb in range(0, TT, CONV_A_ROW_BLOCK)]
        for sb, nsb, rb, nrb in blocks:
            rowb = sb * TT + rb
            acc = jnp.zeros((nsb, nrb, CONV_A_DIM), F32) + r.conv_b[...]
            for j in CONV_A_VPU_TAPS:
                b, lo = _conv_a_tap(j)
                if b == 0:
                    win = r.xp_s[sb:sb + nsb, lo + rb:lo + rb + nrb, :]
                else:
                    win = r.xs_s[b - 1, sb:sb + nsb, lo + rb:lo + rb + nrb, :]
                acc = acc + r.conv_w[j:j + 1, :] * win
            r.cv_s[rowb:rowb + nsb * nrb, :] = acc.reshape(nsb * nrb, CONV_A_DIM)
        mxu_parts = []
        for p in range(2):
            cs = slice(p * half, (p + 1) * half)
            acc = jnp.zeros((R, half), F32)
            for k, j in enumerate(CONV_A_MXU_TAPS):
                b, lo = _conv_a_tap(j)
                win = r.xs_s[b - 1, :, lo:lo + TT, cs]
                acc = acc + _mm(win.reshape(R, half).astype(BF16), r.conv_wd[k, p])
            mxu_parts.append(acc)
        cv_mxu = jnp.concatenate(mxu_parts, axis=-1)
        for sb, nsb, rb, nrb in blocks:
            rowb = sb * TT + rb
            rows_b = slice(rowb, rowb + nsb * nrb)
            cv = r.cv_s[rows_b, :] + cv_mxu[rows_b, :]
            mu = jnp.mean(cv, axis=-1, keepdims=True)
            cvc = cv - mu
            var = jnp.mean(cvc * cvc, axis=-1, keepdims=True)
            ln = cvc * lax.rsqrt(var + LN_EPS) * r.ln_g[...] + r.ln_b[...]
            r.ya_s[rowb:rowb + nsb * nrb, :] = (ln * _sigmoid(ln)).astype(BF16)
        y = (_mm(r.ya_s[...], r.w_out[0:CONV_A_DIM, :])
             + _mm(r.yb_s[...], r.w_out[CONV_A_DIM:CONV_A_DIM + RET_DIM, :])
             + x_src[...].reshape(R, D_MODEL))
        y_dst[...] = y.reshape(S, TT, D_MODEL)

    return [functools.partial(run, *tl) for tl in tiles] + [conv_and_project]


def _even_carry(r, TT):
    HP = CONV_A_HIST_PAD
    r.xp_s[:, 0:HP, :] = r.xp_s[:, TT:TT + HP, :]


def _even_conv_state(r):
    return r.xp_s[:, CONV_A_HIST_PAD - CONV_A_HIST:CONV_A_HIST_PAD, :]


def _rope_tables(pos0, T, reps):
    d = RET_HEAD_DIM
    inv_freq = ROPE_BASE ** (-np.arange(0, d, 2, dtype=np.float64) / d)
    ang = (pos0 + np.arange(T, dtype=np.float64))[:, None] * inv_freq[None, :]
    cos = np.cos(ang)
    sin = np.sin(ang)
    cos2 = np.concatenate([cos, cos], axis=-1)
    sin2 = np.concatenate([-sin, sin], axis=-1)
    return (jnp.asarray(np.tile(cos2, (reps, 1)), F32), jnp.asarray(np.tile(sin2, (reps, 1)), F32))


def _decay_tables(c, groups):
    nh = RET_HEADS
    log_gamma = np.log(1.0 - 2.0 ** (-5.0 - np.arange(nh, dtype=np.float64)))
    idx = np.arange(c, dtype=np.float64)
    rel = idx[:, None] - idx[None, :]
    dmask = np.where(rel >= 0, np.exp(np.maximum(rel, 0.0)[None] * log_gamma[:, None, None]), 0.0)
    qd = np.exp((idx + 1.0)[None, :] * log_gamma[:, None])
    kd = np.exp((c - 1.0 - idx)[None, :] * log_gamma[:, None])
    cd = np.exp(c * log_gamma)
    if groups > 1:
        eye = np.eye(groups)
        dmask = np.einsum('gk,hij->hgikj', eye, dmask).reshape(nh, groups * c, groups * c)
        qd = np.tile(qd, (1, groups))
        kd = np.tile(kd, (1, groups))
    n = groups * c
    qd = np.broadcast_to(qd[:, :, None], (nh, n, RET_HEAD_DIM))
    kd = np.broadcast_to(kd[:, :, None], (nh, n, RET_HEAD_DIM))
    cd = np.broadcast_to(cd[:, None, None], (nh, 1, RET_HEAD_DIM))
    return tuple(jnp.asarray(np.ascontiguousarray(a), F32) for a in (dmask, qd, kd, cd))


def _even_const_args(p, pos0, T, reps, sample):
    cos, sin = _rope_tables(pos0, T, reps)
    dmask, qd, kd, cd = _decay_tables(T, SAMPLE_GROUP) if sample else _decay_tables(RET_CHUNK, 1)
    return [cos, sin, dmask, qd, kd, cd, p['gmix'], p['w_in'], p['conv_w'], p['conv_wd'], p['conv_b'],
            p['ln_g'], p['ln_b'], p['gn_g'], p['w_out']]


def _even_kernel(*refs, S, TT, nt, sample):
    it = iter(refs)
    x_ref = next(it)
    r = _take(it, EVEN_CONSTS)
    convst_ref = next(it) if sample else None
    retst_ref = next(it) if sample else None
    y_ref = next(it); convout_ref = next(it); retout_ref = next(it)
    r.__dict__.update(_take(it, EVEN_SCRATCH).__dict__)
    t = pl.program_id(1)

    @pl.when(t == 0)
    def _():
        _even_init(r, retout_ref, convst_ref, retst_ref)

    for tile in _even_tiles(r, x_ref, y_ref, retout_ref, S, TT, sample):
        tile()
    _even_carry(r, TT)

    @pl.when(t == nt - 1)
    def _():
        convout_ref[...] = _even_conv_state(r)


def _even_mixer(x, conv_state, ret_state, p, *, S, TT):
    B, T, D = x.shape
    sample = conv_state is not None
    nb, nt = B // S, T // TT
    R = S * TT
    if sample:
        assert TT == T == SUBLANES and S % SAMPLE_GROUP == 0
        consts = _even_const_args(p, PAST_LEN, T, S, True)
    else:
        assert S == 1 and TT % RET_CHUNK == 0
        consts = _even_const_args(p, 0, T, 1, False)
    in_specs = [pl.BlockSpec((S, TT, D), lambda b, t: (b, t, 0)),
                pl.BlockSpec((R, RET_HEAD_DIM), lambda b, t: (t, 0)),
                pl.BlockSpec((R, RET_HEAD_DIM), lambda b, t: (t, 0))]
    in_specs += [_const_spec(c.shape) for c in consts[2:]]
    args = [x] + consts
    if sample:
        in_specs += [pl.BlockSpec((None, S, CONV_A_HIST, CONV_A_DIM), lambda b, t: (0, b, 0, 0)),
                     pl.BlockSpec((None, S, RET_HEADS, RET_HEAD_DIM, RET_HEAD_DIM),
                                  lambda b, t: (0, b, 0, 0, 0))]
        args += [conv_state, ret_state]
    out_shape = (jax.ShapeDtypeStruct((B, T, D), F32),
                 jax.ShapeDtypeStruct((B, CONV_A_HIST, CONV_A_DIM), F32),
                 jax.ShapeDtypeStruct((B, RET_HEADS, RET_HEAD_DIM, RET_HEAD_DIM), F32))
    out_specs = (pl.BlockSpec((S, TT, D), lambda b, t: (b, t, 0)),
                 pl.BlockSpec((S, CONV_A_HIST, CONV_A_DIM), lambda b, t: (b, 0, 0)),
                 pl.BlockSpec((S, RET_HEADS, RET_HEAD_DIM, RET_HEAD_DIM), lambda b, t: (b, 0, 0, 0)))
    return pl.pallas_call(
        functools.partial(_even_kernel, S=S, TT=TT, nt=nt, sample=sample),
        grid=(nb, nt), in_specs=in_specs, out_specs=out_specs, out_shape=out_shape,
        scratch_shapes=_even_scratch(S, TT, sample),
        compiler_params=pltpu.CompilerParams(dimension_semantics=("arbitrary", "arbitrary"),
                                             vmem_limit_bytes=VMEM_LIMIT_BYTES),
        name="even_mixer_sample" if sample else "even_mixer_prompt",
    )(*args)


ODD_CONSTS = ('gmix', 'w_in', 'conv_w', 'conv_b', 'w_ax', 'b_a', 'b_x', 'lam', 'w_out')
ODD_SCRATCH = ('xp_s', 'gate_s', 'xc_s', 'r_s', 'i_s', 'act_s')


def _odd_scratch(S, TT):
    R = S * TT
    return ([pltpu.VMEM((S, SUBLANES + TT, LRU_DIM), F32)]
            + [pltpu.VMEM((R, LRU_DIM), F32) for _ in range(4)]
            + [pltpu.VMEM((R, LRU_DIM), BF16)])


def _odd_init(r, h_live, conv_state, h_state):
    S = r.xp_s.shape[0]
    HP, H = SUBLANES, LRU_HIST
    r.xp_s[:, 0:HP, :] = jnp.zeros((S, HP, LRU_DIM), F32)
    if conv_state is not None:
        r.xp_s[:, HP - H:HP, :] = conv_state[...]
        h_live[...] = h_state[...]
    else:
        h_live[...] = jnp.zeros(h_live.shape, F32)


def _odd_tiles(r, x_src, y_dst, h_live, S, TT, sample):
    HP, H = SUBLANES, LRU_HIST
    if sample:
        tiles = [(0, S, 0, TT)]
    else:
        tiles = [(0, 1, r0, min(TT, ODD_SUB_ROWS)) for r0 in range(0, TT, ODD_SUB_ROWS)]
    pair = 2 * LRU_BLOCK

    def scan_group(rows, h_prev, decay, sub):
        rg = _sigmoid(r.r_s[rows, :])
        ig = _sigmoid(r.i_s[rows, :])
        a = jnp.exp2(decay * rg)
        om = jnp.maximum(1.0 - a * a, 0.0)
        root = jnp.where(om > 0.0, om * lax.rsqrt(om), 0.0)
        b = root * (ig * r.xc_s[rows, :])
        for sh in (1, 2, 4):
            keep = sub >= sh
            a_sh = jnp.where(keep, pltpu.roll(a, sh, axis=0), 1.0)
            b_sh = jnp.where(keep, pltpu.roll(b, sh, axis=0), 0.0)
            b = a * b_sh + b
            a = a * a_sh
        return a * h_prev + b

    def run(s0, ns, r0, nr):
        rows = ns * nr
        row0 = s0 * TT + r0
        nlam = -r.lam[...]
        softplus = jnp.maximum(nlam, 0.0) + jnp.log(1.0 + jnp.exp(-jnp.abs(nlam)))
        decay = jnp.broadcast_to((-LRU_C * math.log2(math.e)) * softplus, (SUBLANES, LRU_DIM))
        sub = lax.broadcasted_iota(jnp.int32, (SUBLANES, LRU_DIM), 0)
        x = x_src[s0:s0 + ns, r0:r0 + nr, :].reshape(rows, D_MODEL)
        h = _rms(x, r.gmix[...]).astype(BF16)
        r.gate_s[row0:row0 + rows, :] = _mm(h, r.w_in[:, 0:LRU_DIM])
        rec3 = _mm(h, r.w_in[:, LRU_DIM:2 * LRU_DIM]).reshape(ns, nr, LRU_DIM)
        r.xp_s[s0:s0 + ns, HP + r0:HP + r0 + nr, :] = rec3
        acc = r.conv_w[H:H + 1, :] * rec3 + r.conv_b[...]
        for j in range(H):
            lo = HP - H + j + r0
            acc = acc + r.conv_w[j:j + 1, :] * r.xp_s[s0:s0 + ns, lo:lo + nr, :]
        xc = acc.reshape(rows, LRU_DIM)
        r.xc_s[row0:row0 + rows, :] = xc
        xcb = xc.astype(BF16)
        for p in range(LRU_BLOCKS // 2):
            cs = slice(pair * p, pair * (p + 1))
            ri = _mm(xcb[:, cs], r.w_ax[p])
            r.r_s[row0:row0 + rows, cs] = ri[:, 0:pair] + r.b_a[:, cs]
            r.i_s[row0:row0 + rows, cs] = ri[:, pair:2 * pair] + r.b_x[:, cs]
        h_prev = None if sample else jnp.broadcast_to(h_live[0], (SUBLANES, LRU_DIM))
        for g0 in range(row0, row0 + rows, 2 * SUBLANES):
            parts = []
            for k in range(2):
                ga = g0 + k * SUBLANES
                if sample:
                    h_prev = jnp.broadcast_to(h_live[ga // TT], (SUBLANES, LRU_DIM))
                hs = scan_group(slice(ga, ga + SUBLANES), h_prev, decay, sub)
                h_last = hs[SUBLANES - 1:SUBLANES, :]
                if sample:
                    h_live[ga // TT] = h_last
                h_prev = jnp.broadcast_to(h_last, (SUBLANES, LRU_DIM))
                parts.append(hs)
            rows2 = slice(g0, g0 + 2 * SUBLANES)
            r.act_s[rows2, :] = (jnp.concatenate(parts, axis=0) * _gelu(r.gate_s[rows2, :])).astype(BF16)
        if not sample:
            h_live[0] = h_prev[0:1, :]
        y = _mm(r.act_s[row0:row0 + rows, :], r.w_out[...]) + x
        y_dst[s0:s0 + ns, r0:r0 + nr, :] = y.reshape(ns, nr, D_MODEL)

    return [functools.partial(run, *tl) for tl in tiles]


def _odd_carry(r, TT):
    HP = SUBLANES
    r.xp_s[:, 0:HP, :] = r.xp_s[:, TT:TT + HP, :]


def _odd_conv_state(r):
    return r.xp_s[:, SUBLANES - LRU_HIST:SUBLANES, :]


def _odd_const_args(p):
    return [p[n] for n in ODD_CONSTS]


def _odd_kernel(*refs, S, TT, nt, sample):
    it = iter(refs)
    x_ref = next(it)
    r = _take(it, ODD_CONSTS)
    convst_ref = next(it) if sample else None
    hst_ref = next(it) if sample else None
    y_ref = next(it); convout_ref = next(it); hout_ref = next(it)
    r.__dict__.update(_take(it, ODD_SCRATCH).__dict__)
    t = pl.program_id(1)

    @pl.when(t == 0)
    def _():
        _odd_init(r, hout_ref, convst_ref, hst_ref)

    for tile in _odd_tiles(r, x_ref, y_ref, hout_ref, S, TT, sample):
        tile()
    _odd_carry(r, TT)

    @pl.when(t == nt - 1)
    def _():
        convout_ref[...] = _odd_conv_state(r)


def _odd_mixer(x, conv_state, h_state, p, *, S, TT):
    B, T, D = x.shape
    sample = conv_state is not None
    nb, nt = B // S, T // TT
    consts = _odd_const_args(p)
    in_specs = [pl.BlockSpec((S, TT, D), lambda b, t: (b, t, 0))]
    in_specs += [_const_spec(c.shape) for c in consts]
    args = [x] + consts
    if sample:
        in_specs += [pl.BlockSpec((None, S, LRU_HIST, LRU_DIM), lambda b, t: (0, b, 0, 0)),
                     pl.BlockSpec((S, 1, LRU_DIM), lambda b, t: (b, 0, 0))]
        args += [conv_state, h_state]
    out_shape = (jax.ShapeDtypeStruct((B, T, D), F32),
                 jax.ShapeDtypeStruct((B, LRU_HIST, LRU_DIM), F32),
                 jax.ShapeDtypeStruct((B, 1, LRU_DIM), F32))
    out_specs = (pl.BlockSpec((S, TT, D), lambda b, t: (b, t, 0)),
                 pl.BlockSpec((S, LRU_HIST, LRU_DIM), lambda b, t: (b, 0, 0)),
                 pl.BlockSpec((S, 1, LRU_DIM), lambda b, t: (b, 0, 0)))
    return pl.pallas_call(
        functools.partial(_odd_kernel, S=S, TT=TT, nt=nt, sample=sample),
        grid=(nb, nt), in_specs=in_specs, out_specs=out_specs, out_shape=out_shape,
        scratch_shapes=_odd_scratch(S, TT),
        compiler_params=pltpu.CompilerParams(dimension_semantics=("arbitrary", "arbitrary"),
                                             vmem_limit_bytes=VMEM_LIMIT_BYTES),
        name="odd_mixer_sample" if sample else "odd_mixer_prompt",
    )(*args)


FFN_CONSTS = ('g', 'w_up', 'conv_w', 'conv_b', 'w_down')
FFN_SCRATCH = ('h_s', 'act_s', 'hist_s', 'work_s')


def _ffn_scratch(S, TT):
    R = S * TT
    return [pltpu.VMEM((R, D_MODEL), BF16),
            pltpu.VMEM((R, FFN_DIM), BF16),
            pltpu.VMEM((S, SUBLANES, 2 * FFN_DIM), F32),
            pltpu.VMEM((S, SUBLANES + TT, FFN_COL_CHUNK), F32)]


def _ffn_stages(r, x_src, y_dst, g_final, S, TT):
    R = S * TT
    HP, H, CK = SUBLANES, FFN_HIST, FFN_COL_CHUNK

    def prologue():
        x = x_src[...].reshape(R, D_MODEL)
        r.h_s[...] = _rms(x, r.g[...]).astype(BF16)

    def conv_cols(col):
        z3 = _mm(r.h_s[...], r.w_up[:, col:col + CK]).reshape(S, TT, CK)
        r.work_s[:, 0:HP, :] = r.hist_s[:, :, col:col + CK]
        r.work_s[:, HP:HP + TT, :] = z3
        zc = r.conv_w[H:H + 1, col:col + CK] * z3 + r.conv_b[:, col:col + CK]
        for j in range(H):
            zc = zc + r.conv_w[j:j + 1, col:col + CK] * r.work_s[:, HP - H + j:HP - H + j + TT, :]
        r.hist_s[:, :, col:col + CK] = r.work_s[:, TT:TT + HP, :]
        return zc.reshape(R, CK)

    def chunk(c):
        gz = conv_cols(c * CK)
        uz = conv_cols(FFN_DIM + c * CK)
        r.act_s[:, c * CK:(c + 1) * CK] = (_gelu(gz) * uz).astype(BF16)

    def epilogue():
        y = _mm(r.act_s[...], r.w_down[...]) + x_src[...].reshape(R, D_MODEL)
        if g_final is not None:
            y = _rms(y, g_final[...])
        y_dst[...] = y.reshape(S, TT, D_MODEL)

    return prologue, [functools.partial(chunk, c) for c in range(FFN_DIM // CK)], epilogue


def _ffn_state(r):
    return r.hist_s[:, SUBLANES - FFN_HIST:SUBLANES, :]


def _ffn_kernel(*refs, S, TT, nt, final):
    it = iter(refs)
    x_ref = next(it)
    r = _take(it, FFN_CONSTS)
    gfin_ref = next(it) if final else None
    y_ref = next(it); stout_ref = next(it)
    r.__dict__.update(_take(it, FFN_SCRATCH).__dict__)
    t = pl.program_id(1)

    @pl.when(t == 0)
    def _():
        r.hist_s[...] = jnp.zeros(r.hist_s.shape, F32)

    prologue, chunks, epilogue = _ffn_stages(r, x_ref, y_ref, gfin_ref, S, TT)
    prologue()
    for ch in chunks:
        ch()
    epilogue()

    @pl.when(t == nt - 1)
    def _():
        stout_ref[...] = _ffn_state(r)


def _layer_spec(shape, layer):
    nd = len(shape) - 1
    return pl.BlockSpec((None,) + tuple(shape[1:]), lambda b, t: (layer,) + (0,) * nd,
                        pipeline_mode=pl.Buffered(1))


def _conv_ffn(x, layer, p, g_final, *, S, TT):
    B, T, D = x.shape
    final = g_final is not None
    nb, nt = B // S, T // TT
    consts = [p[n] for n in FFN_CONSTS]
    in_specs = [pl.BlockSpec((S, TT, D), lambda b, t: (b, t, 0))]
    in_specs += [_layer_spec(c.shape, layer) for c in consts]
    args = [x] + consts
    if final:
        in_specs.append(_const_spec(g_final.shape))
        args.append(g_final)
    out_shape = (jax.ShapeDtypeStruct((B, T, D), F32),
                 jax.ShapeDtypeStruct((B, FFN_HIST, 2 * FFN_DIM), F32))
    out_specs = (pl.BlockSpec((S, TT, D), lambda b, t: (b, t, 0)),
                 pl.BlockSpec((S, FFN_HIST, 2 * FFN_DIM), lambda b, t: (b, 0, 0)))
    return pl.pallas_call(
        functools.partial(_ffn_kernel, S=S, TT=TT, nt=nt, final=final),
        grid=(nb, nt), in_specs=in_specs, out_specs=out_specs, out_shape=out_shape,
        scratch_shapes=_ffn_scratch(S, TT),
        compiler_params=pltpu.CompilerParams(dimension_semantics=("arbitrary", "arbitrary"),
                                             vmem_limit_bytes=VMEM_LIMIT_BYTES),
        name="ffn_prompt_final" if final else "ffn_prompt",
    )(*args)


def _ffn_cols_kernel(*refs, B, TT, nc, final):
    it = iter(refs)
    x_ref = next(it); g_ref = next(it)
    wg_ref = next(it); wu_ref = next(it)
    cwg_ref = next(it); cwu_ref = next(it); cbg_ref = next(it); cbu_ref = next(it)
    wdn_ref = next(it)
    gfin_ref = next(it) if final else None
    stg_ref = next(it); stu_ref = next(it)
    y_ref = next(it); stgout_ref = next(it); stuout_ref = next(it)
    h_s = next(it); act_s = next(it)

    c = pl.program_id(0)
    R = B * TT
    CK = FFN_COL_CHUNK
    H = FFN_HIST

    @pl.when(c == 0)
    def _():
        x = x_ref[...].reshape(R, D_MODEL)
        h_s[...] = _rms(x, g_ref[...]).astype(BF16)

    tpos = lax.broadcasted_iota(jnp.int32, (B, TT, CK), 1)

    def conv(w_ref, cw_ref, cb_ref, st_ref, stout_ref):
        z3 = _mm(h_s[...], w_ref[...]).reshape(B, TT, CK)
        stout_ref[...] = z3[:, TT - H:TT, :]
        prev1 = st_ref[:, 1:2, :]
        prev2 = st_ref[:, 0:1, :]
        zm1 = jnp.where(tpos >= 1, pltpu.roll(z3, 1, axis=1), prev1)
        zm2 = jnp.where(tpos >= 2, pltpu.roll(z3, 2, axis=1), jnp.where(tpos == 1, prev1, prev2))
        zc = cw_ref[2:3, :] * z3 + cw_ref[1:2, :] * zm1 + cw_ref[0:1, :] * zm2 + cb_ref[...]
        return zc.reshape(R, CK)

    gz = conv(wg_ref, cwg_ref, cbg_ref, stg_ref, stgout_ref)
    uz = conv(wu_ref, cwu_ref, cbu_ref, stu_ref, stuout_ref)
    act_s[c] = (_gelu(gz) * uz).astype(BF16)

    @pl.when(c == nc - 1)
    def _():
        act = jnp.concatenate([act_s[k] for k in range(nc)], axis=-1)
        y = _mm(act, wdn_ref[...]) + x_ref[...].reshape(R, D_MODEL)
        if final:
            y = _rms(y, gfin_ref[...])
        y_ref[...] = y.reshape(B, TT, D_MODEL)


def _conv_ffn_sample(x, state, layer, p, g_final):
    B, T, D = x.shape
    assert T == SUBLANES and FFN_CONV_WIDTH == 3
    final = g_final is not None
    CK = FFN_COL_CHUNK
    nc = FFN_DIM // CK

    def cols(shape, off):
        nd = len(shape) - 2
        return pl.BlockSpec((None,) + tuple(shape[1:-1]) + (CK,), lambda c: (layer,) + (0,) * nd + (off + c,))

    in_specs = [pl.BlockSpec((B, T, D), lambda c: (0, 0, 0), pipeline_mode=pl.Buffered(1)),
                pl.BlockSpec((None, 1, D), lambda c: (layer, 0, 0), pipeline_mode=pl.Buffered(1)),
                cols(p['w_up'].shape, 0), cols(p['w_up'].shape, nc),
                cols(p['conv_w'].shape, 0), cols(p['conv_w'].shape, nc),
                cols(p['conv_b'].shape, 0), cols(p['conv_b'].shape, nc),
                pl.BlockSpec((None, FFN_DIM, D), lambda c: (layer, 0, 0), pipeline_mode=pl.Buffered(1))]
    args = [x, p['g'], p['w_up'], p['w_up'], p['conv_w'], p['conv_w'], p['conv_b'], p['conv_b'], p['w_down']]
    if final:
        in_specs.append(pl.BlockSpec(g_final.shape, lambda c: (0, 0), pipeline_mode=pl.Buffered(1)))
        args.append(g_final)
    in_specs += [cols(state.shape, 0), cols(state.shape, nc)]
    args += [state, state]
    out_shape = (jax.ShapeDtypeStruct((B, T, D), F32),
                 jax.ShapeDtypeStruct((B, FFN_HIST, FFN_DIM), F32),
                 jax.ShapeDtypeStruct((B, FFN_HIST, FFN_DIM), F32))
    out_specs = (pl.BlockSpec((B, T, D), lambda c: (0, 0, 0)),
                 pl.BlockSpec((B, FFN_HIST, CK), lambda c: (0, 0, c)),
                 pl.BlockSpec((B, FFN_HIST, CK), lambda c: (0, 0, c)))
    y, st_g, st_u = pl.pallas_call(
        functools.partial(_ffn_cols_kernel, B=B, TT=T, nc=nc, final=final),
        grid=(nc,), in_specs=in_specs, out_specs=out_specs, out_shape=out_shape,
        scratch_shapes=[pltpu.VMEM((B * T, D), BF16), pltpu.VMEM((nc, B * T, CK), BF16)],
        compiler_params=pltpu.CompilerParams(dimension_semantics=("arbitrary",),
                                             vmem_limit_bytes=VMEM_LIMIT_BYTES),
        name="ffn_sample_final" if final else "ffn_sample",
    )(*args)
    return y, jnp.concatenate([st_g, st_u], axis=-1)


PROMPT_TILES = dict(even=dict(S=1, TT=512), odd=dict(S=1, TT=1024), ffn=dict(S=1, TT=1024))
SAMPLE_TILES = dict(even=dict(S=16, TT=8), odd=dict(S=32, TT=8))


def _row(v):
    return v.reshape(1, -1)


def _diag_taps(w):
    half = CONV_A_DIM // 2
    w = w[np.array(CONV_A_MXU_TAPS)].reshape(len(CONV_A_MXU_TAPS), 2, 1, half)
    return (jnp.eye(half, dtype=F32)[None, None] * w).astype(BF16)


def _pair_block_diag(w_a, w_x):
    def pairs(w):
        w = w.reshape(LRU_BLOCKS // 2, 2, LRU_BLOCK, LRU_BLOCK)
        z = jnp.zeros_like(w[:, 0])
        top = jnp.concatenate([w[:, 0], z], axis=-1)
        bot = jnp.concatenate([z, w[:, 1]], axis=-1)
        return jnp.concatenate([top, bot], axis=-2)
    return jnp.concatenate([pairs(w_a), pairs(w_x)], axis=-1).astype(BF16)


def kernel(x_prompt, x_sample, state_conv_a, state_ret, state_lru_conv, state_lru_h, state_ffn_conv, norm_mix, norm_ffn, norm_final, w_in_ab, conv_a_w, conv_a_b, ln_a_g, ln_a_b, gn_ret_g, w_out_ab, w_in_c, conv_c_w, conv_c_b, w_lru_a, b_lru_a, w_lru_x, b_lru_x, lru_lambda, w_out_c, w_ffn_up, ffn_conv_w, ffn_conv_b, w_ffn_down):
    pe = dict(gmix=_row(norm_mix[0]), w_in=w_in_ab[0].astype(BF16), conv_w=conv_a_w[0],
              conv_wd=_diag_taps(conv_a_w[0]),
              conv_b=_row(conv_a_b[0]), ln_g=_row(ln_a_g[0]), ln_b=_row(ln_a_b[0]),
              gn_g=_row(gn_ret_g[0]), w_out=w_out_ab[0].astype(BF16))
    po = dict(gmix=_row(norm_mix[1]), w_in=w_in_c[0].astype(BF16), conv_w=conv_c_w[0],
              conv_b=_row(conv_c_b[0]), w_ax=_pair_block_diag(w_lru_a[0], w_lru_x[0]),
              b_a=_row(b_lru_a[0]), b_x=_row(b_lru_x[0]), lam=_row(lru_lambda[0]),
              w_out=w_out_c[0].astype(BF16))
    pf = dict(g=norm_ffn[:, None, :], w_up=w_ffn_up.astype(BF16), conv_w=ffn_conv_w,
              conv_b=ffn_conv_b[:, None, :], w_down=w_ffn_down.astype(BF16))
    g_final = _row(norm_final)

    xp, p_conv_a, p_ret = _even_mixer(x_prompt, None, None, pe, **PROMPT_TILES['even'])
    xp, p_ffn0 = _conv_ffn(xp, 0, pf, None, **PROMPT_TILES['ffn'])
    xp, p_lru_conv, p_lru_h = _odd_mixer(xp, None, None, po, **PROMPT_TILES['odd'])
    y_prompt, p_ffn1 = _conv_ffn(xp, 1, pf, g_final, **PROMPT_TILES['ffn'])

    xs, s_conv_a, s_ret = _even_mixer(x_sample, state_conv_a, state_ret, pe, **SAMPLE_TILES['even'])
    xs, s_ffn0 = _conv_ffn_sample(xs, state_ffn_conv, 0, pf, None)
    xs, s_lru_conv, s_lru_h = _odd_mixer(xs, state_lru_conv, state_lru_h[0][:, None, :], po,
                                         **SAMPLE_TILES['odd'])
    y_sample, s_ffn1 = _conv_ffn_sample(xs, state_ffn_conv, 1, pf, g_final)

    return (y_prompt, y_sample,
            p_conv_a[None], p_ret[None], p_lru_conv[None], p_lru_h[:, 0, :][None],
            jnp.stack([p_ffn0, p_ffn1]),
            s_conv_a[None], s_ret[None], s_lru_conv[None], s_lru_h[:, 0, :][None],
            jnp.stack([s_ffn0, s_ffn1]))
```

```python
import functools
import math
import types

import numpy as np
import jax
import jax.numpy as jnp
from jax import lax
from jax.experimental import pallas as pl
from jax.experimental.pallas import tpu as pltpu

F32 = jnp.float32
BF16 = jnp.bfloat16

D_MODEL = 1024
CONV_A_DIM = 512
CONV_A_WIDTH = 31
CONV_A_HIST = CONV_A_WIDTH - 1
CONV_A_HIST_PAD = 32
CONV_A_ROW_BLOCK = 32
CONV_A_SEQ_BLOCK = 4
LN_EPS = 1e-5
RET_HEADS = 4
RET_HEAD_DIM = 128
RET_DIM = RET_HEADS * RET_HEAD_DIM
RET_CHUNK = 128
ROPE_BASE = 10000.0
GN_EPS = 1e-5
IN_AB_DIM = 2 * CONV_A_DIM + 4 * RET_DIM
Z_SECTION = 512
LRU_DIM = 1024
LRU_BLOCKS = 8
LRU_BLOCK = LRU_DIM // LRU_BLOCKS
LRU_CONV_WIDTH = 4
LRU_HIST = LRU_CONV_WIDTH - 1
LRU_C = 8.0
ODD_SUB_ROWS = 256
FFN_DIM = 2816
FFN_CONV_WIDTH = 3
FFN_HIST = FFN_CONV_WIDTH - 1
FFN_COL_CHUNK = 256
RMS_EPS = 1e-6
PAST_LEN = 16384
SUBLANES = 8
SAMPLE_GROUP = RET_CHUNK // SUBLANES

VMEM_LIMIT_BYTES = 56 * 1024 * 1024


def _rms(x, g):
    return x * lax.rsqrt(jnp.mean(x * x, axis=-1, keepdims=True) + RMS_EPS) * g


def _sigmoid(x):
    return 1.0 / (1.0 + jnp.exp(-x))


def _gelu(x):
    c = math.sqrt(2.0 / math.pi)
    return x * (0.5 + 0.5 * jnp.tanh(x * (c + (c * 0.044715) * (x * x))))


def _mm(a, b):
    return jnp.dot(a, b, preferred_element_type=F32)


def _const_spec(shape):
    nd = len(shape)
    return pl.BlockSpec(shape, lambda b, t: (0,) * nd, pipeline_mode=pl.Buffered(1))


def _take(it, names):
    return types.SimpleNamespace(**{n: next(it) for n in names})


EVEN_CONSTS = ('cos', 'sin', 'dmask', 'qd', 'kd', 'cd', 'gmix', 'w_in', 'conv_w', 'conv_wd', 'conv_b',
               'ln_g', 'ln_b', 'gn_g', 'w_out')
EVEN_SCRATCH = ('xp_s', 'xs_s', 'yb_s', 'cv_s', 'ya_s')


def _conv_a_tap(j):
    off = CONV_A_HIST_PAD - CONV_A_HIST + j
    return off % SUBLANES, off - off % SUBLANES


CONV_A_VPU_TAPS = tuple(j for j in range(CONV_A_WIDTH) if _conv_a_tap(j)[0] <= 4)
CONV_A_MXU_TAPS = tuple(j for j in range(CONV_A_WIDTH) if _conv_a_tap(j)[0] > 4)


def _even_scratch(S, TT, sample):
    return [pltpu.VMEM((S, CONV_A_HIST_PAD + TT, CONV_A_DIM), F32),
            pltpu.VMEM((SUBLANES - 1, S, TT + CONV_A_HIST_PAD - SUBLANES, CONV_A_DIM), F32),
            pltpu.VMEM((S * TT, RET_DIM), BF16),
            pltpu.VMEM((S * TT, CONV_A_DIM), F32),
            pltpu.VMEM((S * TT, CONV_A_DIM), BF16)]


def _even_init(r, ret_live, conv_state, ret_state):
    S = r.xp_s.shape[0]
    HP, H = CONV_A_HIST_PAD, CONV_A_HIST
    r.xp_s[:, 0:HP, :] = jnp.zeros((S, HP, CONV_A_DIM), F32)
    if conv_state is not None:
        r.xp_s[:, HP - H:HP, :] = conv_state[...]
        ret_live[...] = ret_state[...]
    else:
        ret_live[...] = jnp.zeros(ret_live.shape, F32)


def _even_tiles(r, x_src, y_dst, ret_live, S, TT, sample):
    HP, H = CONV_A_HIST_PAD, CONV_A_HIST
    if sample:
        tiles = [(s0, SAMPLE_GROUP, 0, TT) for s0 in range(0, S, SAMPLE_GROUP)]
    else:
        tiles = [(0, 1, r0, RET_CHUNK) for r0 in range(0, TT, RET_CHUNK)]
    o0 = 2 * CONV_A_DIM
    scale = RET_HEAD_DIM ** -0.5

    def run(s0, ns, r0, nr):
        rows = ns * nr
        row0 = s0 * TT + r0
        x = x_src[s0:s0 + ns, r0:r0 + nr, :].reshape(rows, D_MODEL)
        h = _rms(x, r.gmix[...]).astype(BF16)
        zs = [_mm(h, r.w_in[:, c0:c0 + Z_SECTION]) for c0 in range(0, IN_AB_DIM, Z_SECTION)]

        def zcols(lo, hi):
            k = lo // Z_SECTION
            return zs[k][:, lo - k * Z_SECTION:hi - k * Z_SECTION]

        u = zcols(0, CONV_A_DIM) * _sigmoid(zcols(CONV_A_DIM, 2 * CONV_A_DIM))
        r.xp_s[s0:s0 + ns, HP + r0:HP + r0 + nr, :] = u.reshape(ns, nr, CONV_A_DIM)

        cos = r.cos[row0:row0 + rows, :]
        sin = r.sin[row0:row0 + rows, :]
        gate = zcols(o0 + 3 * RET_DIM, o0 + 4 * RET_DIM)
        o_parts = []
        for hh in range(RET_HEADS):
            lo = hh * RET_HEAD_DIM
            hi = lo + RET_HEAD_DIM
            qh = zcols(o0 + lo, o0 + hi)
            kh = zcols(o0 + RET_DIM + lo, o0 + RET_DIM + hi)
            qc = qh * cos + pltpu.roll(qh, RET_HEAD_DIM // 2, axis=1) * sin
            kc = (kh * cos + pltpu.roll(kh, RET_HEAD_DIM // 2, axis=1) * sin) * scale
            vc = zcols(o0 + 2 * RET_DIM + lo, o0 + 2 * RET_DIM + hi)
            qb = qc.astype(BF16)
            vb = vc.astype(BF16)
            scores = lax.dot_general(qb, kc.astype(BF16), (((1,), (1,)), ((), ())),
                                     preferred_element_type=F32) * r.dmask[hh]
            inner = _mm(scores.astype(BF16), vb)
            if not sample:
                kdv = (kc * r.kd[hh]).astype(BF16)
                st = ret_live[0, hh]
                cross = _mm(qb, st.astype(BF16))
                upd = lax.dot_general(kdv, vb, (((0,), (0,)), ((), ())), preferred_element_type=F32)
                ret_live[0, hh] = st * r.cd[hh] + upd
            else:
                parts = []
                for sq in range(SAMPLE_GROUP):
                    sidx = s0 + sq
                    rs = slice(sq * SUBLANES, (sq + 1) * SUBLANES)
                    st = ret_live[sidx, hh]
                    parts.append(_mm(qc[rs].astype(BF16), st.astype(BF16)))
                    kdv_s = (kc[rs] * r.kd[hh, rs, :]).astype(BF16)
                    upd = lax.dot_general(kdv_s, vc[rs].astype(BF16), (((0,), (0,)), ((), ())),
                                          preferred_element_type=F32)
                    ret_live[sidx, hh] = st * r.cd[hh] + upd
                cross = jnp.concatenate(parts, axis=0)
            o = inner + cross * r.qd[hh]
            mu_o = jnp.mean(o, axis=-1, keepdims=True)
            oc = o - mu_o
            var_o = jnp.mean(oc * oc, axis=-1, keepdims=True)
            o_parts.append(oc * lax.rsqrt(var_o + GN_EPS) * r.gn_g[:, lo:hi])

        yb = jnp.concatenate(o_parts, axis=-1) * (gate * _sigmoid(gate))
        r.yb_s[row0:row0 + rows, :] = yb.astype(BF16)

    def conv_and_project():
        R = S * TT
        L = TT + HP - SUBLANES
        for b in range(1, SUBLANES):
            r.xs_s[b - 1] = r.xp_s[:, b:b + L, :]
        half = CONV_A_DIM // 2
        if sample:
            blocks = [(sb, CONV_A_SEQ_BLOCK, 0, TT) for sb in range(0, S, CONV_A_SEQ_BLOCK)]
        else:
            blocks = [(0, 1, rb, CONV_A_ROW_BLOCK) for rb in range(0, TT, CONV_A_ROW_BLOCK)]
        for sb, nsb, rb, nrb in blocks:
            rowb = sb * TT + rb
            acc = jnp.zeros((nsb, nrb, CONV_A_DIM), F32) + r.conv_b[...]
            for j in CONV_A_VPU_TAPS:
                b, lo = _conv_a_tap(j)
                if b == 0:
                    win = r.xp_s[sb:sb + nsb, lo + rb:lo + rb + nrb, :]
                else:
                    win = r.xs_s[b - 1, sb:sb + nsb, lo + rb:lo + rb + nrb, :]
                acc = acc + r.conv_w[j:j + 1, :] * win
            r.cv_s[rowb:rowb + nsb * nrb, :] = acc.reshape(nsb * nrb, CONV_A_DIM)
        mxu_parts = []
        for p in range(2):
            cs = slice(p * half, (p + 1) * half)
            acc = jnp.zeros((R, half), F32)
            for k, j in enumerate(CONV_A_MXU_TAPS):
                b, lo = _conv_a_tap(j)
                win = r.xs_s[b - 1, :, lo:lo + TT, cs]
                acc = acc + _mm(win.reshape(R, half).astype(BF16), r.conv_wd[k, p])
            mxu_parts.append(acc)
        cv_mxu = jnp.concatenate(mxu_parts, axis=-1)
        for sb, nsb, rb, nrb in blocks:
            rowb = sb * TT + rb
            rows_b = slice(rowb, rowb + nsb * nrb)
            cv = r.cv_s[rows_b, :] + cv_mxu[rows_b, :]
            mu = jnp.mean(cv, axis=-1, keepdims=True)
            cvc = cv - mu
            var = jnp.mean(cvc * cvc, axis=-1, keepdims=True)
            ln = cvc * lax.rsqrt(var + LN_EPS) * r.ln_g[...] + r.ln_b[...]
            r.ya_s[rowb:rowb + nsb * nrb, :] = (ln * _sigmoid(ln)).astype(BF16)
        y = (_mm(r.ya_s[...], r.w_out[0:CONV_A_DIM, :])
             + _mm(r.yb_s[...], r.w_out[CONV_A_DIM:CONV_A_DIM + RET_DIM, :])
             + x_src[...].reshape(R, D_MODEL))
        y_dst[...] = y.reshape(S, TT, D_MODEL)

    return [functools.partial(run, *tl) for tl in tiles] + [conv_and_project]


def _even_carry(r, TT):
    HP = CONV_A_HIST_PAD
    r.xp_s[:, 0:HP, :] = r.xp_s[:, TT:TT + HP, :]


def _even_conv_state(r):
    return r.xp_s[:, CONV_A_HIST_PAD - CONV_A_HIST:CONV_A_HIST_PAD, :]


def _rope_tables(pos0, T, reps):
    d = RET_HEAD_DIM
    inv_freq = ROPE_BASE ** (-np.arange(0, d, 2, dtype=np.float64) / d)
    ang = (pos0 + np.arange(T, dtype=np.float64))[:, None] * inv_freq[None, :]
    cos = np.cos(ang)
    sin = np.sin(ang)
    cos2 = np.concatenate([cos, cos], axis=-1)
    sin2 = np.concatenate([-sin, sin], axis=-1)
    return (jnp.asarray(np.tile(cos2, (reps, 1)), F32), jnp.asarray(np.tile(sin2, (reps, 1)), F32))


def _decay_tables(c, groups):
    nh = RET_HEADS
    log_gamma = np.log(1.0 - 2.0 ** (-5.0 - np.arange(nh, dtype=np.float64)))
    idx = np.arange(c, dtype=np.float64)
    rel = idx[:, None] - idx[None, :]
    dmask = np.where(rel >= 0, np.exp(np.maximum(rel, 0.0)[None] * log_gamma[:, None, None]), 0.0)
    qd = np.exp((idx + 1.0)[None, :] * log_gamma[:, None])
    kd = np.exp((c - 1.0 - idx)[None, :] * log_gamma[:, None])
    cd = np.exp(c * log_gamma)
    if groups > 1:
        eye = np.eye(groups)
        dmask = np.einsum('gk,hij->hgikj', eye, dmask).reshape(nh, groups * c, groups * c)
        qd = np.tile(qd, (1, groups))
        kd = np.tile(kd, (1, groups))
    n = groups * c
    qd = np.broadcast_to(qd[:, :, None], (nh, n, RET_HEAD_DIM))
    kd = np.broadcast_to(kd[:, :, None], (nh, n, RET_HEAD_DIM))
    cd = np.broadcast_to(cd[:, None, None], (nh, 1, RET_HEAD_DIM))
    return tuple(jnp.asarray(np.ascontiguousarray(a), F32) for a in (dmask, qd, kd, cd))


def _even_const_args(p, pos0, T, reps, sample):
    cos, sin = _rope_tables(pos0, T, reps)
    dmask, qd, kd, cd = _decay_tables(T, SAMPLE_GROUP) if sample else _decay_tables(RET_CHUNK, 1)
    return [cos, sin, dmask, qd, kd, cd, p['gmix'], p['w_in'], p['conv_w'], p['conv_wd'], p['conv_b'],
            p['ln_g'], p['ln_b'], p['gn_g'], p['w_out']]


def _even_kernel(*refs, S, TT, nt, sample):
    it = iter(refs)
    x_ref = next(it)
    r = _take(it, EVEN_CONSTS)
    convst_ref = next(it) if sample else None
    retst_ref = next(it) if sample else None
    y_ref = next(it); convout_ref = next(it); retout_ref = next(it)
    r.__dict__.update(_take(it, EVEN_SCRATCH).__dict__)
    t = pl.program_id(1)

    @pl.when(t == 0)
    def _():
        _even_init(r, retout_ref, convst_ref, retst_ref)

    for tile in _even_tiles(r, x_ref, y_ref, retout_ref, S, TT, sample):
        tile()
    _even_carry(r, TT)

    @pl.when(t == nt - 1)
    def _():
        convout_ref[...] = _even_conv_state(r)


def _even_mixer(x, conv_state, ret_state, p, *, S, TT):
    B, T, D = x.shape
    sample = conv_state is not None
    nb, nt = B // S, T // TT
    R = S * TT
    if sample:
        assert TT == T == SUBLANES and S % SAMPLE_GROUP == 0
        consts = _even_const_args(p, PAST_LEN, T, S, True)
    else:
        assert S == 1 and TT % RET_CHUNK == 0
        consts = _even_const_args(p, 0, T, 1, False)
    in_specs = [pl.BlockSpec((S, TT, D), lambda b, t: (b, t, 0)),
                pl.BlockSpec((R, RET_HEAD_DIM), lambda b, t: (t, 0)),
                pl.BlockSpec((R, RET_HEAD_DIM), lambda b, t: (t, 0))]
    in_specs += [_const_spec(c.shape) for c in consts[2:]]
    args = [x] + consts
    if sample:
        in_specs += [pl.BlockSpec((None, S, CONV_A_HIST, CONV_A_DIM), lambda b, t: (0, b, 0, 0)),
                     pl.BlockSpec((None, S, RET_HEADS, RET_HEAD_DIM, RET_HEAD_DIM),
                                  lambda b, t: (0, b, 0, 0, 0))]
        args += [conv_state, ret_state]
    out_shape = (jax.ShapeDtypeStruct((B, T, D), F32),
                 jax.ShapeDtypeStruct((B, CONV_A_HIST, CONV_A_DIM), F32),
                 jax.ShapeDtypeStruct((B, RET_HEADS, RET_HEAD_DIM, RET_HEAD_DIM), F32))
    out_specs = (pl.BlockSpec((S, TT, D), lambda b, t: (b, t, 0)),
                 pl.BlockSpec((S, CONV_A_HIST, CONV_A_DIM), lambda b, t: (b, 0, 0)),
                 pl.BlockSpec((S, RET_HEADS, RET_HEAD_DIM, RET_HEAD_DIM), lambda b, t: (b, 0, 0, 0)))
    return pl.pallas_call(
        functools.partial(_even_kernel, S=S, TT=TT, nt=nt, sample=sample),
        grid=(nb, nt), in_specs=in_specs, out_specs=out_specs, out_shape=out_shape,
        scratch_shapes=_even_scratch(S, TT, sample),
        compiler_params=pltpu.CompilerParams(dimension_semantics=("arbitrary", "arbitrary"),
                                             vmem_limit_bytes=VMEM_LIMIT_BYTES),
        name="even_mixer_sample" if sample else "even_mixer_prompt",
    )(*args)


ODD_CONSTS = ('gmix', 'w_in', 'conv_w', 'conv_b', 'w_ax', 'b_a', 'b_x', 'lam', 'w_out')
ODD_SCRATCH = ('xp_s', 'gate_s', 'xc_s', 'r_s', 'i_s', 'act_s')


def _odd_scratch(S, TT):
    R = S * TT
    return ([pltpu.VMEM((S, SUBLANES + TT, LRU_DIM), F32)]
            + [pltpu.VMEM((R, LRU_DIM), F32) for _ in range(4)]
            + [pltpu.VMEM((R, LRU_DIM), BF16)])


def _odd_init(r, h_live, conv_state, h_state):
    S = r.xp_s.shape[0]
    HP, H = SUBLANES, LRU_HIST
    r.xp_s[:, 0:HP, :] = jnp.zeros((S, HP, LRU_DIM), F32)
    if conv_state is not None:
        r.xp_s[:, HP - H:HP, :] = conv_state[...]
        h_live[...] = h_state[...]
    else:
        h_live[...] = jnp.zeros(h_live.shape, F32)


def _odd_tiles(r, x_src, y_dst, h_live, S, TT, sample):
    HP, H = SUBLANES, LRU_HIST
    if sample:
        tiles = [(0, S, 0, TT)]
    else:
        tiles = [(0, 1, r0, min(TT, ODD_SUB_ROWS)) for r0 in range(0, TT, ODD_SUB_ROWS)]
    pair = 2 * LRU_BLOCK

    def scan_group(rows, h_prev, decay, sub):
        rg = _sigmoid(r.r_s[rows, :])
        ig = _sigmoid(r.i_s[rows, :])
        a = jnp.exp2(decay * rg)
        om = jnp.maximum(1.0 - a * a, 0.0)
        root = jnp.where(om > 0.0, om * lax.rsqrt(om), 0.0)
        b = root * (ig * r.xc_s[rows, :])
        for sh in (1, 2, 4):
            keep = sub >= sh
            a_sh = jnp.where(keep, pltpu.roll(a, sh, axis=0), 1.0)
            b_sh = jnp.where(keep, pltpu.roll(b, sh, axis=0), 0.0)
            b = a * b_sh + b
            a = a * a_sh
        return a * h_prev + b

    def run(s0, ns, r0, nr):
        rows = ns * nr
        row0 = s0 * TT + r0
        nlam = -r.lam[...]
        softplus = jnp.maximum(nlam, 0.0) + jnp.log(1.0 + jnp.exp(-jnp.abs(nlam)))
        decay = jnp.broadcast_to((-LRU_C * math.log2(math.e)) * softplus, (SUBLANES, LRU_DIM))
        sub = lax.broadcasted_iota(jnp.int32, (SUBLANES, LRU_DIM), 0)
        x = x_src[s0:s0 + ns, r0:r0 + nr, :].reshape(rows, D_MODEL)
        h = _rms(x, r.gmix[...]).astype(BF16)
        r.gate_s[row0:row0 + rows, :] = _mm(h, r.w_in[:, 0:LRU_DIM])
        rec3 = _mm(h, r.w_in[:, LRU_DIM:2 * LRU_DIM]).reshape(ns, nr, LRU_DIM)
        r.xp_s[s0:s0 + ns, HP + r0:HP + r0 + nr, :] = rec3
        acc = r.conv_w[H:H + 1, :] * rec3 + r.conv_b[...]
        for j in range(H):
            lo = HP - H + j + r0
            acc = acc + r.conv_w[j:j + 1, :] * r.xp_s[s0:s0 + ns, lo:lo + nr, :]
        xc = acc.reshape(rows, LRU_DIM)
        r.xc_s[row0:row0 + rows, :] = xc
        xcb = xc.astype(BF16)
        for p in range(LRU_BLOCKS // 2):
            cs = slice(pair * p, pair * (p + 1))
            ri = _mm(xcb[:, cs], r.w_ax[p])
            r.r_s[row0:row0 + rows, cs] = ri[:, 0:pair] + r.b_a[:, cs]
            r.i_s[row0:row0 + rows, cs] = ri[:, pair:2 * pair] + r.b_x[:, cs]
        h_prev = None if sample else jnp.broadcast_to(h_live[0], (SUBLANES, LRU_DIM))
        for g0 in range(row0, row0 + rows, 2 * SUBLANES):
            parts = []
            for k in range(2):
                ga = g0 + k * SUBLANES
                if sample:
                    h_prev = jnp.broadcast_to(h_live[ga // TT], (SUBLANES, LRU_DIM))
                hs = scan_group(slice(ga, ga + SUBLANES), h_prev, decay, sub)
                h_last = hs[SUBLANES - 1:SUBLANES, :]
                if sample:
                    h_live[ga // TT] = h_last
                h_prev = jnp.broadcast_to(h_last, (SUBLANES, LRU_DIM))
                parts.append(hs)
            rows2 = slice(g0, g0 + 2 * SUBLANES)
            r.act_s[rows2, :] = (jnp.concatenate(parts, axis=0) * _gelu(r.gate_s[rows2, :])).astype(BF16)
        if not sample:
            h_live[0] = h_prev[0:1, :]
        y = _mm(r.act_s[row0:row0 + rows, :], r.w_out[...]) + x
        y_dst[s0:s0 + ns, r0:r0 + nr, :] = y.reshape(ns, nr, D_MODEL)

    return [functools.partial(run, *tl) for tl in tiles]


def _odd_carry(r, TT):
    HP = SUBLANES
    r.xp_s[:, 0:HP, :] = r.xp_s[:, TT:TT + HP, :]


def _odd_conv_state(r):
    return r.xp_s[:, SUBLANES - LRU_HIST:SUBLANES, :]


def _odd_const_args(p):
    return [p[n] for n in ODD_CONSTS]


def _odd_kernel(*refs, S, TT, nt, sample):
    it = iter(refs)
    x_ref = next(it)
    r = _take(it, ODD_CONSTS)
    convst_ref = next(it) if sample else None
    hst_ref = next(it) if sample else None
    y_ref = next(it); convout_ref = next(it); hout_ref = next(it)
    r.__dict__.update(_take(it, ODD_SCRATCH).__dict__)
    t = pl.program_id(1)

    @pl.when(t == 0)
    def _():
        _odd_init(r, hout_ref, convst_ref, hst_ref)

    for tile in _odd_tiles(r, x_ref, y_ref, hout_ref, S, TT, sample):
        tile()
    _odd_carry(r, TT)

    @pl.when(t == nt - 1)
    def _():
        convout_ref[...] = _odd_conv_state(r)


def _odd_mixer(x, conv_state, h_state, p, *, S, TT):
    B, T, D = x.shape
    sample = conv_state is not None
    nb, nt = B // S, T // TT
    consts = _odd_const_args(p)
    in_specs = [pl.BlockSpec((S, TT, D), lambda b, t: (b, t, 0))]
    in_specs += [_const_spec(c.shape) for c in consts]
    args = [x] + consts
    if sample:
        in_specs += [pl.BlockSpec((None, S, LRU_HIST, LRU_DIM), lambda b, t: (0, b, 0, 0)),
                     pl.BlockSpec((S, 1, LRU_DIM), lambda b, t: (b, 0, 0))]
        args += [conv_state, h_state]
    out_shape = (jax.ShapeDtypeStruct((B, T, D), F32),
                 jax.ShapeDtypeStruct((B, LRU_HIST, LRU_DIM), F32),
                 jax.ShapeDtypeStruct((B, 1, LRU_DIM), F32))
    out_specs = (pl.BlockSpec((S, TT, D), lambda b, t: (b, t, 0)),
                 pl.BlockSpec((S, LRU_HIST, LRU_DIM), lambda b, t: (b, 0, 0)),
                 pl.BlockSpec((S, 1, LRU_DIM), lambda b, t: (b, 0, 0)))
    return pl.pallas_call(
        functools.partial(_odd_kernel, S=S, TT=TT, nt=nt, sample=sample),
        grid=(nb, nt), in_specs=in_specs, out_specs=out_specs, out_shape=out_shape,
        scratch_shapes=_odd_scratch(S, TT),
        compiler_params=pltpu.CompilerParams(dimension_semantics=("arbitrary", "arbitrary"),
                                             vmem_limit_bytes=VMEM_LIMIT_BYTES),
        name="odd_mixer_sample" if sample else "odd_mixer_prompt",
    )(*args)


FFN_CONSTS = ('g', 'w_up', 'conv_w', 'conv_b', 'w_down')
FFN_SCRATCH = ('h_s', 'act_s', 'hist_s', 'work_s')


def _ffn_scratch(S, TT):
    R = S * TT
    return [pltpu.VMEM((R, D_MODEL), BF16),
            pltpu.VMEM((R, FFN_DIM), BF16),
            pltpu.VMEM((S, SUBLANES, 2 * FFN_DIM), F32),
            pltpu.VMEM((S, SUBLANES + TT, FFN_COL_CHUNK), F32)]


def _ffn_stages(r, x_src, y_dst, g_final, S, TT):
    R = S * TT
    HP, H, CK = SUBLANES, FFN_HIST, FFN_COL_CHUNK

    def prologue():
        x = x_src[...].reshape(R, D_MODEL)
        r.h_s[...] = _rms(x, r.g[...]).astype(BF16)

    def conv_cols(col):
        z3 = _mm(r.h_s[...], r.w_up[:, col:col + CK]).reshape(S, TT, CK)
        r.work_s[:, 0:HP, :] = r.hist_s[:, :, col:col + CK]
        r.work_s[:, HP:HP + TT, :] = z3
        zc = r.conv_w[H:H + 1, col:col + CK] * z3 + r.conv_b[:, col:col + CK]
        for j in range(H):
            zc = zc + r.conv_w[j:j + 1, col:col + CK] * r.work_s[:, HP - H + j:HP - H + j + TT, :]
        r.hist_s[:, :, col:col + CK] = r.work_s[:, TT:TT + HP, :]
        return zc.reshape(R, CK)

    def chunk(c):
        gz = conv_cols(c * CK)
        uz = conv_cols(FFN_DIM + c * CK)
        r.act_s[:, c * CK:(c + 1) * CK] = (_gelu(gz) * uz).astype(BF16)

    def epilogue():
        y = _mm(r.act_s[...], r.w_down[...]) + x_src[...].reshape(R, D_MODEL)
        if g_final is not None:
            y = _rms(y, g_final[...])
        y_dst[...] = y.reshape(S, TT, D_MODEL)

    return prologue, [functools.partial(chunk, c) for c in range(FFN_DIM // CK)], epilogue


def _ffn_state(r):
    return r.hist_s[:, SUBLANES - FFN_HIST:SUBLANES, :]


def _ffn_kernel(*refs, S, TT, nt, final):
    it = iter(refs)
    x_ref = next(it)
    r = _take(it, FFN_CONSTS)
    gfin_ref = next(it) if final else None
    y_ref = next(it); stout_ref = next(it)
    r.__dict__.update(_take(it, FFN_SCRATCH).__dict__)
    t = pl.program_id(1)

    @pl.when(t == 0)
    def _():
        r.hist_s[...] = jnp.zeros(r.hist_s.shape, F32)

    prologue, chunks, epilogue = _ffn_stages(r, x_ref, y_ref, gfin_ref, S, TT)
    prologue()
    for ch in chunks:
        ch()
    epilogue()

    @pl.when(t == nt - 1)
    def _():
        stout_ref[...] = _ffn_state(r)


def _layer_spec(shape, layer):
    nd = len(shape) - 1
    return pl.BlockSpec((None,) + tuple(shape[1:]), lambda b, t: (layer,) + (0,) * nd,
                        pipeline_mode=pl.Buffered(1))


def _conv_ffn(x, layer, p, g_final, *, S, TT):
    B, T, D = x.shape
    final = g_final is not None
    nb, nt = B // S, T // TT
    consts = [p[n] for n in FFN_CONSTS]
    in_specs = [pl.BlockSpec((S, TT, D), lambda b, t: (b, t, 0))]
    in_specs += [_layer_spec(c.shape, layer) for c in consts]
    args = [x] + consts
    if final:
        in_specs.append(_const_spec(g_final.shape))
        args.append(g_final)
    out_shape = (jax.ShapeDtypeStruct((B, T, D), F32),
                 jax.ShapeDtypeStruct((B, FFN_HIST, 2 * FFN_DIM), F32))
    out_specs = (pl.BlockSpec((S, TT, D), lambda b, t: (b, t, 0)),
                 pl.BlockSpec((S, FFN_HIST, 2 * FFN_DIM), lambda b, t: (b, 0, 0)))
    return pl.pallas_call(
        functools.partial(_ffn_kernel, S=S, TT=TT, nt=nt, final=final),
        grid=(nb, nt), in_specs=in_specs, out_specs=out_specs, out_shape=out_shape,
        scratch_shapes=_ffn_scratch(S, TT),
        compiler_params=pltpu.CompilerParams(dimension_semantics=("arbitrary", "arbitrary"),
                                             vmem_limit_bytes=VMEM_LIMIT_BYTES),
        name="ffn_prompt_final" if final else "ffn_prompt",
    )(*args)


def _ffn_cols_kernel(*refs, B, TT, nc, final):
    it = iter(refs)
    x_ref = next(it); g_ref = next(it); w_ref = next(it); cw_ref = next(it); cb_ref = next(it)
    wdn_ref = next(it)
    gfin_ref = next(it) if final else None
    st_ref = next(it)
    y_ref = next(it); stout_ref = next(it)
    h_s = next(it); gate_s = next(it); act_s = next(it)

    j = pl.program_id(0)
    R = B * TT
    CK = FFN_COL_CHUNK
    H = FFN_HIST

    @pl.when(j == 0)
    def _():
        x = x_ref[...].reshape(R, D_MODEL)
        h_s[...] = _rms(x, g_ref[...]).astype(BF16)

    tpos = lax.broadcasted_iota(jnp.int32, (B, TT, CK), 1)
    z3 = _mm(h_s[...], w_ref[...]).reshape(B, TT, CK)
    stout_ref[...] = z3[:, TT - H:TT, :]
    prev1 = st_ref[:, 1:2, :]
    prev2 = st_ref[:, 0:1, :]
    zm1 = jnp.where(tpos >= 1, pltpu.roll(z3, 1, axis=1), prev1)
    zm2 = jnp.where(tpos >= 2, pltpu.roll(z3, 2, axis=1), jnp.where(tpos == 1, prev1, prev2))
    zc = (cw_ref[2:3, :] * z3 + cw_ref[1:2, :] * zm1 + cw_ref[0:1, :] * zm2 + cb_ref[...]).reshape(R, CK)

    @pl.when(j < nc)
    def _():
        gate_s[j] = _gelu(zc)

    @pl.when(j >= nc)
    def _():
        act_s[j - nc] = (gate_s[j - nc] * zc).astype(BF16)

    @pl.when(j == 2 * nc - 1)
    def _():
        act = jnp.concatenate([act_s[k] for k in range(nc)], axis=-1)
        y = _mm(act, wdn_ref[...]) + x_ref[...].reshape(R, D_MODEL)
        if final:
            y = _rms(y, gfin_ref[...])
        y_ref[...] = y.reshape(B, TT, D_MODEL)


def _conv_ffn_sample(x, state, layer, p, g_final):
    B, T, D = x.shape
    assert T == SUBLANES and FFN_CONV_WIDTH == 3
    final = g_final is not None
    CK = FFN_COL_CHUNK
    nc = FFN_DIM // CK

    def cols(shape):
        nd = len(shape) - 2
        return pl.BlockSpec((None,) + tuple(shape[1:-1]) + (CK,), lambda j: (layer,) + (0,) * nd + (j,))

    in_specs = [pl.BlockSpec((B, T, D), lambda j: (0, 0, 0), pipeline_mode=pl.Buffered(1)),
                pl.BlockSpec((None, 1, D), lambda j: (layer, 0, 0), pipeline_mode=pl.Buffered(1)),
                cols(p['w_up'].shape), cols(p['conv_w'].shape), cols(p['conv_b'].shape),
                pl.BlockSpec((None, FFN_DIM, D), lambda j: (layer, 0, 0), pipeline_mode=pl.Buffered(1))]
    args = [x, p['g'], p['w_up'], p['conv_w'], p['conv_b'], p['w_down']]
    if final:
        in_specs.append(pl.BlockSpec(g_final.shape, lambda j: (0, 0), pipeline_mode=pl.Buffered(1)))
        args.append(g_final)
    in_specs.append(cols(state.shape))
    args.append(state)
    out_shape = (jax.ShapeDtypeStruct((B, T, D), F32),
                 jax.ShapeDtypeStruct((B, FFN_HIST, 2 * FFN_DIM), F32))
    out_specs = (pl.BlockSpec((B, T, D), lambda j: (0, 0, 0)),
                 pl.BlockSpec((B, FFN_HIST, CK), lambda j: (0, 0, j)))
    return pl.pallas_call(
        functools.partial(_ffn_cols_kernel, B=B, TT=T, nc=nc, final=final),
        grid=(2 * nc,), in_specs=in_specs, out_specs=out_specs, out_shape=out_shape,
        scratch_shapes=[pltpu.VMEM((B * T, D), BF16), pltpu.VMEM((nc, B * T, CK), F32),
                        pltpu.VMEM((nc, B * T, CK), BF16)],
        compiler_params=pltpu.CompilerParams(dimension_semantics=("arbitrary",),
                                             vmem_limit_bytes=VMEM_LIMIT_BYTES),
        name="ffn_sample_final" if final else "ffn_sample",
    )(*args)


PROMPT_TILES = dict(even=dict(S=1, TT=512), odd=dict(S=1, TT=1024), ffn=dict(S=1, TT=1024))
SAMPLE_TILES = dict(even=dict(S=16, TT=8), odd=dict(S=32, TT=8))


def _row(v):
    return v.reshape(1, -1)


def _diag_taps(w):
    half = CONV_A_DIM // 2
    w = w[np.array(CONV_A_MXU_TAPS)].reshape(len(CONV_A_MXU_TAPS), 2, 1, half)
    return (jnp.eye(half, dtype=F32)[None, None] * w).astype(BF16)


def _pair_block_diag(w_a, w_x):
    def pairs(w):
        w = w.reshape(LRU_BLOCKS // 2, 2, LRU_BLOCK, LRU_BLOCK)
        z = jnp.zeros_like(w[:, 0])
        top = jnp.concatenate([w[:, 0], z], axis=-1)
        bot = jnp.concatenate([z, w[:, 1]], axis=-1)
        return jnp.concatenate([top, bot], axis=-2)
    return jnp.concatenate([pairs(w_a), pairs(w_x)], axis=-1).astype(BF16)


def kernel(x_prompt, x_sample, state_conv_a, state_ret, state_lru_conv, state_lru_h, state_ffn_conv, norm_mix, norm_ffn, norm_final, w_in_ab, conv_a_w, conv_a_b, ln_a_g, ln_a_b, gn_ret_g, w_out_ab, w_in_c, conv_c_w, conv_c_b, w_lru_a, b_lru_a, w_lru_x, b_lru_x, lru_lambda, w_out_c, w_ffn_up, ffn_conv_w, ffn_conv_b, w_ffn_down):
    pe = dict(gmix=_row(norm_mix[0]), w_in=w_in_ab[0].astype(BF16), conv_w=conv_a_w[0],
              conv_wd=_diag_taps(conv_a_w[0]),
              conv_b=_row(conv_a_b[0]), ln_g=_row(ln_a_g[0]), ln_b=_row(ln_a_b[0]),
              gn_g=_row(gn_ret_g[0]), w_out=w_out_ab[0].astype(BF16))
    po = dict(gmix=_row(norm_mix[1]), w_in=w_in_c[0].astype(BF16), conv_w=conv_c_w[0],
              conv_b=_row(conv_c_b[0]), w_ax=_pair_block_diag(w_lru_a[0], w_lru_x[0]),
              b_a=_row(b_lru_a[0]), b_x=_row(b_lru_x[0]), lam=_row(lru_lambda[0]),
              w_out=w_out_c[0].astype(BF16))
    pf = dict(g=norm_ffn[:, None, :], w_up=w_ffn_up.astype(BF16), conv_w=ffn_conv_w,
              conv_b=ffn_conv_b[:, None, :], w_down=w_ffn_down.astype(BF16))
    g_final = _row(norm_final)

    xp, p_conv_a, p_ret = _even_mixer(x_prompt, None, None, pe, **PROMPT_TILES['even'])
    xp, p_ffn0 = _conv_ffn(xp, 0, pf, None, **PROMPT_TILES['ffn'])
    xp, p_lru_conv, p_lru_h = _odd_mixer(xp, None, None, po, **PROMPT_TILES['odd'])
    y_prompt, p_ffn1 = _conv_ffn(xp, 1, pf, g_final, **PROMPT_TILES['ffn'])

    xs, s_conv_a, s_ret = _even_mixer(x_sample, state_conv_a, state_ret, pe, **SAMPLE_TILES['even'])
    xs, s_ffn0 = _conv_ffn_sample(xs, state_ffn_conv, 0, pf, None)
    xs, s_lru_conv, s_lru_h = _odd_mixer(xs, state_lru_conv, state_lru_h[0][:, None, :], po,
                                         **SAMPLE_TILES['odd'])
    y_sample, s_ffn1 = _conv_ffn_sample(xs, state_ffn_conv, 1, pf, g_final)

    return (y_prompt, y_sample,
            p_conv_a[None], p_ret[None], p_lru_conv[None], p_lru_h[:, 0, :][None],
            jnp.stack([p_ffn0, p_ffn1]),
            s_conv_a[None], s_ret[None], s_lru_conv[None], s_lru_h[:, 0, :][None],
            jnp.stack([s_ffn0, s_ffn1]))
```

```python
import functools
import math
import types

import numpy as np
import jax
import jax.numpy as jnp
from jax import lax
from jax.experimental import pallas as pl
from jax.experimental.pallas import tpu as pltpu

F32 = jnp.float32
BF16 = jnp.bfloat16

D_MODEL = 1024
CONV_A_DIM = 512
CONV_A_WIDTH = 31
CONV_A_HIST = CONV_A_WIDTH - 1
CONV_A_HIST_PAD = 32
CONV_A_ROW_BLOCK = 32
CONV_A_SEQ_BLOCK = 4
LN_EPS = 1e-5
RET_HEADS = 4
RET_HEAD_DIM = 128
RET_DIM = RET_HEADS * RET_HEAD_DIM
RET_CHUNK = 128
ROPE_BASE = 10000.0
GN_EPS = 1e-5
IN_AB_DIM = 2 * CONV_A_DIM + 4 * RET_DIM
Z_SECTION = 512
EVEN_SUB_ROWS = 256
LRU_DIM = 1024
LRU_BLOCKS = 8
LRU_BLOCK = LRU_DIM // LRU_BLOCKS
LRU_CONV_WIDTH = 4
LRU_HIST = LRU_CONV_WIDTH - 1
LRU_C = 8.0
ODD_SUB_ROWS = 256
FFN_DIM = 2816
FFN_CONV_WIDTH = 3
FFN_HIST = FFN_CONV_WIDTH - 1
FFN_COL_CHUNK = 256
RMS_EPS = 1e-6
PAST_LEN = 16384
SUBLANES = 8
SAMPLE_GROUP = RET_CHUNK // SUBLANES

VMEM_LIMIT_BYTES = 56 * 1024 * 1024


def _rms(x, g):
    return x * lax.rsqrt(jnp.mean(x * x, axis=-1, keepdims=True) + RMS_EPS) * g


def _sigmoid(x):
    return 1.0 / (1.0 + jnp.exp(-x))


def _gelu(x):
    c = math.sqrt(2.0 / math.pi)
    return x * (0.5 + 0.5 * jnp.tanh(x * (c + (c * 0.044715) * (x * x))))


def _mm(a, b):
    return jnp.dot(a, b, preferred_element_type=F32)


def _const_spec(shape):
    nd = len(shape)
    return pl.BlockSpec(shape, lambda b, t: (0,) * nd, pipeline_mode=pl.Buffered(1))


def _take(it, names):
    return types.SimpleNamespace(**{n: next(it) for n in names})


EVEN_CONSTS = ('cos', 'sin', 'dmask', 'qd', 'kd', 'cd', 'gmix', 'w_in', 'conv_w', 'conv_wd', 'conv_b',
               'ln_g', 'ln_b', 'gn_g', 'w_out')
EVEN_SCRATCH = ('xp_s', 'xs_s', 'yb_s', 'cv_s', 'ya_s')


def _conv_a_tap(j):
    off = CONV_A_HIST_PAD - CONV_A_HIST + j
    return off % SUBLANES, off - off % SUBLANES


CONV_A_VPU_TAPS = tuple(j for j in range(CONV_A_WIDTH) if _conv_a_tap(j)[0] <= 4)
CONV_A_MXU_TAPS = tuple(j for j in range(CONV_A_WIDTH) if _conv_a_tap(j)[0] > 4)


def _even_scratch(S, TT, sample):
    return [pltpu.VMEM((S, CONV_A_HIST_PAD + TT, CONV_A_DIM), F32),
            pltpu.VMEM((SUBLANES - 1, S, TT + CONV_A_HIST_PAD - SUBLANES, CONV_A_DIM), F32),
            pltpu.VMEM((S * TT, RET_DIM), BF16),
            pltpu.VMEM((S * TT, CONV_A_DIM), F32),
            pltpu.VMEM((S * TT, CONV_A_DIM), BF16)]


def _even_init(r, ret_live, conv_state, ret_state):
    S = r.xp_s.shape[0]
    HP, H = CONV_A_HIST_PAD, CONV_A_HIST
    r.xp_s[:, 0:HP, :] = jnp.zeros((S, HP, CONV_A_DIM), F32)
    if conv_state is not None:
        r.xp_s[:, HP - H:HP, :] = conv_state[...]
        ret_live[...] = ret_state[...]
    else:
        ret_live[...] = jnp.zeros(ret_live.shape, F32)


def _even_tiles(r, x_src, y_dst, ret_live, S, TT, sample):
    HP, H = CONV_A_HIST_PAD, CONV_A_HIST
    o0 = 2 * CONV_A_DIM
    scale = RET_HEAD_DIM ** -0.5
    projected = {}

    def project(s0, ns, g0, nr):
        rows = ns * nr
        x = x_src[s0:s0 + ns, g0:g0 + nr, :].reshape(rows, D_MODEL)
        h = _rms(x, r.gmix[...]).astype(BF16)
        zs = [_mm(h, r.w_in[:, c0:c0 + Z_SECTION]) for c0 in range(0, IN_AB_DIM, Z_SECTION)]
        u = zs[0] * _sigmoid(zs[1])
        r.xp_s[s0:s0 + ns, HP + g0:HP + g0 + nr, :] = u.reshape(ns, nr, CONV_A_DIM)
        projected[(s0, g0)] = zs

    def run(s0, ns, r0, nr, g0):
        rows = ns * nr
        row0 = s0 * TT + r0
        zs = projected[(s0, g0)]
        off = (r0 - g0) * ns

        def zcols(lo, hi):
            k = lo // Z_SECTION
            return zs[k][off:off + rows, lo - k * Z_SECTION:hi - k * Z_SECTION]

        cos = r.cos[row0:row0 + rows, :]
        sin = r.sin[row0:row0 + rows, :]
        gate = zcols(o0 + 3 * RET_DIM, o0 + 4 * RET_DIM)
        o_parts = []
        for hh in range(RET_HEADS):
            lo = hh * RET_HEAD_DIM
            hi = lo + RET_HEAD_DIM
            qh = zcols(o0 + lo, o0 + hi)
            kh = zcols(o0 + RET_DIM + lo, o0 + RET_DIM + hi)
            qc = qh * cos + pltpu.roll(qh, RET_HEAD_DIM // 2, axis=1) * sin
            kc = (kh * cos + pltpu.roll(kh, RET_HEAD_DIM // 2, axis=1) * sin) * scale
            vc = zcols(o0 + 2 * RET_DIM + lo, o0 + 2 * RET_DIM + hi)
            qb = qc.astype(BF16)
            vb = vc.astype(BF16)
            scores = lax.dot_general(qb, kc.astype(BF16), (((1,), (1,)), ((), ())),
                                     preferred_element_type=F32) * r.dmask[hh]
            inner = _mm(scores.astype(BF16), vb)
            if not sample:
                kdv = (kc * r.kd[hh]).astype(BF16)
                st = ret_live[0, hh]
                cross = _mm(qb, st.astype(BF16))
                upd = lax.dot_general(kdv, vb, (((0,), (0,)), ((), ())), preferred_element_type=F32)
                ret_live[0, hh] = st * r.cd[hh] + upd
            else:
                parts = []
                for sq in range(SAMPLE_GROUP):
                    sidx = s0 + sq
                    rs = slice(sq * SUBLANES, (sq + 1) * SUBLANES)
                    st = ret_live[sidx, hh]
                    parts.append(_mm(qc[rs].astype(BF16), st.astype(BF16)))
                    kdv_s = (kc[rs] * r.kd[hh, rs, :]).astype(BF16)
                    upd = lax.dot_general(kdv_s, vc[rs].astype(BF16), (((0,), (0,)), ((), ())),
                                          preferred_element_type=F32)
                    ret_live[sidx, hh] = st * r.cd[hh] + upd
                cross = jnp.concatenate(parts, axis=0)
            o = inner + cross * r.qd[hh]
            mu_o = jnp.mean(o, axis=-1, keepdims=True)
            oc = o - mu_o
            var_o = jnp.mean(oc * oc, axis=-1, keepdims=True)
            o_parts.append(oc * lax.rsqrt(var_o + GN_EPS) * r.gn_g[:, lo:hi])

        yb = jnp.concatenate(o_parts, axis=-1) * (gate * _sigmoid(gate))
        r.yb_s[row0:row0 + rows, :] = yb.astype(BF16)

    def conv_and_project():
        R = S * TT
        L = TT + HP - SUBLANES
        for b in range(1, SUBLANES):
            r.xs_s[b - 1] = r.xp_s[:, b:b + L, :]
        half = CONV_A_DIM // 2
        if sample:
            blocks = [(sb, CONV_A_SEQ_BLOCK, 0, TT) for sb in range(0, S, CONV_A_SEQ_BLOCK)]
        else:
            blocks = [(0, 1, rb, CONV_A_ROW_BLOCK) for rb in range(0, TT, CONV_A_ROW_BLOCK)]
        for sb, nsb, rb, nrb in blocks:
            rowb = sb * TT + rb
            acc = jnp.zeros((nsb, nrb, CONV_A_DIM), F32) + r.conv_b[...]
            for j in CONV_A_VPU_TAPS:
                b, lo = _conv_a_tap(j)
                if b == 0:
                    win = r.xp_s[sb:sb + nsb, lo + rb:lo + rb + nrb, :]
                else:
                    win = r.xs_s[b - 1, sb:sb + nsb, lo + rb:lo + rb + nrb, :]
                acc = acc + r.conv_w[j:j + 1, :] * win
            r.cv_s[rowb:rowb + nsb * nrb, :] = acc.reshape(nsb * nrb, CONV_A_DIM)
        mxu_parts = []
        for p in range(2):
            cs = slice(p * half, (p + 1) * half)
            acc = jnp.zeros((R, half), F32)
            for k, j in enumerate(CONV_A_MXU_TAPS):
                b, lo = _conv_a_tap(j)
                win = r.xs_s[b - 1, :, lo:lo + TT, cs]
                acc = acc + _mm(win.reshape(R, half).astype(BF16), r.conv_wd[k, p])
            mxu_parts.append(acc)
        cv_mxu = jnp.concatenate(mxu_parts, axis=-1)
        for sb, nsb, rb, nrb in blocks:
            rowb = sb * TT + rb
            rows_b = slice(rowb, rowb + nsb * nrb)
            cv = r.cv_s[rows_b, :] + cv_mxu[rows_b, :]
            mu = jnp.mean(cv, axis=-1, keepdims=True)
            cvc = cv - mu
            var = jnp.mean(cvc * cvc, axis=-1, keepdims=True)
            ln = cvc * lax.rsqrt(var + LN_EPS) * r.ln_g[...] + r.ln_b[...]
            r.ya_s[rowb:rowb + nsb * nrb, :] = (ln * _sigmoid(ln)).astype(BF16)
        y = (_mm(r.ya_s[...], r.w_out[0:CONV_A_DIM, :])
             + _mm(r.yb_s[...], r.w_out[CONV_A_DIM:CONV_A_DIM + RET_DIM, :])
             + x_src[...].reshape(R, D_MODEL))
        y_dst[...] = y.reshape(S, TT, D_MODEL)

    assert Z_SECTION == CONV_A_DIM
    stages = []
    if sample:
        for s0 in range(0, S, SAMPLE_GROUP):
            stages.append(functools.partial(project, s0, SAMPLE_GROUP, 0, TT))
            stages.append(functools.partial(run, s0, SAMPLE_GROUP, 0, TT, 0))
    else:
        for g0 in range(0, TT, EVEN_SUB_ROWS):
            stages.append(functools.partial(project, 0, 1, g0, EVEN_SUB_ROWS))
            for r0 in range(g0, g0 + EVEN_SUB_ROWS, RET_CHUNK):
                stages.append(functools.partial(run, 0, 1, r0, RET_CHUNK, g0))
    return stages + [conv_and_project]


def _even_carry(r, TT):
    HP = CONV_A_HIST_PAD
    r.xp_s[:, 0:HP, :] = r.xp_s[:, TT:TT + HP, :]


def _even_conv_state(r):
    return r.xp_s[:, CONV_A_HIST_PAD - CONV_A_HIST:CONV_A_HIST_PAD, :]


def _rope_tables(pos0, T, reps):
    d = RET_HEAD_DIM
    inv_freq = ROPE_BASE ** (-np.arange(0, d, 2, dtype=np.float64) / d)
    ang = (pos0 + np.arange(T, dtype=np.float64))[:, None] * inv_freq[None, :]
    cos = np.cos(ang)
    sin = np.sin(ang)
    cos2 = np.concatenate([cos, cos], axis=-1)
    sin2 = np.concatenate([-sin, sin], axis=-1)
    return (jnp.asarray(np.tile(cos2, (reps, 1)), F32), jnp.asarray(np.tile(sin2, (reps, 1)), F32))


def _decay_tables(c, groups):
    nh = RET_HEADS
    log_gamma = np.log(1.0 - 2.0 ** (-5.0 - np.arange(nh, dtype=np.float64)))
    idx = np.arange(c, dtype=np.float64)
    rel = idx[:, None] - idx[None, :]
    dmask = np.where(rel >= 0, np.exp(np.maximum(rel, 0.0)[None] * log_gamma[:, None, None]), 0.0)
    qd = np.exp((idx + 1.0)[None, :] * log_gamma[:, None])
    kd = np.exp((c - 1.0 - idx)[None, :] * log_gamma[:, None])
    cd = np.exp(c * log_gamma)
    if groups > 1:
        eye = np.eye(groups)
        dmask = np.einsum('gk,hij->hgikj', eye, dmask).reshape(nh, groups * c, groups * c)
        qd = np.tile(qd, (1, groups))
        kd = np.tile(kd, (1, groups))
    n = groups * c
    qd = np.broadcast_to(qd[:, :, None], (nh, n, RET_HEAD_DIM))
    kd = np.broadcast_to(kd[:, :, None], (nh, n, RET_HEAD_DIM))
    cd = np.broadcast_to(cd[:, None, None], (nh, 1, RET_HEAD_DIM))
    return tuple(jnp.asarray(np.ascontiguousarray(a), F32) for a in (dmask, qd, kd, cd))


def _even_const_args(p, pos0, T, reps, sample):
    cos, sin = _rope_tables(pos0, T, reps)
    dmask, qd, kd, cd = _decay_tables(T, SAMPLE_GROUP) if sample else _decay_tables(RET_CHUNK, 1)
    return [cos, sin, dmask, qd, kd, cd, p['gmix'], p['w_in'], p['conv_w'], p['conv_wd'], p['conv_b'],
            p['ln_g'], p['ln_b'], p['gn_g'], p['w_out']]


def _even_kernel(*refs, S, TT, nt, sample):
    it = iter(refs)
    x_ref = next(it)
    r = _take(it, EVEN_CONSTS)
    convst_ref = next(it) if sample else None
    retst_ref = next(it) if sample else None
    y_ref = next(it); convout_ref = next(it); retout_ref = next(it)
    r.__dict__.update(_take(it, EVEN_SCRATCH).__dict__)
    t = pl.program_id(1)

    @pl.when(t == 0)
    def _():
        _even_init(r, retout_ref, convst_ref, retst_ref)

    for tile in _even_tiles(r, x_ref, y_ref, retout_ref, S, TT, sample):
        tile()
    _even_carry(r, TT)

    @pl.when(t == nt - 1)
    def _():
        convout_ref[...] = _even_conv_state(r)


def _even_mixer(x, conv_state, ret_state, p, *, S, TT):
    B, T, D = x.shape
    sample = conv_state is not None
    nb, nt = B // S, T // TT
    R = S * TT
    if sample:
        assert TT == T == SUBLANES and S % SAMPLE_GROUP == 0
        consts = _even_const_args(p, PAST_LEN, T, S, True)
    else:
        assert S == 1 and TT % RET_CHUNK == 0
        consts = _even_const_args(p, 0, T, 1, False)
    in_specs = [pl.BlockSpec((S, TT, D), lambda b, t: (b, t, 0)),
                pl.BlockSpec((R, RET_HEAD_DIM), lambda b, t: (t, 0)),
                pl.BlockSpec((R, RET_HEAD_DIM), lambda b, t: (t, 0))]
    in_specs += [_const_spec(c.shape) for c in consts[2:]]
    args = [x] + consts
    if sample:
        in_specs += [pl.BlockSpec((None, S, CONV_A_HIST, CONV_A_DIM), lambda b, t: (0, b, 0, 0)),
                     pl.BlockSpec((None, S, RET_HEADS, RET_HEAD_DIM, RET_HEAD_DIM),
                                  lambda b, t: (0, b, 0, 0, 0))]
        args += [conv_state, ret_state]
    out_shape = (jax.ShapeDtypeStruct((B, T, D), F32),
                 jax.ShapeDtypeStruct((B, CONV_A_HIST, CONV_A_DIM), F32),
                 jax.ShapeDtypeStruct((B, RET_HEADS, RET_HEAD_DIM, RET_HEAD_DIM), F32))
    out_specs = (pl.BlockSpec((S, TT, D), lambda b, t: (b, t, 0)),
                 pl.BlockSpec((S, CONV_A_HIST, CONV_A_DIM), lambda b, t: (b, 0, 0)),
                 pl.BlockSpec((S, RET_HEADS, RET_HEAD_DIM, RET_HEAD_DIM), lambda b, t: (b, 0, 0, 0)))
    return pl.pallas_call(
        functools.partial(_even_kernel, S=S, TT=TT, nt=nt, sample=sample),
        grid=(nb, nt), in_specs=in_specs, out_specs=out_specs, out_shape=out_shape,
        scratch_shapes=_even_scratch(S, TT, sample),
        compiler_params=pltpu.CompilerParams(dimension_semantics=("arbitrary", "arbitrary"),
                                             vmem_limit_bytes=VMEM_LIMIT_BYTES),
        name="even_mixer_sample" if sample else "even_mixer_prompt",
    )(*args)


ODD_CONSTS = ('gmix', 'w_in', 'conv_w', 'conv_b', 'w_ax', 'b_a', 'b_x', 'lam', 'w_out')
ODD_SCRATCH = ('xp_s', 'gate_s', 'xc_s', 'r_s', 'i_s', 'act_s')


def _odd_scratch(S, TT):
    R = S * TT
    return ([pltpu.VMEM((S, SUBLANES + TT, LRU_DIM), F32)]
            + [pltpu.VMEM((R, LRU_DIM), F32) for _ in range(4)]
            + [pltpu.VMEM((R, LRU_DIM), BF16)])


def _odd_init(r, h_live, conv_state, h_state):
    S = r.xp_s.shape[0]
    HP, H = SUBLANES, LRU_HIST
    r.xp_s[:, 0:HP, :] = jnp.zeros((S, HP, LRU_DIM), F32)
    if conv_state is not None:
        r.xp_s[:, HP - H:HP, :] = conv_state[...]
        h_live[...] = h_state[...]
    else:
        h_live[...] = jnp.zeros(h_live.shape, F32)


def _odd_tiles(r, x_src, y_dst, h_live, S, TT, sample):
    HP, H = SUBLANES, LRU_HIST
    if sample:
        tiles = [(0, S, 0, TT)]
    else:
        tiles = [(0, 1, r0, min(TT, ODD_SUB_ROWS)) for r0 in range(0, TT, ODD_SUB_ROWS)]
    pair = 2 * LRU_BLOCK

    def scan_group(rows, h_prev, decay, sub):
        rg = _sigmoid(r.r_s[rows, :])
        ig = _sigmoid(r.i_s[rows, :])
        a = jnp.exp2(decay * rg)
        om = jnp.maximum(1.0 - a * a, 0.0)
        root = jnp.where(om > 0.0, om * lax.rsqrt(om), 0.0)
        b = root * (ig * r.xc_s[rows, :])
        for sh in (1, 2, 4):
            keep = sub >= sh
            a_sh = jnp.where(keep, pltpu.roll(a, sh, axis=0), 1.0)
            b_sh = jnp.where(keep, pltpu.roll(b, sh, axis=0), 0.0)
            b = a * b_sh + b
            a = a * a_sh
        return a * h_prev + b

    def run(s0, ns, r0, nr):
        rows = ns * nr
        row0 = s0 * TT + r0
        nlam = -r.lam[...]
        softplus = jnp.maximum(nlam, 0.0) + jnp.log(1.0 + jnp.exp(-jnp.abs(nlam)))
        decay = jnp.broadcast_to((-LRU_C * math.log2(math.e)) * softplus, (SUBLANES, LRU_DIM))
        sub = lax.broadcasted_iota(jnp.int32, (SUBLANES, LRU_DIM), 0)
        x = x_src[s0:s0 + ns, r0:r0 + nr, :].reshape(rows, D_MODEL)
        h = _rms(x, r.gmix[...]).astype(BF16)
        r.gate_s[row0:row0 + rows, :] = _mm(h, r.w_in[:, 0:LRU_DIM])
        rec3 = _mm(h, r.w_in[:, LRU_DIM:2 * LRU_DIM]).reshape(ns, nr, LRU_DIM)
        r.xp_s[s0:s0 + ns, HP + r0:HP + r0 + nr, :] = rec3
        acc = r.conv_w[H:H + 1, :] * rec3 + r.conv_b[...]
        for j in range(H):
            lo = HP - H + j + r0
            acc = acc + r.conv_w[j:j + 1, :] * r.xp_s[s0:s0 + ns, lo:lo + nr, :]
        xc = acc.reshape(rows, LRU_DIM)
        r.xc_s[row0:row0 + rows, :] = xc
        xcb = xc.astype(BF16)
        for p in range(LRU_BLOCKS // 2):
            cs = slice(pair * p, pair * (p + 1))
            ri = _mm(xcb[:, cs], r.w_ax[p])
            r.r_s[row0:row0 + rows, cs] = ri[:, 0:pair] + r.b_a[:, cs]
            r.i_s[row0:row0 + rows, cs] = ri[:, pair:2 * pair] + r.b_x[:, cs]
        h_prev = None if sample else jnp.broadcast_to(h_live[0], (SUBLANES, LRU_DIM))
        for g0 in range(row0, row0 + rows, 2 * SUBLANES):
            parts = []
            for k in range(2):
                ga = g0 + k * SUBLANES
                if sample:
                    h_prev = jnp.broadcast_to(h_live[ga // TT], (SUBLANES, LRU_DIM))
                hs = scan_group(slice(ga, ga + SUBLANES), h_prev, decay, sub)
                h_last = hs[SUBLANES - 1:SUBLANES, :]
                if sample:
                    h_live[ga // TT] = h_last
                h_prev = jnp.broadcast_to(h_last, (SUBLANES, LRU_DIM))
                parts.append(hs)
            rows2 = slice(g0, g0 + 2 * SUBLANES)
            r.act_s[rows2, :] = (jnp.concatenate(parts, axis=0) * _gelu(r.gate_s[rows2, :])).astype(BF16)
        if not sample:
            h_live[0] = h_prev[0:1, :]
        y = _mm(r.act_s[row0:row0 + rows, :], r.w_out[...]) + x
        y_dst[s0:s0 + ns, r0:r0 + nr, :] = y.reshape(ns, nr, D_MODEL)

    return [functools.partial(run, *tl) for tl in tiles]


def _odd_carry(r, TT):
    HP = SUBLANES
    r.xp_s[:, 0:HP, :] = r.xp_s[:, TT:TT + HP, :]


def _odd_conv_state(r):
    return r.xp_s[:, SUBLANES - LRU_HIST:SUBLANES, :]


def _odd_const_args(p):
    return [p[n] for n in ODD_CONSTS]


def _odd_kernel(*refs, S, TT, nt, sample):
    it = iter(refs)
    x_ref = next(it)
    r = _take(it, ODD_CONSTS)
    convst_ref = next(it) if sample else None
    hst_ref = next(it) if sample else None
    y_ref = next(it); convout_ref = next(it); hout_ref = next(it)
    r.__dict__.update(_take(it, ODD_SCRATCH).__dict__)
    t = pl.program_id(1)

    @pl.when(t == 0)
    def _():
        _odd_init(r, hout_ref, convst_ref, hst_ref)

    for tile in _odd_tiles(r, x_ref, y_ref, hout_ref, S, TT, sample):
        tile()
    _odd_carry(r, TT)

    @pl.when(t == nt - 1)
    def _():
        convout_ref[...] = _odd_conv_state(r)


def _odd_mixer(x, conv_state, h_state, p, *, S, TT):
    B, T, D = x.shape
    sample = conv_state is not None
    nb, nt = B // S, T // TT
    consts = _odd_const_args(p)
    in_specs = [pl.BlockSpec((S, TT, D), lambda b, t: (b, t, 0))]
    in_specs += [_const_spec(c.shape) for c in consts]
    args = [x] + consts
    if sample:
        in_specs += [pl.BlockSpec((None, S, LRU_HIST, LRU_DIM), lambda b, t: (0, b, 0, 0)),
                     pl.BlockSpec((S, 1, LRU_DIM), lambda b, t: (b, 0, 0))]
        args += [conv_state, h_state]
    out_shape = (jax.ShapeDtypeStruct((B, T, D), F32),
                 jax.ShapeDtypeStruct((B, LRU_HIST, LRU_DIM), F32),
                 jax.ShapeDtypeStruct((B, 1, LRU_DIM), F32))
    out_specs = (pl.BlockSpec((S, TT, D), lambda b, t: (b, t, 0)),
                 pl.BlockSpec((S, LRU_HIST, LRU_DIM), lambda b, t: (b, 0, 0)),
                 pl.BlockSpec((S, 1, LRU_DIM), lambda b, t: (b, 0, 0)))
    return pl.pallas_call(
        functools.partial(_odd_kernel, S=S, TT=TT, nt=nt, sample=sample),
        grid=(nb, nt), in_specs=in_specs, out_specs=out_specs, out_shape=out_shape,
        scratch_shapes=_odd_scratch(S, TT),
        compiler_params=pltpu.CompilerParams(dimension_semantics=("arbitrary", "arbitrary"),
                                             vmem_limit_bytes=VMEM_LIMIT_BYTES),
        name="odd_mixer_sample" if sample else "odd_mixer_prompt",
    )(*args)


FFN_CONSTS = ('g', 'w_up', 'conv_w', 'conv_b', 'w_down')
FFN_SCRATCH = ('h_s', 'act_s', 'hist_s', 'work_s')


def _ffn_scratch(S, TT):
    R = S * TT
    return [pltpu.VMEM((R, D_MODEL), BF16),
            pltpu.VMEM((R, FFN_DIM), BF16),
            pltpu.VMEM((S, SUBLANES, 2 * FFN_DIM), F32),
            pltpu.VMEM((S, SUBLANES + TT, FFN_COL_CHUNK), F32)]


def _ffn_stages(r, x_src, y_dst, g_final, S, TT):
    R = S * TT
    HP, H, CK = SUBLANES, FFN_HIST, FFN_COL_CHUNK

    def prologue():
        x = x_src[...].reshape(R, D_MODEL)
        r.h_s[...] = _rms(x, r.g[...]).astype(BF16)

    def conv_cols(col):
        z3 = _mm(r.h_s[...], r.w_up[:, col:col + CK]).reshape(S, TT, CK)
        r.work_s[:, 0:HP, :] = r.hist_s[:, :, col:col + CK]
        r.work_s[:, HP:HP + TT, :] = z3
        zc = r.conv_w[H:H + 1, col:col + CK] * z3 + r.conv_b[:, col:col + CK]
        for j in range(H):
            zc = zc + r.conv_w[j:j + 1, col:col + CK] * r.work_s[:, HP - H + j:HP - H + j + TT, :]
        r.hist_s[:, :, col:col + CK] = r.work_s[:, TT:TT + HP, :]
        return zc.reshape(R, CK)

    def chunk(c):
        gz = conv_cols(c * CK)
        uz = conv_cols(FFN_DIM + c * CK)
        r.act_s[:, c * CK:(c + 1) * CK] = (_gelu(gz) * uz).astype(BF16)

    def epilogue():
        y = _mm(r.act_s[...], r.w_down[...]) + x_src[...].reshape(R, D_MODEL)
        if g_final is not None:
            y = _rms(y, g_final[...])
        y_dst[...] = y.reshape(S, TT, D_MODEL)

    return prologue, [functools.partial(chunk, c) for c in range(FFN_DIM // CK)], epilogue


def _ffn_state(r):
    return r.hist_s[:, SUBLANES - FFN_HIST:SUBLANES, :]


def _ffn_kernel(*refs, S, TT, nt, final):
    it = iter(refs)
    x_ref = next(it)
    r = _take(it, FFN_CONSTS)
    gfin_ref = next(it) if final else None
    y_ref = next(it); stout_ref = next(it)
    r.__dict__.update(_take(it, FFN_SCRATCH).__dict__)
    t = pl.program_id(1)

    @pl.when(t == 0)
    def _():
        r.hist_s[...] = jnp.zeros(r.hist_s.shape, F32)

    prologue, chunks, epilogue = _ffn_stages(r, x_ref, y_ref, gfin_ref, S, TT)
    prologue()
    for ch in chunks:
        ch()
    epilogue()

    @pl.when(t == nt - 1)
    def _():
        stout_ref[...] = _ffn_state(r)


def _layer_spec(shape, layer):
    nd = len(shape) - 1
    return pl.BlockSpec((None,) + tuple(shape[1:]), lambda b, t: (layer,) + (0,) * nd,
                        pipeline_mode=pl.Buffered(1))


def _conv_ffn(x, layer, p, g_final, *, S, TT):
    B, T, D = x.shape
    final = g_final is not None
    nb, nt = B // S, T // TT
    consts = [p[n] for n in FFN_CONSTS]
    in_specs = [pl.BlockSpec((S, TT, D), lambda b, t: (b, t, 0))]
    in_specs += [_layer_spec(c.shape, layer) for c in consts]
    args = [x] + consts
    if final:
        in_specs.append(_const_spec(g_final.shape))
        args.append(g_final)
    out_shape = (jax.ShapeDtypeStruct((B, T, D), F32),
                 jax.ShapeDtypeStruct((B, FFN_HIST, 2 * FFN_DIM), F32))
    out_specs = (pl.BlockSpec((S, TT, D), lambda b, t: (b, t, 0)),
                 pl.BlockSpec((S, FFN_HIST, 2 * FFN_DIM), lambda b, t: (b, 0, 0)))
    return pl.pallas_call(
        functools.partial(_ffn_kernel, S=S, TT=TT, nt=nt, final=final),
        grid=(nb, nt), in_specs=in_specs, out_specs=out_specs, out_shape=out_shape,
        scratch_shapes=_ffn_scratch(S, TT),
        compiler_params=pltpu.CompilerParams(dimension_semantics=("arbitrary", "arbitrary"),
                                             vmem_limit_bytes=VMEM_LIMIT_BYTES),
        name="ffn_prompt_final" if final else "ffn_prompt",
    )(*args)


def _ffn_cols_kernel(*refs, B, TT, nc, final):
    it = iter(refs)
    x_ref = next(it); g_ref = next(it); w_ref = next(it); cw_ref = next(it); cb_ref = next(it)
    wdn_ref = next(it)
    gfin_ref = next(it) if final else None
    st_ref = next(it)
    y_ref = next(it); stout_ref = next(it)
    h_s = next(it); gate_s = next(it); act_s = next(it)

    j = pl.program_id(0)
    R = B * TT
    CK = FFN_COL_CHUNK
    H = FFN_HIST

    @pl.when(j == 0)
    def _():
        x = x_ref[...].reshape(R, D_MODEL)
        h_s[...] = _rms(x, g_ref[...]).astype(BF16)

    tpos = lax.broadcasted_iota(jnp.int32, (B, TT, CK), 1)
    z3 = _mm(h_s[...], w_ref[...]).reshape(B, TT, CK)
    stout_ref[...] = z3[:, TT - H:TT, :]
    prev1 = st_ref[:, 1:2, :]
    prev2 = st_ref[:, 0:1, :]
    zm1 = jnp.where(tpos >= 1, pltpu.roll(z3, 1, axis=1), prev1)
    zm2 = jnp.where(tpos >= 2, pltpu.roll(z3, 2, axis=1), jnp.where(tpos == 1, prev1, prev2))
    zc = (cw_ref[2:3, :] * z3 + cw_ref[1:2, :] * zm1 + cw_ref[0:1, :] * zm2 + cb_ref[...]).reshape(R, CK)

    @pl.when(j < nc)
    def _():
        gate_s[j] = _gelu(zc)

    @pl.when(j >= nc)
    def _():
        act_s[j - nc] = (gate_s[j - nc] * zc).astype(BF16)

    @pl.when(j == 2 * nc - 1)
    def _():
        act = jnp.concatenate([act_s[k] for k in range(nc)], axis=-1)
        y = _mm(act, wdn_ref[...]) + x_ref[...].reshape(R, D_MODEL)
        if final:
            y = _rms(y, gfin_ref[...])
        y_ref[...] = y.reshape(B, TT, D_MODEL)


def _conv_ffn_sample(x, state, layer, p, g_final):
    B, T, D = x.shape
    assert T == SUBLANES and FFN_CONV_WIDTH == 3
    final = g_final is not None
    CK = FFN_COL_CHUNK
    nc = FFN_DIM // CK

    def cols(shape):
        nd = len(shape) - 2
        return pl.BlockSpec((None,) + tuple(shape[1:-1]) + (CK,), lambda j: (layer,) + (0,) * nd + (j,))

    in_specs = [pl.BlockSpec((B, T, D), lambda j: (0, 0, 0), pipeline_mode=pl.Buffered(1)),
                pl.BlockSpec((None, 1, D), lambda j: (layer, 0, 0), pipeline_mode=pl.Buffered(1)),
                cols(p['w_up'].shape), cols(p['conv_w'].shape), cols(p['conv_b'].shape),
                pl.BlockSpec((None, FFN_DIM, D), lambda j: (layer, 0, 0), pipeline_mode=pl.Buffered(1))]
    args = [x, p['g'], p['w_up'], p['conv_w'], p['conv_b'], p['w_down']]
    if final:
        in_specs.append(pl.BlockSpec(g_final.shape, lambda j: (0, 0), pipeline_mode=pl.Buffered(1)))
        args.append(g_final)
    in_specs.append(cols(state.shape))
    args.append(state)
    out_shape = (jax.ShapeDtypeStruct((B, T, D), F32),
                 jax.ShapeDtypeStruct((B, FFN_HIST, 2 * FFN_DIM), F32))
    out_specs = (pl.BlockSpec((B, T, D), lambda j: (0, 0, 0)),
                 pl.BlockSpec((B, FFN_HIST, CK), lambda j: (0, 0, j)))
    return pl.pallas_call(
        functools.partial(_ffn_cols_kernel, B=B, TT=T, nc=nc, final=final),
        grid=(2 * nc,), in_specs=in_specs, out_specs=out_specs, out_shape=out_shape,
        scratch_shapes=[pltpu.VMEM((B * T, D), BF16), pltpu.VMEM((nc, B * T, CK), F32),
                        pltpu.VMEM((nc, B * T, CK), BF16)],
        compiler_params=pltpu.CompilerParams(dimension_semantics=("arbitrary",),
                                             vmem_limit_bytes=VMEM_LIMIT_BYTES),
        name="ffn_sample_final" if final else "ffn_sample",
    )(*args)


PROMPT_TILES = dict(even=dict(S=1, TT=512), odd=dict(S=1, TT=1024), ffn=dict(S=1, TT=1024))
SAMPLE_TILES = dict(even=dict(S=16, TT=8), odd=dict(S=32, TT=8))


def _row(v):
    return v.reshape(1, -1)


def _diag_taps(w):
    half = CONV_A_DIM // 2
    w = w[np.array(CONV_A_MXU_TAPS)].reshape(len(CONV_A_MXU_TAPS), 2, 1, half)
    return (jnp.eye(half, dtype=F32)[None, None] * w).astype(BF16)


def _pair_block_diag(w_a, w_x):
    def pairs(w):
        w = w.reshape(LRU_BLOCKS // 2, 2, LRU_BLOCK, LRU_BLOCK)
        z = jnp.zeros_like(w[:, 0])
        top = jnp.concatenate([w[:, 0], z], axis=-1)
        bot = jnp.concatenate([z, w[:, 1]], axis=-1)
        return jnp.concatenate([top, bot], axis=-2)
    return jnp.concatenate([pairs(w_a), pairs(w_x)], axis=-1).astype(BF16)


def kernel(x_prompt, x_sample, state_conv_a, state_ret, state_lru_conv, state_lru_h, state_ffn_conv, norm_mix, norm_ffn, norm_final, w_in_ab, conv_a_w, conv_a_b, ln_a_g, ln_a_b, gn_ret_g, w_out_ab, w_in_c, conv_c_w, conv_c_b, w_lru_a, b_lru_a, w_lru_x, b_lru_x, lru_lambda, w_out_c, w_ffn_up, ffn_conv_w, ffn_conv_b, w_ffn_down):
    pe = dict(gmix=_row(norm_mix[0]), w_in=w_in_ab[0].astype(BF16), conv_w=conv_a_w[0],
              conv_wd=_diag_taps(conv_a_w[0]),
              conv_b=_row(conv_a_b[0]), ln_g=_row(ln_a_g[0]), ln_b=_row(ln_a_b[0]),
              gn_g=_row(gn_ret_g[0]), w_out=w_out_ab[0].astype(BF16))
    po = dict(gmix=_row(norm_mix[1]), w_in=w_in_c[0].astype(BF16), conv_w=conv_c_w[0],
              conv_b=_row(conv_c_b[0]), w_ax=_pair_block_diag(w_lru_a[0], w_lru_x[0]),
              b_a=_row(b_lru_a[0]), b_x=_row(b_lru_x[0]), lam=_row(lru_lambda[0]),
              w_out=w_out_c[0].astype(BF16))
    pf = dict(g=norm_ffn[:, None, :], w_up=w_ffn_up.astype(BF16), conv_w=ffn_conv_w,
              conv_b=ffn_conv_b[:, None, :], w_down=w_ffn_down.astype(BF16))
    g_final = _row(norm_final)

    xp, p_conv_a, p_ret = _even_mixer(x_prompt, None, None, pe, **PROMPT_TILES['even'])
    xp, p_ffn0 = _conv_ffn(xp, 0, pf, None, **PROMPT_TILES['ffn'])
    xp, p_lru_conv, p_lru_h = _odd_mixer(xp, None, None, po, **PROMPT_TILES['odd'])
    y_prompt, p_ffn1 = _conv_ffn(xp, 1, pf, g_final, **PROMPT_TILES['ffn'])

    xs, s_conv_a, s_ret = _even_mixer(x_sample, state_conv_a, state_ret, pe, **SAMPLE_TILES['even'])
    xs, s_ffn0 = _conv_ffn_sample(xs, state_ffn_conv, 0, pf, None)
    xs, s_lru_conv, s_lru_h = _odd_mixer(xs, state_lru_conv, state_lru_h[0][:, None, :], po,
                                         **SAMPLE_TILES['odd'])
    y_sample, s_ffn1 = _conv_ffn_sample(xs, state_ffn_conv, 1, pf, g_final)

    return (y_prompt, y_sample,
            p_conv_a[None], p_ret[None], p_lru_conv[None], p_lru_h[:, 0, :][None],
            jnp.stack([p_ffn0, p_ffn1]),
            s_conv_a[None], s_ret[None], s_lru_conv[None], s_lru_h[:, 0, :][None],
            jnp.stack([s_ffn0, s_ffn1]))
```

```python
import functools
import math
import types

import numpy as np
import jax
import jax.numpy as jnp
from jax import lax
from jax.experimental import pallas as pl
from jax.experimental.pallas import tpu as pltpu

F32 = jnp.float32
BF16 = jnp.bfloat16

D_MODEL = 1024
CONV_A_DIM = 512
CONV_A_WIDTH = 31
CONV_A_HIST = CONV_A_WIDTH - 1
CONV_A_HIST_PAD = 32
CONV_A_ROW_BLOCK = 32
CONV_A_SEQ_BLOCK = 4
LN_EPS = 1e-5
RET_HEADS = 4
RET_HEAD_DIM = 128
RET_DIM = RET_HEADS * RET_HEAD_DIM
RET_CHUNK = 128
ROPE_BASE = 10000.0
GN_EPS = 1e-5
IN_AB_DIM = 2 * CONV_A_DIM + 4 * RET_DIM
Z_SECTION = 512
EVEN_SUB_ROWS = 256
LRU_DIM = 1024
LRU_BLOCKS = 8
LRU_BLOCK = LRU_DIM // LRU_BLOCKS
LRU_CONV_WIDTH = 4
LRU_HIST = LRU_CONV_WIDTH - 1
LRU_C = 8.0
ODD_SUB_ROWS = 256
FFN_DIM = 2816
FFN_CONV_WIDTH = 3
FFN_HIST = FFN_CONV_WIDTH - 1
FFN_COL_CHUNK = 256
RMS_EPS = 1e-6
PAST_LEN = 16384
SUBLANES = 8
SAMPLE_GROUP = RET_CHUNK // SUBLANES

VMEM_LIMIT_BYTES = 56 * 1024 * 1024


def _rms(x, g):
    return x * lax.rsqrt(jnp.mean(x * x, axis=-1, keepdims=True) + RMS_EPS) * g


def _sigmoid(x):
    return 1.0 / (1.0 + jnp.exp(-x))


def _gelu(x):
    c = math.sqrt(2.0 / math.pi)
    return x * (0.5 + 0.5 * jnp.tanh(x * (c + (c * 0.044715) * (x * x))))


def _mm(a, b):
    return jnp.dot(a, b, preferred_element_type=F32)


def _const_spec(shape):
    nd = len(shape)
    return pl.BlockSpec(shape, lambda b, t: (0,) * nd, pipeline_mode=pl.Buffered(1))


def _take(it, names):
    return types.SimpleNamespace(**{n: next(it) for n in names})


EVEN_CONSTS = ('cos', 'sin', 'dmask', 'qd', 'kd', 'cd', 'gmix', 'w_in', 'conv_w', 'conv_wd', 'conv_b',
               'ln_g', 'ln_b', 'gn_g', 'w_out')
EVEN_SCRATCH = ('xp_s', 'xs_s', 'yb_s', 'cv_s', 'ya_s')


def _conv_a_tap(j):
    off = CONV_A_HIST_PAD - CONV_A_HIST + j
    return off % SUBLANES, off - off % SUBLANES


CONV_A_VPU_TAPS = tuple(j for j in range(CONV_A_WIDTH) if _conv_a_tap(j)[0] <= 5)
CONV_A_MXU_TAPS = tuple(j for j in range(CONV_A_WIDTH) if _conv_a_tap(j)[0] > 5)


def _even_scratch(S, TT, sample):
    return [pltpu.VMEM((S, CONV_A_HIST_PAD + TT, CONV_A_DIM), F32),
            pltpu.VMEM((SUBLANES - 1, S, TT + CONV_A_HIST_PAD - SUBLANES, CONV_A_DIM), F32),
            pltpu.VMEM((S * TT, RET_DIM), BF16),
            pltpu.VMEM((S * TT, CONV_A_DIM), F32),
            pltpu.VMEM((S * TT, CONV_A_DIM), BF16)]


def _even_init(r, ret_live, conv_state, ret_state):
    S = r.xp_s.shape[0]
    HP, H = CONV_A_HIST_PAD, CONV_A_HIST
    r.xp_s[:, 0:HP, :] = jnp.zeros((S, HP, CONV_A_DIM), F32)
    if conv_state is not None:
        r.xp_s[:, HP - H:HP, :] = conv_state[...]
        ret_live[...] = ret_state[...]
    else:
        ret_live[...] = jnp.zeros(ret_live.shape, F32)


def _even_tiles(r, x_src, y_dst, ret_live, S, TT, sample):
    HP, H = CONV_A_HIST_PAD, CONV_A_HIST
    o0 = 2 * CONV_A_DIM
    scale = RET_HEAD_DIM ** -0.5
    projected = {}

    def project(s0, ns, g0, nr):
        rows = ns * nr
        x = x_src[s0:s0 + ns, g0:g0 + nr, :].reshape(rows, D_MODEL)
        h = _rms(x, r.gmix[...]).astype(BF16)
        zs = [_mm(h, r.w_in[:, c0:c0 + Z_SECTION]) for c0 in range(0, IN_AB_DIM, Z_SECTION)]
        u = zs[0] * _sigmoid(zs[1])
        r.xp_s[s0:s0 + ns, HP + g0:HP + g0 + nr, :] = u.reshape(ns, nr, CONV_A_DIM)
        projected[(s0, g0)] = zs

    def run(s0, ns, r0, nr, g0):
        rows = ns * nr
        row0 = s0 * TT + r0
        zs = projected[(s0, g0)]
        off = (r0 - g0) * ns

        def zcols(lo, hi):
            k = lo // Z_SECTION
            return zs[k][off:off + rows, lo - k * Z_SECTION:hi - k * Z_SECTION]

        cos = r.cos[row0:row0 + rows, :]
        sin = r.sin[row0:row0 + rows, :]
        gate = zcols(o0 + 3 * RET_DIM, o0 + 4 * RET_DIM)
        o_parts = []
        for hh in range(RET_HEADS):
            lo = hh * RET_HEAD_DIM
            hi = lo + RET_HEAD_DIM
            qh = zcols(o0 + lo, o0 + hi)
            kh = zcols(o0 + RET_DIM + lo, o0 + RET_DIM + hi)
            qc = qh * cos + pltpu.roll(qh, RET_HEAD_DIM // 2, axis=1) * sin
            kc = (kh * cos + pltpu.roll(kh, RET_HEAD_DIM // 2, axis=1) * sin) * scale
            vc = zcols(o0 + 2 * RET_DIM + lo, o0 + 2 * RET_DIM + hi)
            qb = qc.astype(BF16)
            vb = vc.astype(BF16)
            scores = lax.dot_general(qb, kc.astype(BF16), (((1,), (1,)), ((), ())),
                                     preferred_element_type=F32) * r.dmask[hh]
            inner = _mm(scores.astype(BF16), vb)
            if not sample:
                kdv = (kc * r.kd[hh]).astype(BF16)
                st = ret_live[0, hh]
                cross = _mm(qb, st.astype(BF16))
                upd = lax.dot_general(kdv, vb, (((0,), (0,)), ((), ())), preferred_element_type=F32)
                ret_live[0, hh] = st * r.cd[hh] + upd
            else:
                parts = []
                for sq in range(SAMPLE_GROUP):
                    sidx = s0 + sq
                    rs = slice(sq * SUBLANES, (sq + 1) * SUBLANES)
                    st = ret_live[sidx, hh]
                    parts.append(_mm(qc[rs].astype(BF16), st.astype(BF16)))
                    kdv_s = (kc[rs] * r.kd[hh, rs, :]).astype(BF16)
                    upd = lax.dot_general(kdv_s, vc[rs].astype(BF16), (((0,), (0,)), ((), ())),
                                          preferred_element_type=F32)
                    ret_live[sidx, hh] = st * r.cd[hh] + upd
                cross = jnp.concatenate(parts, axis=0)
            o = inner + cross * r.qd[hh]
            mu_o = jnp.mean(o, axis=-1, keepdims=True)
            oc = o - mu_o
            var_o = jnp.mean(oc * oc, axis=-1, keepdims=True)
            o_parts.append(oc * lax.rsqrt(var_o + GN_EPS) * r.gn_g[:, lo:hi])

        yb = jnp.concatenate(o_parts, axis=-1) * (gate * _sigmoid(gate))
        r.yb_s[row0:row0 + rows, :] = yb.astype(BF16)

    def conv_and_project():
        R = S * TT
        L = TT + HP - SUBLANES
        for b in range(1, SUBLANES):
            r.xs_s[b - 1] = r.xp_s[:, b:b + L, :]
        half = CONV_A_DIM // 2
        if sample:
            blocks = [(sb, CONV_A_SEQ_BLOCK, 0, TT) for sb in range(0, S, CONV_A_SEQ_BLOCK)]
        else:
            blocks = [(0, 1, rb, CONV_A_ROW_BLOCK) for rb in range(0, TT, CONV_A_ROW_BLOCK)]
        for sb, nsb, rb, nrb in blocks:
            rowb = sb * TT + rb
            acc = jnp.zeros((nsb, nrb, CONV_A_DIM), F32) + r.conv_b[...]
            for j in CONV_A_VPU_TAPS:
                b, lo = _conv_a_tap(j)
                if b == 0:
                    win = r.xp_s[sb:sb + nsb, lo + rb:lo + rb + nrb, :]
                else:
                    win = r.xs_s[b - 1, sb:sb + nsb, lo + rb:lo + rb + nrb, :]
                acc = acc + r.conv_w[j:j + 1, :] * win
            r.cv_s[rowb:rowb + nsb * nrb, :] = acc.reshape(nsb * nrb, CONV_A_DIM)
        mxu_parts = []
        for p in range(2):
            cs = slice(p * half, (p + 1) * half)
            acc = jnp.zeros((R, half), F32)
            for k, j in enumerate(CONV_A_MXU_TAPS):
                b, lo = _conv_a_tap(j)
                win = r.xs_s[b - 1, :, lo:lo + TT, cs]
                acc = acc + _mm(win.reshape(R, half).astype(BF16), r.conv_wd[k, p])
            mxu_parts.append(acc)
        cv_mxu = jnp.concatenate(mxu_parts, axis=-1)
        for sb, nsb, rb, nrb in blocks:
            rowb = sb * TT + rb
            rows_b = slice(rowb, rowb + nsb * nrb)
            cv = r.cv_s[rows_b, :] + cv_mxu[rows_b, :]
            mu = jnp.mean(cv, axis=-1, keepdims=True)
            cvc = cv - mu
            var = jnp.mean(cvc * cvc, axis=-1, keepdims=True)
            ln = cvc * lax.rsqrt(var + LN_EPS) * r.ln_g[...] + r.ln_b[...]
            r.ya_s[rowb:rowb + nsb * nrb, :] = (ln * _sigmoid(ln)).astype(BF16)
        y = (_mm(r.ya_s[...], r.w_out[0:CONV_A_DIM, :])
             + _mm(r.yb_s[...], r.w_out[CONV_A_DIM:CONV_A_DIM + RET_DIM, :])
             + x_src[...].reshape(R, D_MODEL))
        y_dst[...] = y.reshape(S, TT, D_MODEL)

    assert Z_SECTION == CONV_A_DIM
    stages = []
    if sample:
        for s0 in range(0, S, SAMPLE_GROUP):
            stages.append(functools.partial(project, s0, SAMPLE_GROUP, 0, TT))
            stages.append(functools.partial(run, s0, SAMPLE_GROUP, 0, TT, 0))
    else:
        for g0 in range(0, TT, EVEN_SUB_ROWS):
            stages.append(functools.partial(project, 0, 1, g0, EVEN_SUB_ROWS))
            for r0 in range(g0, g0 + EVEN_SUB_ROWS, RET_CHUNK):
                stages.append(functools.partial(run, 0, 1, r0, RET_CHUNK, g0))
    return stages + [conv_and_project]


def _even_carry(r, TT):
    HP = CONV_A_HIST_PAD
    r.xp_s[:, 0:HP, :] = r.xp_s[:, TT:TT + HP, :]


def _even_conv_state(r):
    return r.xp_s[:, CONV_A_HIST_PAD - CONV_A_HIST:CONV_A_HIST_PAD, :]


def _rope_tables(pos0, T, reps):
    d = RET_HEAD_DIM
    inv_freq = ROPE_BASE ** (-np.arange(0, d, 2, dtype=np.float64) / d)
    ang = (pos0 + np.arange(T, dtype=np.float64))[:, None] * inv_freq[None, :]
    cos = np.cos(ang)
    sin = np.sin(ang)
    cos2 = np.concatenate([cos, cos], axis=-1)
    sin2 = np.concatenate([-sin, sin], axis=-1)
    return (jnp.asarray(np.tile(cos2, (reps, 1)), F32), jnp.asarray(np.tile(sin2, (reps, 1)), F32))


def _decay_tables(c, groups):
    nh = RET_HEADS
    log_gamma = np.log(1.0 - 2.0 ** (-5.0 - np.arange(nh, dtype=np.float64)))
    idx = np.arange(c, dtype=np.float64)
    rel = idx[:, None] - idx[None, :]
    dmask = np.where(rel >= 0, np.exp(np.maximum(rel, 0.0)[None] * log_gamma[:, None, None]), 0.0)
    qd = np.exp((idx + 1.0)[None, :] * log_gamma[:, None])
    kd = np.exp((c - 1.0 - idx)[None, :] * log_gamma[:, None])
    cd = np.exp(c * log_gamma)
    if groups > 1:
        eye = np.eye(groups)
        dmask = np.einsum('gk,hij->hgikj', eye, dmask).reshape(nh, groups * c, groups * c)
        qd = np.tile(qd, (1, groups))
        kd = np.tile(kd, (1, groups))
    n = groups * c
    qd = np.broadcast_to(qd[:, :, None], (nh, n, RET_HEAD_DIM))
    kd = np.broadcast_to(kd[:, :, None], (nh, n, RET_HEAD_DIM))
    cd = np.broadcast_to(cd[:, None, None], (nh, 1, RET_HEAD_DIM))
    return tuple(jnp.asarray(np.ascontiguousarray(a), F32) for a in (dmask, qd, kd, cd))


def _even_const_args(p, pos0, T, reps, sample):
    cos, sin = _rope_tables(pos0, T, reps)
    dmask, qd, kd, cd = _decay_tables(T, SAMPLE_GROUP) if sample else _decay_tables(RET_CHUNK, 1)
    return [cos, sin, dmask, qd, kd, cd, p['gmix'], p['w_in'], p['conv_w'], p['conv_wd'], p['conv_b'],
            p['ln_g'], p['ln_b'], p['gn_g'], p['w_out']]


def _even_kernel(*refs, S, TT, nt, sample):
    it = iter(refs)
    x_ref = next(it)
    r = _take(it, EVEN_CONSTS)
    convst_ref = next(it) if sample else None
    retst_ref = next(it) if sample else None
    y_ref = next(it); convout_ref = next(it); retout_ref = next(it)
    r.__dict__.update(_take(it, EVEN_SCRATCH).__dict__)
    t = pl.program_id(1)

    @pl.when(t == 0)
    def _():
        _even_init(r, retout_ref, convst_ref, retst_ref)

    for tile in _even_tiles(r, x_ref, y_ref, retout_ref, S, TT, sample):
        tile()
    _even_carry(r, TT)

    @pl.when(t == nt - 1)
    def _():
        convout_ref[...] = _even_conv_state(r)


def _even_mixer(x, conv_state, ret_state, p, *, S, TT):
    B, T, D = x.shape
    sample = conv_state is not None
    nb, nt = B // S, T // TT
    R = S * TT
    if sample:
        assert TT == T == SUBLANES and S % SAMPLE_GROUP == 0
        consts = _even_const_args(p, PAST_LEN, T, S, True)
    else:
        assert S == 1 and TT % RET_CHUNK == 0
        consts = _even_const_args(p, 0, T, 1, False)
    in_specs = [pl.BlockSpec((S, TT, D), lambda b, t: (b, t, 0)),
                pl.BlockSpec((R, RET_HEAD_DIM), lambda b, t: (t, 0)),
                pl.BlockSpec((R, RET_HEAD_DIM), lambda b, t: (t, 0))]
    in_specs += [_const_spec(c.shape) for c in consts[2:]]
    args = [x] + consts
    if sample:
        in_specs += [pl.BlockSpec((None, S, CONV_A_HIST, CONV_A_DIM), lambda b, t: (0, b, 0, 0)),
                     pl.BlockSpec((None, S, RET_HEADS, RET_HEAD_DIM, RET_HEAD_DIM),
                                  lambda b, t: (0, b, 0, 0, 0))]
        args += [conv_state, ret_state]
    out_shape = (jax.ShapeDtypeStruct((B, T, D), F32),
                 jax.ShapeDtypeStruct((B, CONV_A_HIST, CONV_A_DIM), F32),
                 jax.ShapeDtypeStruct((B, RET_HEADS, RET_HEAD_DIM, RET_HEAD_DIM), F32))
    out_specs = (pl.BlockSpec((S, TT, D), lambda b, t: (b, t, 0)),
                 pl.BlockSpec((S, CONV_A_HIST, CONV_A_DIM), lambda b, t: (b, 0, 0)),
                 pl.BlockSpec((S, RET_HEADS, RET_HEAD_DIM, RET_HEAD_DIM), lambda b, t: (b, 0, 0, 0)))
    return pl.pallas_call(
        functools.partial(_even_kernel, S=S, TT=TT, nt=nt, sample=sample),
        grid=(nb, nt), in_specs=in_specs, out_specs=out_specs, out_shape=out_shape,
        scratch_shapes=_even_scratch(S, TT, sample),
        compiler_params=pltpu.CompilerParams(dimension_semantics=("arbitrary", "arbitrary"),
                                             vmem_limit_bytes=VMEM_LIMIT_BYTES),
        name="even_mixer_sample" if sample else "even_mixer_prompt",
    )(*args)


ODD_CONSTS = ('gmix', 'w_in', 'conv_w', 'conv_b', 'w_ax', 'b_a', 'b_x', 'lam', 'w_out')
ODD_SCRATCH = ('xp_s', 'gate_s', 'xc_s', 'r_s', 'i_s', 'act_s')


def _odd_scratch(S, TT):
    R = S * TT
    return ([pltpu.VMEM((S, SUBLANES + TT, LRU_DIM), F32)]
            + [pltpu.VMEM((R, LRU_DIM), F32) for _ in range(4)]
            + [pltpu.VMEM((R, LRU_DIM), BF16)])


def _odd_init(r, h_live, conv_state, h_state):
    S = r.xp_s.shape[0]
    HP, H = SUBLANES, LRU_HIST
    r.xp_s[:, 0:HP, :] = jnp.zeros((S, HP, LRU_DIM), F32)
    if conv_state is not None:
        r.xp_s[:, HP - H:HP, :] = conv_state[...]
        h_live[...] = h_state[...]
    else:
        h_live[...] = jnp.zeros(h_live.shape, F32)


def _odd_tiles(r, x_src, y_dst, h_live, S, TT, sample):
    HP, H = SUBLANES, LRU_HIST
    if sample:
        tiles = [(0, S, 0, TT)]
    else:
        tiles = [(0, 1, r0, min(TT, ODD_SUB_ROWS)) for r0 in range(0, TT, ODD_SUB_ROWS)]
    pair = 2 * LRU_BLOCK

    def scan_group(rows, h_prev, decay, sub):
        rg = _sigmoid(r.r_s[rows, :])
        ig = _sigmoid(r.i_s[rows, :])
        a = jnp.exp2(decay * rg)
        om = jnp.maximum(1.0 - a * a, 0.0)
        root = jnp.where(om > 0.0, om * lax.rsqrt(om), 0.0)
        b = root * (ig * r.xc_s[rows, :])
        for sh in (1, 2, 4):
            keep = sub >= sh
            a_sh = jnp.where(keep, pltpu.roll(a, sh, axis=0), 1.0)
            b_sh = jnp.where(keep, pltpu.roll(b, sh, axis=0), 0.0)
            b = a * b_sh + b
            a = a * a_sh
        return a * h_prev + b

    def run(s0, ns, r0, nr):
        rows = ns * nr
        row0 = s0 * TT + r0
        nlam = -r.lam[...]
        softplus = jnp.maximum(nlam, 0.0) + jnp.log(1.0 + jnp.exp(-jnp.abs(nlam)))
        decay = jnp.broadcast_to((-LRU_C * math.log2(math.e)) * softplus, (SUBLANES, LRU_DIM))
        sub = lax.broadcasted_iota(jnp.int32, (SUBLANES, LRU_DIM), 0)
        x = x_src[s0:s0 + ns, r0:r0 + nr, :].reshape(rows, D_MODEL)
        h = _rms(x, r.gmix[...]).astype(BF16)
        r.gate_s[row0:row0 + rows, :] = _mm(h, r.w_in[:, 0:LRU_DIM])
        rec3 = _mm(h, r.w_in[:, LRU_DIM:2 * LRU_DIM]).reshape(ns, nr, LRU_DIM)
        r.xp_s[s0:s0 + ns, HP + r0:HP + r0 + nr, :] = rec3
        acc = r.conv_w[H:H + 1, :] * rec3 + r.conv_b[...]
        for j in range(H):
            lo = HP - H + j + r0
            acc = acc + r.conv_w[j:j + 1, :] * r.xp_s[s0:s0 + ns, lo:lo + nr, :]
        xc = acc.reshape(rows, LRU_DIM)
        r.xc_s[row0:row0 + rows, :] = xc
        xcb = xc.astype(BF16)
        for p in range(LRU_BLOCKS // 2):
            cs = slice(pair * p, pair * (p + 1))
            ri = _mm(xcb[:, cs], r.w_ax[p])
            r.r_s[row0:row0 + rows, cs] = ri[:, 0:pair] + r.b_a[:, cs]
            r.i_s[row0:row0 + rows, cs] = ri[:, pair:2 * pair] + r.b_x[:, cs]
        h_prev = None if sample else jnp.broadcast_to(h_live[0], (SUBLANES, LRU_DIM))
        for g0 in range(row0, row0 + rows, 2 * SUBLANES):
            parts = []
            for k in range(2):
                ga = g0 + k * SUBLANES
                if sample:
                    h_prev = jnp.broadcast_to(h_live[ga // TT], (SUBLANES, LRU_DIM))
                hs = scan_group(slice(ga, ga + SUBLANES), h_prev, decay, sub)
                h_last = hs[SUBLANES - 1:SUBLANES, :]
                if sample:
                    h_live[ga // TT] = h_last
                h_prev = jnp.broadcast_to(h_last, (SUBLANES, LRU_DIM))
                parts.append(hs)
            rows2 = slice(g0, g0 + 2 * SUBLANES)
            r.act_s[rows2, :] = (jnp.concatenate(parts, axis=0) * _gelu(r.gate_s[rows2, :])).astype(BF16)
        if not sample:
            h_live[0] = h_prev[0:1, :]
        y = _mm(r.act_s[row0:row0 + rows, :], r.w_out[...]) + x
        y_dst[s0:s0 + ns, r0:r0 + nr, :] = y.reshape(ns, nr, D_MODEL)

    return [functools.partial(run, *tl) for tl in tiles]


def _odd_carry(r, TT):
    HP = SUBLANES
    r.xp_s[:, 0:HP, :] = r.xp_s[:, TT:TT + HP, :]


def _odd_conv_state(r):
    return r.xp_s[:, SUBLANES - LRU_HIST:SUBLANES, :]


def _odd_const_args(p):
    return [p[n] for n in ODD_CONSTS]


def _odd_kernel(*refs, S, TT, nt, sample):
    it = iter(refs)
    x_ref = next(it)
    r = _take(it, ODD_CONSTS)
    convst_ref = next(it) if sample else None
    hst_ref = next(it) if sample else None
    y_ref = next(it); convout_ref = next(it); hout_ref = next(it)
    r.__dict__.update(_take(it, ODD_SCRATCH).__dict__)
    t = pl.program_id(1)

    @pl.when(t == 0)
    def _():
        _odd_init(r, hout_ref, convst_ref, hst_ref)

    for tile in _odd_tiles(r, x_ref, y_ref, hout_ref, S, TT, sample):
        tile()
    _odd_carry(r, TT)

    @pl.when(t == nt - 1)
    def _():
        convout_ref[...] = _odd_conv_state(r)


def _odd_mixer(x, conv_state, h_state, p, *, S, TT):
    B, T, D = x.shape
    sample = conv_state is not None
    nb, nt = B // S, T // TT
    consts = _odd_const_args(p)
    in_specs = [pl.BlockSpec((S, TT, D), lambda b, t: (b, t, 0))]
    in_specs += [_const_spec(c.shape) for c in consts]
    args = [x] + consts
    if sample:
        in_specs += [pl.BlockSpec((None, S, LRU_HIST, LRU_DIM), lambda b, t: (0, b, 0, 0)),
                     pl.BlockSpec((S, 1, LRU_DIM), lambda b, t: (b, 0, 0))]
        args += [conv_state, h_state]
    out_shape = (jax.ShapeDtypeStruct((B, T, D), F32),
                 jax.ShapeDtypeStruct((B, LRU_HIST, LRU_DIM), F32),
                 jax.ShapeDtypeStruct((B, 1, LRU_DIM), F32))
    out_specs = (pl.BlockSpec((S, TT, D), lambda b, t: (b, t, 0)),
                 pl.BlockSpec((S, LRU_HIST, LRU_DIM), lambda b, t: (b, 0, 0)),
                 pl.BlockSpec((S, 1, LRU_DIM), lambda b, t: (b, 0, 0)))
    return pl.pallas_call(
        functools.partial(_odd_kernel, S=S, TT=TT, nt=nt, sample=sample),
        grid=(nb, nt), in_specs=in_specs, out_specs=out_specs, out_shape=out_shape,
        scratch_shapes=_odd_scratch(S, TT),
        compiler_params=pltpu.CompilerParams(dimension_semantics=("arbitrary", "arbitrary"),
                                             vmem_limit_bytes=VMEM_LIMIT_BYTES),
        name="odd_mixer_sample" if sample else "odd_mixer_prompt",
    )(*args)


FFN_CONSTS = ('g', 'w_up', 'conv_w', 'conv_b', 'w_down')
FFN_SCRATCH = ('h_s', 'act_s', 'hist_s', 'work_s')


def _ffn_scratch(S, TT):
    R = S * TT
    return [pltpu.VMEM((R, D_MODEL), BF16),
            pltpu.VMEM((R, FFN_DIM), BF16),
            pltpu.VMEM((S, SUBLANES, 2 * FFN_DIM), F32),
            pltpu.VMEM((S, SUBLANES + TT, FFN_COL_CHUNK), F32)]


def _ffn_stages(r, x_src, y_dst, g_final, S, TT):
    R = S * TT
    HP, H, CK = SUBLANES, FFN_HIST, FFN_COL_CHUNK

    def prologue():
        x = x_src[...].reshape(R, D_MODEL)
        r.h_s[...] = _rms(x, r.g[...]).astype(BF16)

    def conv_cols(col):
        z3 = _mm(r.h_s[...], r.w_up[:, col:col + CK]).reshape(S, TT, CK)
        r.work_s[:, 0:HP, :] = r.hist_s[:, :, col:col + CK]
        r.work_s[:, HP:HP + TT, :] = z3
        zc = r.conv_w[H:H + 1, col:col + CK] * z3 + r.conv_b[:, col:col + CK]
        for j in range(H):
            zc = zc + r.conv_w[j:j + 1, col:col + CK] * r.work_s[:, HP - H + j:HP - H + j + TT, :]
        r.hist_s[:, :, col:col + CK] = r.work_s[:, TT:TT + HP, :]
        return zc.reshape(R, CK)

    def chunk(c):
        gz = conv_cols(c * CK)
        uz = conv_cols(FFN_DIM + c * CK)
        r.act_s[:, c * CK:(c + 1) * CK] = (_gelu(gz) * uz).astype(BF16)

    def epilogue():
        y = _mm(r.act_s[...], r.w_down[...]) + x_src[...].reshape(R, D_MODEL)
        if g_final is not None:
            y = _rms(y, g_final[...])
        y_dst[...] = y.reshape(S, TT, D_MODEL)

    return prologue, [functools.partial(chunk, c) for c in range(FFN_DIM // CK)], epilogue


def _ffn_state(r):
    return r.hist_s[:, SUBLANES - FFN_HIST:SUBLANES, :]


def _ffn_kernel(*refs, S, TT, nt, final):
    it = iter(refs)
    x_ref = next(it)
    r = _take(it, FFN_CONSTS)
    gfin_ref = next(it) if final else None
    y_ref = next(it); stout_ref = next(it)
    r.__dict__.update(_take(it, FFN_SCRATCH).__dict__)
    t = pl.program_id(1)

    @pl.when(t == 0)
    def _():
        r.hist_s[...] = jnp.zeros(r.hist_s.shape, F32)

    prologue, chunks, epilogue = _ffn_stages(r, x_ref, y_ref, gfin_ref, S, TT)
    prologue()
    for ch in chunks:
        ch()
    epilogue()

    @pl.when(t == nt - 1)
    def _():
        stout_ref[...] = _ffn_state(r)


def _layer_spec(shape, layer):
    nd = len(shape) - 1
    return pl.BlockSpec((None,) + tuple(shape[1:]), lambda b, t: (layer,) + (0,) * nd,
                        pipeline_mode=pl.Buffered(1))


def _conv_ffn(x, layer, p, g_final, *, S, TT):
    B, T, D = x.shape
    final = g_final is not None
    nb, nt = B // S, T // TT
    consts = [p[n] for n in FFN_CONSTS]
    in_specs = [pl.BlockSpec((S, TT, D), lambda b, t: (b, t, 0))]
    in_specs += [_layer_spec(c.shape, layer) for c in consts]
    args = [x] + consts
    if final:
        in_specs.append(_const_spec(g_final.shape))
        args.append(g_final)
    out_shape = (jax.ShapeDtypeStruct((B, T, D), F32),
                 jax.ShapeDtypeStruct((B, FFN_HIST, 2 * FFN_DIM), F32))
    out_specs = (pl.BlockSpec((S, TT, D), lambda b, t: (b, t, 0)),
                 pl.BlockSpec((S, FFN_HIST, 2 * FFN_DIM), lambda b, t: (b, 0, 0)))
    return pl.pallas_call(
        functools.partial(_ffn_kernel, S=S, TT=TT, nt=nt, final=final),
        grid=(nb, nt), in_specs=in_specs, out_specs=out_specs, out_shape=out_shape,
        scratch_shapes=_ffn_scratch(S, TT),
        compiler_params=pltpu.CompilerParams(dimension_semantics=("arbitrary", "arbitrary"),
                                             vmem_limit_bytes=VMEM_LIMIT_BYTES),
        name="ffn_prompt_final" if final else "ffn_prompt",
    )(*args)


def _ffn_cols_kernel(*refs, B, TT, nc, final):
    it = iter(refs)
    x_ref = next(it); g_ref = next(it); w_ref = next(it); cw_ref = next(it); cb_ref = next(it)
    wdn_ref = next(it)
    gfin_ref = next(it) if final else None
    st_ref = next(it)
    y_ref = next(it); stout_ref = next(it)
    h_s = next(it); gate_s = next(it); act_s = next(it)

    j = pl.program_id(0)
    R = B * TT
    CK = FFN_COL_CHUNK
    H = FFN_HIST

    @pl.when(j == 0)
    def _():
        x = x_ref[...].reshape(R, D_MODEL)
        h_s[...] = _rms(x, g_ref[...]).astype(BF16)

    tpos = lax.broadcasted_iota(jnp.int32, (B, TT, CK), 1)
    z3 = _mm(h_s[...], w_ref[...]).reshape(B, TT, CK)
    stout_ref[...] = z3[:, TT - H:TT, :]
    prev1 = st_ref[:, 1:2, :]
    prev2 = st_ref[:, 0:1, :]
    zm1 = jnp.where(tpos >= 1, pltpu.roll(z3, 1, axis=1), prev1)
    zm2 = jnp.where(tpos >= 2, pltpu.roll(z3, 2, axis=1), jnp.where(tpos == 1, prev1, prev2))
    zc = (cw_ref[2:3, :] * z3 + cw_ref[1:2, :] * zm1 + cw_ref[0:1, :] * zm2 + cb_ref[...]).reshape(R, CK)

    @pl.when(j < nc)
    def _():
        gate_s[j] = _gelu(zc)

    @pl.when(j >= nc)
    def _():
        act_s[j - nc] = (gate_s[j - nc] * zc).astype(BF16)

    @pl.when(j == 2 * nc - 1)
    def _():
        act = jnp.concatenate([act_s[k] for k in range(nc)], axis=-1)
        y = _mm(act, wdn_ref[...]) + x_ref[...].reshape(R, D_MODEL)
        if final:
            y = _rms(y, gfin_ref[...])
        y_ref[...] = y.reshape(B, TT, D_MODEL)


def _conv_ffn_sample(x, state, layer, p, g_final):
    B, T, D = x.shape
    assert T == SUBLANES and FFN_CONV_WIDTH == 3
    final = g_final is not None
    CK = FFN_COL_CHUNK
    nc = FFN_DIM // CK

    def cols(shape):
        nd = len(shape) - 2
        return pl.BlockSpec((None,) + tuple(shape[1:-1]) + (CK,), lambda j: (layer,) + (0,) * nd + (j,))

    in_specs = [pl.BlockSpec((B, T, D), lambda j: (0, 0, 0), pipeline_mode=pl.Buffered(1)),
                pl.BlockSpec((None, 1, D), lambda j: (layer, 0, 0), pipeline_mode=pl.Buffered(1)),
                cols(p['w_up'].shape), cols(p['conv_w'].shape), cols(p['conv_b'].shape),
                pl.BlockSpec((None, FFN_DIM, D), lambda j: (layer, 0, 0), pipeline_mode=pl.Buffered(1))]
    args = [x, p['g'], p['w_up'], p['conv_w'], p['conv_b'], p['w_down']]
    if final:
        in_specs.append(pl.BlockSpec(g_final.shape, lambda j: (0, 0), pipeline_mode=pl.Buffered(1)))
        args.append(g_final)
    in_specs.append(cols(state.shape))
    args.append(state)
    out_shape = (jax.ShapeDtypeStruct((B, T, D), F32),
                 jax.ShapeDtypeStruct((B, FFN_HIST, 2 * FFN_DIM), F32))
    out_specs = (pl.BlockSpec((B, T, D), lambda j: (0, 0, 0)),
                 pl.BlockSpec((B, FFN_HIST, CK), lambda j: (0, 0, j)))
    return pl.pallas_call(
        functools.partial(_ffn_cols_kernel, B=B, TT=T, nc=nc, final=final),
        grid=(2 * nc,), in_specs=in_specs, out_specs=out_specs, out_shape=out_shape,
        scratch_shapes=[pltpu.VMEM((B * T, D), BF16), pltpu.VMEM((nc, B * T, CK), F32),
                        pltpu.VMEM((nc, B * T, CK), BF16)],
        compiler_params=pltpu.CompilerParams(dimension_semantics=("arbitrary",),
                                             vmem_limit_bytes=VMEM_LIMIT_BYTES),
        name="ffn_sample_final" if final else "ffn_sample",
    )(*args)


PROMPT_TILES = dict(even=dict(S=1, TT=512), odd=dict(S=1, TT=1024), ffn=dict(S=1, TT=1024))
SAMPLE_TILES = dict(even=dict(S=16, TT=8), odd=dict(S=32, TT=8))


def _row(v):
    return v.reshape(1, -1)


def _diag_taps(w):
    half = CONV_A_DIM // 2
    w = w[np.array(CONV_A_MXU_TAPS)].reshape(len(CONV_A_MXU_TAPS), 2, 1, half)
    return (jnp.eye(half, dtype=F32)[None, None] * w).astype(BF16)


def _pair_block_diag(w_a, w_x):
    def pairs(w):
        w = w.reshape(LRU_BLOCKS // 2, 2, LRU_BLOCK, LRU_BLOCK)
        z = jnp.zeros_like(w[:, 0])
        top = jnp.concatenate([w[:, 0], z], axis=-1)
        bot = jnp.concatenate([z, w[:, 1]], axis=-1)
        return jnp.concatenate([top, bot], axis=-2)
    return jnp.concatenate([pairs(w_a), pairs(w_x)], axis=-1).astype(BF16)


def kernel(x_prompt, x_sample, state_conv_a, state_ret, state_lru_conv, state_lru_h, state_ffn_conv, norm_mix, norm_ffn, norm_final, w_in_ab, conv_a_w, conv_a_b, ln_a_g, ln_a_b, gn_ret_g, w_out_ab, w_in_c, conv_c_w, conv_c_b, w_lru_a, b_lru_a, w_lru_x, b_lru_x, lru_lambda, w_out_c, w_ffn_up, ffn_conv_w, ffn_conv_b, w_ffn_down):
    pe = dict(gmix=_row(norm_mix[0]), w_in=w_in_ab[0].astype(BF16), conv_w=conv_a_w[0],
              conv_wd=_diag_taps(conv_a_w[0]),
              conv_b=_row(conv_a_b[0]), ln_g=_row(ln_a_g[0]), ln_b=_row(ln_a_b[0]),
              gn_g=_row(gn_ret_g[0]), w_out=w_out_ab[0].astype(BF16))
    po = dict(gmix=_row(norm_mix[1]), w_in=w_in_c[0].astype(BF16), conv_w=conv_c_w[0],
              conv_b=_row(conv_c_b[0]), w_ax=_pair_block_diag(w_lru_a[0], w_lru_x[0]),
              b_a=_row(b_lru_a[0]), b_x=_row(b_lru_x[0]), lam=_row(lru_lambda[0]),
              w_out=w_out_c[0].astype(BF16))
    pf = dict(g=norm_ffn[:, None, :], w_up=w_ffn_up.astype(BF16), conv_w=ffn_conv_w,
              conv_b=ffn_conv_b[:, None, :], w_down=w_ffn_down.astype(BF16))
    g_final = _row(norm_final)

    xp, p_conv_a, p_ret = _even_mixer(x_prompt, None, None, pe, **PROMPT_TILES['even'])
    xp, p_ffn0 = _conv_ffn(xp, 0, pf, None, **PROMPT_TILES['ffn'])
    xp, p_lru_conv, p_lru_h = _odd_mixer(xp, None, None, po, **PROMPT_TILES['odd'])
    y_prompt, p_ffn1 = _conv_ffn(xp, 1, pf, g_final, **PROMPT_TILES['ffn'])

    xs, s_conv_a, s_ret = _even_mixer(x_sample, state_conv_a, state_ret, pe, **SAMPLE_TILES['even'])
    xs, s_ffn0 = _conv_ffn_sample(xs, state_ffn_conv, 0, pf, None)
    xs, s_lru_conv, s_lru_h = _odd_mixer(xs, state_lru_conv, state_lru_h[0][:, None, :], po,
                                         **SAMPLE_TILES['odd'])
    y_sample, s_ffn1 = _conv_ffn_sample(xs, state_ffn_conv, 1, pf, g_final)

    return (y_prompt, y_sample,
            p_conv_a[None], p_ret[None], p_lru_conv[None], p_lru_h[:, 0, :][None],
            jnp.stack([p_ffn0, p_ffn1]),
            s_conv_a[None], s_ret[None], s_lru_conv[None], s_lru_h[:, 0, :][None],
            jnp.stack([s_ffn0, s_ffn1]))
```

```python
import functools
import math
import types

import numpy as np
import jax
import jax.numpy as jnp
from jax import lax
from jax.experimental import pallas as pl
from jax.experimental.pallas import tpu as pltpu

F32 = jnp.float32
BF16 = jnp.bfloat16

D_MODEL = 1024
CONV_A_DIM = 512
CONV_A_WIDTH = 31
CONV_A_HIST = CONV_A_WIDTH - 1
CONV_A_HIST_PAD = 32
CONV_A_ROW_BLOCK = 64
CONV_A_SEQ_BLOCK = 4
LN_EPS = 1e-5
RET_HEADS = 4
RET_HEAD_DIM = 128
RET_DIM = RET_HEADS * RET_HEAD_DIM
RET_CHUNK = 128
ROPE_BASE = 10000.0
GN_EPS = 1e-5
IN_AB_DIM = 2 * CONV_A_DIM + 4 * RET_DIM
Z_SECTION = 512
EVEN_SUB_ROWS = 256
LRU_DIM = 1024
LRU_BLOCKS = 8
LRU_BLOCK = LRU_DIM // LRU_BLOCKS
LRU_CONV_WIDTH = 4
LRU_HIST = LRU_CONV_WIDTH - 1
LRU_C = 8.0
ODD_SUB_ROWS = 256
FFN_DIM = 2816
FFN_CONV_WIDTH = 3
FFN_HIST = FFN_CONV_WIDTH - 1
FFN_COL_CHUNK = 256
RMS_EPS = 1e-6
PAST_LEN = 16384
SUBLANES = 8
SAMPLE_GROUP = RET_CHUNK // SUBLANES

VMEM_LIMIT_BYTES = 56 * 1024 * 1024


def _rms(x, g):
    return x * lax.rsqrt(jnp.mean(x * x, axis=-1, keepdims=True) + RMS_EPS) * g


def _sigmoid(x):
    return 1.0 / (1.0 + jnp.exp(-x))


def _gelu(x):
    c = math.sqrt(2.0 / math.pi)
    return x * (0.5 + 0.5 * jnp.tanh(x * (c + (c * 0.044715) * (x * x))))


def _mm(a, b):
    return jnp.dot(a, b, preferred_element_type=F32)


def _const_spec(shape):
    nd = len(shape)
    return pl.BlockSpec(shape, lambda b, t: (0,) * nd, pipeline_mode=pl.Buffered(1))


def _take(it, names):
    return types.SimpleNamespace(**{n: next(it) for n in names})


EVEN_CONSTS = ('cos', 'sin', 'dmask', 'qd', 'kd', 'cd', 'gmix', 'w_in', 'conv_w', 'conv_wd', 'conv_b',
               'ln_g', 'ln_b', 'gn_g', 'w_out')
EVEN_SCRATCH = ('xp_s', 'xs_s', 'yb_s', 'cv_s', 'ya_s')


def _conv_a_tap(j):
    off = CONV_A_HIST_PAD - CONV_A_HIST + j
    return off % SUBLANES, off - off % SUBLANES


CONV_A_VPU_TAPS = tuple(j for j in range(CONV_A_WIDTH) if _conv_a_tap(j)[0] <= 5)
CONV_A_MXU_TAPS = tuple(j for j in range(CONV_A_WIDTH) if _conv_a_tap(j)[0] > 5)


def _even_scratch(S, TT, sample):
    return [pltpu.VMEM((S, CONV_A_HIST_PAD + TT, CONV_A_DIM), F32),
            pltpu.VMEM((SUBLANES - 1, S, TT + CONV_A_HIST_PAD - SUBLANES, CONV_A_DIM), F32),
            pltpu.VMEM((S * TT, RET_DIM), BF16),
            pltpu.VMEM((S * TT, CONV_A_DIM), F32),
            pltpu.VMEM((S * TT, CONV_A_DIM), BF16)]


def _even_init(r, ret_live, conv_state, ret_state):
    S = r.xp_s.shape[0]
    HP, H = CONV_A_HIST_PAD, CONV_A_HIST
    r.xp_s[:, 0:HP, :] = jnp.zeros((S, HP, CONV_A_DIM), F32)
    if conv_state is not None:
        r.xp_s[:, HP - H:HP, :] = conv_state[...]
        ret_live[...] = ret_state[...]
    else:
        ret_live[...] = jnp.zeros(ret_live.shape, F32)


def _even_tiles(r, x_src, y_dst, ret_live, S, TT, sample):
    HP, H = CONV_A_HIST_PAD, CONV_A_HIST
    o0 = 2 * CONV_A_DIM
    scale = RET_HEAD_DIM ** -0.5
    projected = {}

    def project(s0, ns, g0, nr):
        rows = ns * nr
        x = x_src[s0:s0 + ns, g0:g0 + nr, :].reshape(rows, D_MODEL)
        h = _rms(x, r.gmix[...]).astype(BF16)
        zs = [_mm(h, r.w_in[:, c0:c0 + Z_SECTION]) for c0 in range(0, IN_AB_DIM, Z_SECTION)]
        u = zs[0] * _sigmoid(zs[1])
        r.xp_s[s0:s0 + ns, HP + g0:HP + g0 + nr, :] = u.reshape(ns, nr, CONV_A_DIM)
        projected[(s0, g0)] = zs

    def run(s0, ns, r0, nr, g0):
        rows = ns * nr
        row0 = s0 * TT + r0
        zs = projected[(s0, g0)]
        off = (r0 - g0) * ns

        def zcols(lo, hi):
            k = lo // Z_SECTION
            return zs[k][off:off + rows, lo - k * Z_SECTION:hi - k * Z_SECTION]

        cos = r.cos[row0:row0 + rows, :]
        sin = r.sin[row0:row0 + rows, :]
        gate = zcols(o0 + 3 * RET_DIM, o0 + 4 * RET_DIM)
        o_parts = []
        for hh in range(RET_HEADS):
            lo = hh * RET_HEAD_DIM
            hi = lo + RET_HEAD_DIM
            qh = zcols(o0 + lo, o0 + hi)
            kh = zcols(o0 + RET_DIM + lo, o0 + RET_DIM + hi)
            qc = qh * cos + pltpu.roll(qh, RET_HEAD_DIM // 2, axis=1) * sin
            kc = (kh * cos + pltpu.roll(kh, RET_HEAD_DIM // 2, axis=1) * sin) * scale
            vc = zcols(o0 + 2 * RET_DIM + lo, o0 + 2 * RET_DIM + hi)
            qb = qc.astype(BF16)
            vb = vc.astype(BF16)
            scores = lax.dot_general(qb, kc.astype(BF16), (((1,), (1,)), ((), ())),
                                     preferred_element_type=F32) * r.dmask[hh]
            inner = _mm(scores.astype(BF16), vb)
            if not sample:
                kdv = (kc * r.kd[hh]).astype(BF16)
                st = ret_live[0, hh]
                cross = _mm(qb, st.astype(BF16))
                upd = lax.dot_general(kdv, vb, (((0,), (0,)), ((), ())), preferred_element_type=F32)
                ret_live[0, hh] = st * r.cd[hh] + upd
            else:
                parts = []
                for sq in range(SAMPLE_GROUP):
                    sidx = s0 + sq
                    rs = slice(sq * SUBLANES, (sq + 1) * SUBLANES)
                    st = ret_live[sidx, hh]
                    parts.append(_mm(qc[rs].astype(BF16), st.astype(BF16)))
                    kdv_s = (kc[rs] * r.kd[hh, rs, :]).astype(BF16)
                    upd = lax.dot_general(kdv_s, vc[rs].astype(BF16), (((0,), (0,)), ((), ())),
                                          preferred_element_type=F32)
                    ret_live[sidx, hh] = st * r.cd[hh] + upd
                cross = jnp.concatenate(parts, axis=0)
            o = inner + cross * r.qd[hh]
            mu_o = jnp.mean(o, axis=-1, keepdims=True)
            oc = o - mu_o
            var_o = jnp.mean(oc * oc, axis=-1, keepdims=True)
            o_parts.append(oc * lax.rsqrt(var_o + GN_EPS) * r.gn_g[:, lo:hi])

        yb = jnp.concatenate(o_parts, axis=-1) * (gate * _sigmoid(gate))
        r.yb_s[row0:row0 + rows, :] = yb.astype(BF16)

    def conv_and_project():
        R = S * TT
        L = TT + HP - SUBLANES
        for b in range(1, SUBLANES):
            r.xs_s[b - 1] = r.xp_s[:, b:b + L, :]
        half = CONV_A_DIM // 2
        if sample:
            blocks = [(sb, CONV_A_SEQ_BLOCK, 0, TT) for sb in range(0, S, CONV_A_SEQ_BLOCK)]
        else:
            blocks = [(0, 1, rb, CONV_A_ROW_BLOCK) for rb in range(0, TT, CONV_A_ROW_BLOCK)]
        for sb, nsb, rb, nrb in blocks:
            rowb = sb * TT + rb
            acc = jnp.zeros((nsb, nrb, CONV_A_DIM), F32) + r.conv_b[...]
            for j in CONV_A_VPU_TAPS:
                b, lo = _conv_a_tap(j)
                if b == 0:
                    win = r.xp_s[sb:sb + nsb, lo + rb:lo + rb + nrb, :]
                else:
                    win = r.xs_s[b - 1, sb:sb + nsb, lo + rb:lo + rb + nrb, :]
                acc = acc + r.conv_w[j:j + 1, :] * win
            r.cv_s[rowb:rowb + nsb * nrb, :] = acc.reshape(nsb * nrb, CONV_A_DIM)
        mxu_parts = []
        for p in range(2):
            cs = slice(p * half, (p + 1) * half)
            acc = jnp.zeros((R, half), F32)
            for k, j in enumerate(CONV_A_MXU_TAPS):
                b, lo = _conv_a_tap(j)
                win = r.xs_s[b - 1, :, lo:lo + TT, cs]
                acc = acc + _mm(win.reshape(R, half).astype(BF16), r.conv_wd[k, p])
            mxu_parts.append(acc)
        cv_mxu = jnp.concatenate(mxu_parts, axis=-1)
        for sb, nsb, rb, nrb in blocks:
            rowb = sb * TT + rb
            rows_b = slice(rowb, rowb + nsb * nrb)
            cv = r.cv_s[rows_b, :] + cv_mxu[rows_b, :]
            mu = jnp.mean(cv, axis=-1, keepdims=True)
            cvc = cv - mu
            var = jnp.mean(cvc * cvc, axis=-1, keepdims=True)
            ln = cvc * lax.rsqrt(var + LN_EPS) * r.ln_g[...] + r.ln_b[...]
            r.ya_s[rowb:rowb + nsb * nrb, :] = (ln * _sigmoid(ln)).astype(BF16)
        y = (_mm(r.ya_s[...], r.w_out[0:CONV_A_DIM, :])
             + _mm(r.yb_s[...], r.w_out[CONV_A_DIM:CONV_A_DIM + RET_DIM, :])
             + x_src[...].reshape(R, D_MODEL))
        y_dst[...] = y.reshape(S, TT, D_MODEL)

    assert Z_SECTION == CONV_A_DIM
    stages = []
    if sample:
        for s0 in range(0, S, SAMPLE_GROUP):
            stages.append(functools.partial(project, s0, SAMPLE_GROUP, 0, TT))
            stages.append(functools.partial(run, s0, SAMPLE_GROUP, 0, TT, 0))
    else:
        for g0 in range(0, TT, EVEN_SUB_ROWS):
            stages.append(functools.partial(project, 0, 1, g0, EVEN_SUB_ROWS))
            for r0 in range(g0, g0 + EVEN_SUB_ROWS, RET_CHUNK):
                stages.append(functools.partial(run, 0, 1, r0, RET_CHUNK, g0))
    return stages + [conv_and_project]


def _even_carry(r, TT):
    HP = CONV_A_HIST_PAD
    r.xp_s[:, 0:HP, :] = r.xp_s[:, TT:TT + HP, :]


def _even_conv_state(r):
    return r.xp_s[:, CONV_A_HIST_PAD - CONV_A_HIST:CONV_A_HIST_PAD, :]


def _rope_tables(pos0, T, reps):
    d = RET_HEAD_DIM
    inv_freq = ROPE_BASE ** (-np.arange(0, d, 2, dtype=np.float64) / d)
    ang = (pos0 + np.arange(T, dtype=np.float64))[:, None] * inv_freq[None, :]
    cos = np.cos(ang)
    sin = np.sin(ang)
    cos2 = np.concatenate([cos, cos], axis=-1)
    sin2 = np.concatenate([-sin, sin], axis=-1)
    return (jnp.asarray(np.tile(cos2, (reps, 1)), F32), jnp.asarray(np.tile(sin2, (reps, 1)), F32))


def _decay_tables(c, groups):
    nh = RET_HEADS
    log_gamma = np.log(1.0 - 2.0 ** (-5.0 - np.arange(nh, dtype=np.float64)))
    idx = np.arange(c, dtype=np.float64)
    rel = idx[:, None] - idx[None, :]
    dmask = np.where(rel >= 0, np.exp(np.maximum(rel, 0.0)[None] * log_gamma[:, None, None]), 0.0)
    qd = np.exp((idx + 1.0)[None, :] * log_gamma[:, None])
    kd = np.exp((c - 1.0 - idx)[None, :] * log_gamma[:, None])
    cd = np.exp(c * log_gamma)
    if groups > 1:
        eye = np.eye(groups)
        dmask = np.einsum('gk,hij->hgikj', eye, dmask).reshape(nh, groups * c, groups * c)
        qd = np.tile(qd, (1, groups))
        kd = np.tile(kd, (1, groups))
    n = groups * c
    qd = np.broadcast_to(qd[:, :, None], (nh, n, RET_HEAD_DIM))
    kd = np.broadcast_to(kd[:, :, None], (nh, n, RET_HEAD_DIM))
    cd = np.broadcast_to(cd[:, None, None], (nh, 1, RET_HEAD_DIM))
    return tuple(jnp.asarray(np.ascontiguousarray(a), F32) for a in (dmask, qd, kd, cd))


def _even_const_args(p, pos0, T, reps, sample):
    cos, sin = _rope_tables(pos0, T, reps)
    dmask, qd, kd, cd = _decay_tables(T, SAMPLE_GROUP) if sample else _decay_tables(RET_CHUNK, 1)
    return [cos, sin, dmask, qd, kd, cd, p['gmix'], p['w_in'], p['conv_w'], p['conv_wd'], p['conv_b'],
            p['ln_g'], p['ln_b'], p['gn_g'], p['w_out']]


def _even_kernel(*refs, S, TT, nt, sample):
    it = iter(refs)
    x_ref = next(it)
    r = _take(it, EVEN_CONSTS)
    convst_ref = next(it) if sample else None
    retst_ref = next(it) if sample else None
    y_ref = next(it); convout_ref = next(it); retout_ref = next(it)
    r.__dict__.update(_take(it, EVEN_SCRATCH).__dict__)
    t = pl.program_id(1)

    @pl.when(t == 0)
    def _():
        _even_init(r, retout_ref, convst_ref, retst_ref)

    for tile in _even_tiles(r, x_ref, y_ref, retout_ref, S, TT, sample):
        tile()
    _even_carry(r, TT)

    @pl.when(t == nt - 1)
    def _():
        convout_ref[...] = _even_conv_state(r)


def _even_mixer(x, conv_state, ret_state, p, *, S, TT):
    B, T, D = x.shape
    sample = conv_state is not None
    nb, nt = B // S, T // TT
    R = S * TT
    if sample:
        assert TT == T == SUBLANES and S % SAMPLE_GROUP == 0
        consts = _even_const_args(p, PAST_LEN, T, S, True)
    else:
        assert S == 1 and TT % RET_CHUNK == 0
        consts = _even_const_args(p, 0, T, 1, False)
    in_specs = [pl.BlockSpec((S, TT, D), lambda b, t: (b, t, 0)),
                pl.BlockSpec((R, RET_HEAD_DIM), lambda b, t: (t, 0)),
                pl.BlockSpec((R, RET_HEAD_DIM), lambda b, t: (t, 0))]
    in_specs += [_const_spec(c.shape) for c in consts[2:]]
    args = [x] + consts
    if sample:
        in_specs += [pl.BlockSpec((None, S, CONV_A_HIST, CONV_A_DIM), lambda b, t: (0, b, 0, 0)),
                     pl.BlockSpec((None, S, RET_HEADS, RET_HEAD_DIM, RET_HEAD_DIM),
                                  lambda b, t: (0, b, 0, 0, 0))]
        args += [conv_state, ret_state]
    out_shape = (jax.ShapeDtypeStruct((B, T, D), F32),
                 jax.ShapeDtypeStruct((B, CONV_A_HIST, CONV_A_DIM), F32),
                 jax.ShapeDtypeStruct((B, RET_HEADS, RET_HEAD_DIM, RET_HEAD_DIM), F32))
    out_specs = (pl.BlockSpec((S, TT, D), lambda b, t: (b, t, 0)),
                 pl.BlockSpec((S, CONV_A_HIST, CONV_A_DIM), lambda b, t: (b, 0, 0)),
                 pl.BlockSpec((S, RET_HEADS, RET_HEAD_DIM, RET_HEAD_DIM), lambda b, t: (b, 0, 0, 0)))
    return pl.pallas_call(
        functools.partial(_even_kernel, S=S, TT=TT, nt=nt, sample=sample),
        grid=(nb, nt), in_specs=in_specs, out_specs=out_specs, out_shape=out_shape,
        scratch_shapes=_even_scratch(S, TT, sample),
        compiler_params=pltpu.CompilerParams(dimension_semantics=("arbitrary", "arbitrary"),
                                             vmem_limit_bytes=VMEM_LIMIT_BYTES),
        name="even_mixer_sample" if sample else "even_mixer_prompt",
    )(*args)


ODD_CONSTS = ('gmix', 'w_in', 'conv_w', 'conv_b', 'w_ax', 'b_a', 'b_x', 'lam', 'w_out')
ODD_SCRATCH = ('xp_s', 'gate_s', 'xc_s', 'r_s', 'i_s', 'act_s')


def _odd_scratch(S, TT):
    R = S * TT
    return ([pltpu.VMEM((S, SUBLANES + TT, LRU_DIM), F32)]
            + [pltpu.VMEM((R, LRU_DIM), F32) for _ in range(4)]
            + [pltpu.VMEM((R, LRU_DIM), BF16)])


def _odd_init(r, h_live, conv_state, h_state):
    S = r.xp_s.shape[0]
    HP, H = SUBLANES, LRU_HIST
    r.xp_s[:, 0:HP, :] = jnp.zeros((S, HP, LRU_DIM), F32)
    if conv_state is not None:
        r.xp_s[:, HP - H:HP, :] = conv_state[...]
        h_live[...] = h_state[...]
    else:
        h_live[...] = jnp.zeros(h_live.shape, F32)


def _odd_tiles(r, x_src, y_dst, h_live, S, TT, sample):
    HP, H = SUBLANES, LRU_HIST
    if sample:
        tiles = [(0, S, 0, TT)]
    else:
        tiles = [(0, 1, r0, min(TT, ODD_SUB_ROWS)) for r0 in range(0, TT, ODD_SUB_ROWS)]
    pair = 2 * LRU_BLOCK

    def scan_group(rows, h_prev, decay, sub):
        rg = _sigmoid(r.r_s[rows, :])
        ig = _sigmoid(r.i_s[rows, :])
        a = jnp.exp2(decay * rg)
        om = jnp.maximum(1.0 - a * a, 0.0)
        root = jnp.where(om > 0.0, om * lax.rsqrt(om), 0.0)
        b = root * (ig * r.xc_s[rows, :])
        for sh in (1, 2, 4):
            keep = sub >= sh
            a_sh = jnp.where(keep, pltpu.roll(a, sh, axis=0), 1.0)
            b_sh = jnp.where(keep, pltpu.roll(b, sh, axis=0), 0.0)
            b = a * b_sh + b
            a = a * a_sh
        return a * h_prev + b

    def run(s0, ns, r0, nr):
        rows = ns * nr
        row0 = s0 * TT + r0
        nlam = -r.lam[...]
        softplus = jnp.maximum(nlam, 0.0) + jnp.log(1.0 + jnp.exp(-jnp.abs(nlam)))
        decay = jnp.broadcast_to((-LRU_C * math.log2(math.e)) * softplus, (SUBLANES, LRU_DIM))
        sub = lax.broadcasted_iota(jnp.int32, (SUBLANES, LRU_DIM), 0)
        x = x_src[s0:s0 + ns, r0:r0 + nr, :].reshape(rows, D_MODEL)
        h = _rms(x, r.gmix[...]).astype(BF16)
        r.gate_s[row0:row0 + rows, :] = _mm(h, r.w_in[:, 0:LRU_DIM])
        rec3 = _mm(h, r.w_in[:, LRU_DIM:2 * LRU_DIM]).reshape(ns, nr, LRU_DIM)
        r.xp_s[s0:s0 + ns, HP + r0:HP + r0 + nr, :] = rec3
        acc = r.conv_w[H:H + 1, :] * rec3 + r.conv_b[...]
        for j in range(H):
            lo = HP - H + j + r0
            acc = acc + r.conv_w[j:j + 1, :] * r.xp_s[s0:s0 + ns, lo:lo + nr, :]
        xc = acc.reshape(rows, LRU_DIM)
        r.xc_s[row0:row0 + rows, :] = xc
        xcb = xc.astype(BF16)
        for p in range(LRU_BLOCKS // 2):
            cs = slice(pair * p, pair * (p + 1))
            ri = _mm(xcb[:, cs], r.w_ax[p])
            r.r_s[row0:row0 + rows, cs] = ri[:, 0:pair] + r.b_a[:, cs]
            r.i_s[row0:row0 + rows, cs] = ri[:, pair:2 * pair] + r.b_x[:, cs]
        h_prev = None if sample else jnp.broadcast_to(h_live[0], (SUBLANES, LRU_DIM))
        for g0 in range(row0, row0 + rows, 2 * SUBLANES):
            parts = []
            for k in range(2):
                ga = g0 + k * SUBLANES
                if sample:
                    h_prev = jnp.broadcast_to(h_live[ga // TT], (SUBLANES, LRU_DIM))
                hs = scan_group(slice(ga, ga + SUBLANES), h_prev, decay, sub)
                h_last = hs[SUBLANES - 1:SUBLANES, :]
                if sample:
                    h_live[ga // TT] = h_last
                h_prev = jnp.broadcast_to(h_last, (SUBLANES, LRU_DIM))
                parts.append(hs)
            rows2 = slice(g0, g0 + 2 * SUBLANES)
            r.act_s[rows2, :] = (jnp.concatenate(parts, axis=0) * _gelu(r.gate_s[rows2, :])).astype(BF16)
        if not sample:
            h_live[0] = h_prev[0:1, :]
        y = _mm(r.act_s[row0:row0 + rows, :], r.w_out[...]) + x
        y_dst[s0:s0 + ns, r0:r0 + nr, :] = y.reshape(ns, nr, D_MODEL)

    return [functools.partial(run, *tl) for tl in tiles]


def _odd_carry(r, TT):
    HP = SUBLANES
    r.xp_s[:, 0:HP, :] = r.xp_s[:, TT:TT + HP, :]


def _odd_conv_state(r):
    return r.xp_s[:, SUBLANES - LRU_HIST:SUBLANES, :]


def _odd_const_args(p):
    return [p[n] for n in ODD_CONSTS]


def _odd_kernel(*refs, S, TT, nt, sample):
    it = iter(refs)
    x_ref = next(it)
    r = _take(it, ODD_CONSTS)
    convst_ref = next(it) if sample else None
    hst_ref = next(it) if sample else None
    y_ref = next(it); convout_ref = next(it); hout_ref = next(it)
    r.__dict__.update(_take(it, ODD_SCRATCH).__dict__)
    t = pl.program_id(1)

    @pl.when(t == 0)
    def _():
        _odd_init(r, hout_ref, convst_ref, hst_ref)

    for tile in _odd_tiles(r, x_ref, y_ref, hout_ref, S, TT, sample):
        tile()
    _odd_carry(r, TT)

    @pl.when(t == nt - 1)
    def _():
        convout_ref[...] = _odd_conv_state(r)


def _odd_mixer(x, conv_state, h_state, p, *, S, TT):
    B, T, D = x.shape
    sample = conv_state is not None
    nb, nt = B // S, T // TT
    consts = _odd_const_args(p)
    in_specs = [pl.BlockSpec((S, TT, D), lambda b, t: (b, t, 0))]
    in_specs += [_const_spec(c.shape) for c in consts]
    args = [x] + consts
    if sample:
        in_specs += [pl.BlockSpec((None, S, LRU_HIST, LRU_DIM), lambda b, t: (0, b, 0, 0)),
                     pl.BlockSpec((S, 1, LRU_DIM), lambda b, t: (b, 0, 0))]
        args += [conv_state, h_state]
    out_shape = (jax.ShapeDtypeStruct((B, T, D), F32),
                 jax.ShapeDtypeStruct((B, LRU_HIST, LRU_DIM), F32),
                 jax.ShapeDtypeStruct((B, 1, LRU_DIM), F32))
    out_specs = (pl.BlockSpec((S, TT, D), lambda b, t: (b, t, 0)),
                 pl.BlockSpec((S, LRU_HIST, LRU_DIM), lambda b, t: (b, 0, 0)),
                 pl.BlockSpec((S, 1, LRU_DIM), lambda b, t: (b, 0, 0)))
    return pl.pallas_call(
        functools.partial(_odd_kernel, S=S, TT=TT, nt=nt, sample=sample),
        grid=(nb, nt), in_specs=in_specs, out_specs=out_specs, out_shape=out_shape,
        scratch_shapes=_odd_scratch(S, TT),
        compiler_params=pltpu.CompilerParams(dimension_semantics=("arbitrary", "arbitrary"),
                                             vmem_limit_bytes=VMEM_LIMIT_BYTES),
        name="odd_mixer_sample" if sample else "odd_mixer_prompt",
    )(*args)


FFN_CONSTS = ('g', 'w_up', 'conv_w', 'conv_b', 'w_down')
FFN_SCRATCH = ('h_s', 'act_s', 'hist_s', 'work_s')


def _ffn_scratch(S, TT):
    R = S * TT
    return [pltpu.VMEM((R, D_MODEL), BF16),
            pltpu.VMEM((R, FFN_DIM), BF16),
            pltpu.VMEM((S, SUBLANES, 2 * FFN_DIM), F32),
            pltpu.VMEM((S, SUBLANES + TT, FFN_COL_CHUNK), F32)]


def _ffn_stages(r, x_src, y_dst, g_final, S, TT):
    R = S * TT
    HP, H, CK = SUBLANES, FFN_HIST, FFN_COL_CHUNK

    def prologue():
        x = x_src[...].reshape(R, D_MODEL)
        r.h_s[...] = _rms(x, r.g[...]).astype(BF16)

    def conv_cols(col):
        z3 = _mm(r.h_s[...], r.w_up[:, col:col + CK]).reshape(S, TT, CK)
        r.work_s[:, 0:HP, :] = r.hist_s[:, :, col:col + CK]
        r.work_s[:, HP:HP + TT, :] = z3
        zc = r.conv_w[H:H + 1, col:col + CK] * z3 + r.conv_b[:, col:col + CK]
        for j in range(H):
            zc = zc + r.conv_w[j:j + 1, col:col + CK] * r.work_s[:, HP - H + j:HP - H + j + TT, :]
        r.hist_s[:, :, col:col + CK] = r.work_s[:, TT:TT + HP, :]
        return zc.reshape(R, CK)

    def chunk(c):
        gz = conv_cols(c * CK)
        uz = conv_cols(FFN_DIM + c * CK)
        r.act_s[:, c * CK:(c + 1) * CK] = (_gelu(gz) * uz).astype(BF16)

    def epilogue():
        y = _mm(r.act_s[...], r.w_down[...]) + x_src[...].reshape(R, D_MODEL)
        if g_final is not None:
            y = _rms(y, g_final[...])
        y_dst[...] = y.reshape(S, TT, D_MODEL)

    return prologue, [functools.partial(chunk, c) for c in range(FFN_DIM // CK)], epilogue


def _ffn_state(r):
    return r.hist_s[:, SUBLANES - FFN_HIST:SUBLANES, :]


def _ffn_kernel(*refs, S, TT, nt, final):
    it = iter(refs)
    x_ref = next(it)
    r = _take(it, FFN_CONSTS)
    gfin_ref = next(it) if final else None
    y_ref = next(it); stout_ref = next(it)
    r.__dict__.update(_take(it, FFN_SCRATCH).__dict__)
    t = pl.program_id(1)

    @pl.when(t == 0)
    def _():
        r.hist_s[...] = jnp.zeros(r.hist_s.shape, F32)

    prologue, chunks, epilogue = _ffn_stages(r, x_ref, y_ref, gfin_ref, S, TT)
    prologue()
    for ch in chunks:
        ch()
    epilogue()

    @pl.when(t == nt - 1)
    def _():
        stout_ref[...] = _ffn_state(r)


def _layer_spec(shape, layer):
    nd = len(shape) - 1
    return pl.BlockSpec((None,) + tuple(shape[1:]), lambda b, t: (layer,) + (0,) * nd,
                        pipeline_mode=pl.Buffered(1))


def _conv_ffn(x, layer, p, g_final, *, S, TT):
    B, T, D = x.shape
    final = g_final is not None
    nb, nt = B // S, T // TT
    consts = [p[n] for n in FFN_CONSTS]
    in_specs = [pl.BlockSpec((S, TT, D), lambda b, t: (b, t, 0))]
    in_specs += [_layer_spec(c.shape, layer) for c in consts]
    args = [x] + consts
    if final:
        in_specs.append(_const_spec(g_final.shape))
        args.append(g_final)
    out_shape = (jax.ShapeDtypeStruct((B, T, D), F32),
                 jax.ShapeDtypeStruct((B, FFN_HIST, 2 * FFN_DIM), F32))
    out_specs = (pl.BlockSpec((S, TT, D), lambda b, t: (b, t, 0)),
                 pl.BlockSpec((S, FFN_HIST, 2 * FFN_DIM), lambda b, t: (b, 0, 0)))
    return pl.pallas_call(
        functools.partial(_ffn_kernel, S=S, TT=TT, nt=nt, final=final),
        grid=(nb, nt), in_specs=in_specs, out_specs=out_specs, out_shape=out_shape,
        scratch_shapes=_ffn_scratch(S, TT),
        compiler_params=pltpu.CompilerParams(dimension_semantics=("arbitrary", "arbitrary"),
                                             vmem_limit_bytes=VMEM_LIMIT_BYTES),
        name="ffn_prompt_final" if final else "ffn_prompt",
    )(*args)


def _ffn_cols_kernel(*refs, B, TT, nc, final):
    it = iter(refs)
    x_ref = next(it); g_ref = next(it); w_ref = next(it); cw_ref = next(it); cb_ref = next(it)
    wdn_ref = next(it)
    gfin_ref = next(it) if final else None
    st_ref = next(it)
    y_ref = next(it); stout_ref = next(it)
    h_s = next(it); gate_s = next(it); act_s = next(it)

    j = pl.program_id(0)
    R = B * TT
    CK = FFN_COL_CHUNK
    H = FFN_HIST

    @pl.when(j == 0)
    def _():
        x = x_ref[...].reshape(R, D_MODEL)
        h_s[...] = _rms(x, g_ref[...]).astype(BF16)

    tpos = lax.broadcasted_iota(jnp.int32, (B, TT, CK), 1)
    z3 = _mm(h_s[...], w_ref[...]).reshape(B, TT, CK)
    stout_ref[...] = z3[:, TT - H:TT, :]
    prev1 = st_ref[:, 1:2, :]
    prev2 = st_ref[:, 0:1, :]
    zm1 = jnp.where(tpos >= 1, pltpu.roll(z3, 1, axis=1), prev1)
    zm2 = jnp.where(tpos >= 2, pltpu.roll(z3, 2, axis=1), jnp.where(tpos == 1, prev1, prev2))
    zc = (cw_ref[2:3, :] * z3 + cw_ref[1:2, :] * zm1 + cw_ref[0:1, :] * zm2 + cb_ref[...]).reshape(R, CK)

    @pl.when(j < nc)
    def _():
        gate_s[j] = _gelu(zc)

    @pl.when(j >= nc)
    def _():
        act_s[j - nc] = (gate_s[j - nc] * zc).astype(BF16)

    @pl.when(j == 2 * nc - 1)
    def _():
        act = jnp.concatenate([act_s[k] for k in range(nc)], axis=-1)
        y = _mm(act, wdn_ref[...]) + x_ref[...].reshape(R, D_MODEL)
        if final:
            y = _rms(y, gfin_ref[...])
        y_ref[...] = y.reshape(B, TT, D_MODEL)


def _conv_ffn_sample(x, state, layer, p, g_final):
    B, T, D = x.shape
    assert T == SUBLANES and FFN_CONV_WIDTH == 3
    final = g_final is not None
    CK = FFN_COL_CHUNK
    nc = FFN_DIM // CK

    def cols(shape):
        nd = len(shape) - 2
        return pl.BlockSpec((None,) + tuple(shape[1:-1]) + (CK,), lambda j: (layer,) + (0,) * nd + (j,))

    in_specs = [pl.BlockSpec((B, T, D), lambda j: (0, 0, 0), pipeline_mode=pl.Buffered(1)),
                pl.BlockSpec((None, 1, D), lambda j: (layer, 0, 0), pipeline_mode=pl.Buffered(1)),
                cols(p['w_up'].shape), cols(p['conv_w'].shape), cols(p['conv_b'].shape),
                pl.BlockSpec((None, FFN_DIM, D), lambda j: (layer, 0, 0), pipeline_mode=pl.Buffered(1))]
    args = [x, p['g'], p['w_up'], p['conv_w'], p['conv_b'], p['w_down']]
    if final:
        in_specs.append(pl.BlockSpec(g_final.shape, lambda j: (0, 0), pipeline_mode=pl.Buffered(1)))
        args.append(g_final)
    in_specs.append(cols(state.shape))
    args.append(state)
    out_shape = (jax.ShapeDtypeStruct((B, T, D), F32),
                 jax.ShapeDtypeStruct((B, FFN_HIST, 2 * FFN_DIM), F32))
    out_specs = (pl.BlockSpec((B, T, D), lambda j: (0, 0, 0)),
                 pl.BlockSpec((B, FFN_HIST, CK), lambda j: (0, 0, j)))
    return pl.pallas_call(
        functools.partial(_ffn_cols_kernel, B=B, TT=T, nc=nc, final=final),
        grid=(2 * nc,), in_specs=in_specs, out_specs=out_specs, out_shape=out_shape,
        scratch_shapes=[pltpu.VMEM((B * T, D), BF16), pltpu.VMEM((nc, B * T, CK), F32),
                        pltpu.VMEM((nc, B * T, CK), BF16)],
        compiler_params=pltpu.CompilerParams(dimension_semantics=("arbitrary",),
                                             vmem_limit_bytes=VMEM_LIMIT_BYTES),
        name="ffn_sample_final" if final else "ffn_sample",
    )(*args)


PROMPT_TILES = dict(even=dict(S=1, TT=512), odd=dict(S=1, TT=1024), ffn=dict(S=1, TT=1024))
SAMPLE_TILES = dict(even=dict(S=16, TT=8), odd=dict(S=32, TT=8))


def _row(v):
    return v.reshape(1, -1)


def _diag_taps(w):
    half = CONV_A_DIM // 2
    w = w[np.array(CONV_A_MXU_TAPS)].reshape(len(CONV_A_MXU_TAPS), 2, 1, half)
    return (jnp.eye(half, dtype=F32)[None, None] * w).astype(BF16)


def _pair_block_diag(w_a, w_x):
    def pairs(w):
        w = w.reshape(LRU_BLOCKS // 2, 2, LRU_BLOCK, LRU_BLOCK)
        z = jnp.zeros_like(w[:, 0])
        top = jnp.concatenate([w[:, 0], z], axis=-1)
        bot = jnp.concatenate([z, w[:, 1]], axis=-1)
        return jnp.concatenate([top, bot], axis=-2)
    return jnp.concatenate([pairs(w_a), pairs(w_x)], axis=-1).astype(BF16)


def kernel(x_prompt, x_sample, state_conv_a, state_ret, state_lru_conv, state_lru_h, state_ffn_conv, norm_mix, norm_ffn, norm_final, w_in_ab, conv_a_w, conv_a_b, ln_a_g, ln_a_b, gn_ret_g, w_out_ab, w_in_c, conv_c_w, conv_c_b, w_lru_a, b_lru_a, w_lru_x, b_lru_x, lru_lambda, w_out_c, w_ffn_up, ffn_conv_w, ffn_conv_b, w_ffn_down):
    pe = dict(gmix=_row(norm_mix[0]), w_in=w_in_ab[0].astype(BF16), conv_w=conv_a_w[0],
              conv_wd=_diag_taps(conv_a_w[0]),
              conv_b=_row(conv_a_b[0]), ln_g=_row(ln_a_g[0]), ln_b=_row(ln_a_b[0]),
              gn_g=_row(gn_ret_g[0]), w_out=w_out_ab[0].astype(BF16))
    po = dict(gmix=_row(norm_mix[1]), w_in=w_in_c[0].astype(BF16), conv_w=conv_c_w[0],
              conv_b=_row(conv_c_b[0]), w_ax=_pair_block_diag(w_lru_a[0], w_lru_x[0]),
              b_a=_row(b_lru_a[0]), b_x=_row(b_lru_x[0]), lam=_row(lru_lambda[0]),
              w_out=w_out_c[0].astype(BF16))
    pf = dict(g=norm_ffn[:, None, :], w_up=w_ffn_up.astype(BF16), conv_w=ffn_conv_w,
              conv_b=ffn_conv_b[:, None, :], w_down=w_ffn_down.astype(BF16))
    g_final = _row(norm_final)

    xp, p_conv_a, p_ret = _even_mixer(x_prompt, None, None, pe, **PROMPT_TILES['even'])
    xp, p_ffn0 = _conv_ffn(xp, 0, pf, None, **PROMPT_TILES['ffn'])
    xp, p_lru_conv, p_lru_h = _odd_mixer(xp, None, None, po, **PROMPT_TILES['odd'])
    y_prompt, p_ffn1 = _conv_ffn(xp, 1, pf, g_final, **PROMPT_TILES['ffn'])

    xs, s_conv_a, s_ret = _even_mixer(x_sample, state_conv_a, state_ret, pe, **SAMPLE_TILES['even'])
    xs, s_ffn0 = _conv_ffn_sample(xs, state_ffn_conv, 0, pf, None)
    xs, s_lru_conv, s_lru_h = _odd_mixer(xs, state_lru_conv, state_lru_h[0][:, None, :], po,
                                         **SAMPLE_TILES['odd'])
    y_sample, s_ffn1 = _conv_ffn_sample(xs, state_ffn_conv, 1, pf, g_final)

    return (y_prompt, y_sample,
            p_conv_a[None], p_ret[None], p_lru_conv[None], p_lru_h[:, 0, :][None],
            jnp.stack([p_ffn0, p_ffn1]),
            s_conv_a[None], s_ret[None], s_lru_conv[None], s_lru_h[:, 0, :][None],
            jnp.stack([s_ffn0, s_ffn1]))
```
